```python
import jax, jax.numpy as jnp
from jax import lax
import numpy as np

D_MODEL = 1024
BATCH = 16
SEQ = 4096
DEPTH = 4
DEC_BATCH = 8
DEC_SEQ = 64
PAST_LEN = 1024

CHUNK = 64
N_META = 16
N_AB = (DEPTH + 1) // 2
N_POOL = DEPTH // 2
A_HEADS = 4
A_DK = 128
A_DV = 128
A_KW = A_HEADS * A_DK
A_VW = A_HEADS * A_DV
B_HEADS = 4
B_DH = 128
B_QW = B_HEADS * B_DH
IDX_HEADS = 4
IDX_DIM = 64
TOPK_MAX = 256
Q_BLOCK = 128
MIX_W = A_VW + B_QW
IN_SPLITS = (A_KW, A_KW, A_VW, A_VW, B_QW, B_DH, B_DH, IDX_HEADS * IDX_DIM, IDX_DIM, IDX_HEADS)
IN_COLS = sum(IN_SPLITS)
ROPE_THETA = 500000.0
ROT_FRAC = 4
POOL_WINDOWS = (2, 4, 8, 16)
POOL_GROUP = D_MODEL // len(POOL_WINDOWS)
POOL_HIST = max(POOL_WINDOWS) - 1
D_FF = 2816
RMS_EPS = 1e-6
NEG_BIG = -1e30

kernel_name = 'hybrid_streaming_hgrn2_dsa_pool_step'


def rms_norm(x, g):
    xf = x.astype(jnp.float32)
    y = xf * lax.rsqrt(jnp.mean(xf * xf, axis=-1, keepdims=True) + RMS_EPS)
    return (y * g.astype(jnp.float32)).astype(x.dtype)


def swiglu(u, wg, wu, wd):
    return (jax.nn.silu(u @ wg) * (u @ wu)) @ wd


def rope_partial(x, pos):
    d = x.shape[-1]
    r = d // ROT_FRAC
    half = r // 2
    inv = ROPE_THETA ** (-jnp.arange(half, dtype=jnp.float32) / half)
    ang = pos.astype(jnp.float32)[:, None] * inv[None, :]
    cos = jnp.cos(ang)[None, :, None, :]
    sin = jnp.sin(ang)[None, :, None, :]
    xf = x.astype(jnp.float32)
    x1, x2, rest = xf[..., :half], xf[..., half:r], xf[..., r:]
    return jnp.concatenate([x1 * cos - x2 * sin, x2 * cos + x1 * sin, rest], axis=-1).astype(x.dtype)


def hgrn2_recurrence(q, k, v, log_f, s0):
    B, H, T, _ = q.shape
    nc = -(-T // CHUNK)
    pad = nc * CHUNK - T

    def blk(a):
        a = jnp.pad(a, ((0, 0), (0, 0), (0, pad), (0, 0)))
        return a.reshape(B, H, nc, CHUNK, a.shape[-1]).transpose(2, 0, 1, 3, 4)

    causal = jnp.tril(jnp.ones((CHUNK, CHUNK), dtype=bool))[:, :, None]

    def step(s, inp):
        qc, kc, vc, gc = inp
        b = jnp.cumsum(gc, axis=2)
        o_inter = jnp.einsum('bhtk,bhkv->bhtv', qc * jnp.exp(b), s)
        diff = b[:, :, :, None, :] - b[:, :, None, :, :]
        decay = jnp.where(causal, jnp.exp(jnp.where(causal, diff, 0.0)), 0.0)
        att = jnp.einsum('bhtk,bhsk,bhtsk->bhts', qc, kc, decay)
        o = o_inter + jnp.einsum('bhts,bhsv->bhtv', att, vc)
        b_last = b[:, :, -1:, :]
        s_new = jnp.exp(b_last[:, :, 0, :])[..., None] * s + jnp.einsum(
            'bhsk,bhsv->bhkv', kc * jnp.exp(b_last - b), vc)
        return s_new, o

    s_fin, o = lax.scan(step, s0, (blk(q), blk(k), blk(v), blk(log_f)))
    o = o.transpose(1, 2, 0, 3, 4).reshape(B, H, nc * CHUNK, -1)[:, :, :T]
    return o, s_fin


def dsa_attend(q, k, v, qi, ki, wi, q_chunk, k_chunk, topk):
    B, Tq, H, Dh = q.shape
    qb = min(Q_BLOCK, Tq)
    nb = -(-Tq // qb)
    pad = nb * qb - Tq

    def blocks(a):
        a = jnp.pad(a, [(0, 0), (0, pad)] + [(0, 0)] * (a.ndim - 2))
        return a.reshape((B, nb, qb) + a.shape[2:]).swapaxes(0, 1)

    qc_blocks = jnp.pad(q_chunk, (0, pad), mode='edge').reshape(nb, qb)
    kif = ki.astype(jnp.float32)

    def one(args):
        q_b, qi_b, wi_b, qc_b = args
        sc = jnp.einsum('bqhd,bkd->bqhk', qi_b.astype(jnp.float32), kif) * (IDX_DIM ** -0.5)
        score = jnp.einsum('bqhk,bqh->bqk', jax.nn.relu(sc), wi_b.astype(jnp.float32))
        adm = k_chunk[None, :] <= qc_b[:, None]
        score = jnp.where(adm[None], score, NEG_BIG)
        _, top_idx = lax.top_k(score, topk)
        valid = jnp.take_along_axis(jnp.broadcast_to(adm[None], score.shape), top_idx, axis=-1)
        kg = jax.vmap(lambda kb, ib: kb[ib])(k, top_idx)
        vg = jax.vmap(lambda vb, ib: vb[ib])(v, top_idx)
        s = jnp.einsum('bqhd,bqnd->bqhn', q_b.astype(jnp.float32), kg.astype(jnp.float32)) * (Dh ** -0.5)
        s = jnp.where(valid[:, :, None, :], s, NEG_BIG)
        p = jax.nn.softmax(s, axis=-1)
        return jnp.einsum('bqhn,bqnd->bqhd', p, vg.astype(jnp.float32)).astype(q.dtype)

    out = lax.map(one, (blocks(q), blocks(qi), blocks(wi), qc_blocks))
    return out.swapaxes(0, 1).reshape(B, nb * qb, H, Dh)[:, :Tq]


def ab_mixer(u, pos, q_chunk, past_k, past_v, past_ki, past_chunk, s0,
             w_in, w_out, lb, o_gain, q_gain, k_gain, topk):
    B, T, _ = u.shape
    z = u @ w_in
    split_at = [int(c) for c in np.cumsum(IN_SPLITS)[:-1]]
    aq, af, ai, ag, bq, bk, bv, bqi, bki, bw = jnp.split(z, split_at, axis=-1)
    qa = jax.nn.silu(aq.astype(jnp.float32)).reshape(B, T, A_HEADS, A_DK).transpose(0, 2, 1, 3)
    zf = af.astype(jnp.float32).reshape(B, T, A_HEADS, A_DK).transpose(0, 2, 1, 3)
    lb_h = lb.reshape(A_HEADS, 1, A_DK)
    lb_pos = lb_h > 0
    lb_safe = jnp.where(lb_pos, lb_h, 0.5)
    log_f = jnp.where(lb_pos,
                      jnp.logaddexp(jnp.log(lb_safe), jnp.log1p(-lb_safe) + jax.nn.log_sigmoid(zf)),
                      jax.nn.log_sigmoid(zf))
    ka = (1.0 - lb_h) * jax.nn.sigmoid(-zf)
    va = ai.astype(jnp.float32).reshape(B, T, A_HEADS, A_DV).transpose(0, 2, 1, 3)
    oa, s_fin = hgrn2_recurrence(qa, ka, va, log_f, s0.astype(jnp.float32))
    oa = oa.transpose(0, 2, 1, 3)
    oa = oa * lax.rsqrt(jnp.mean(oa * oa, axis=-1, keepdims=True) + RMS_EPS)
    oa = oa.reshape(B, T, A_VW) * o_gain.astype(jnp.float32) * jax.nn.sigmoid(ag.astype(jnp.float32))
    q = rope_partial(rms_norm(bq.reshape(B, T, B_HEADS, B_DH), q_gain), pos)
    k_new = rope_partial(rms_norm(bk.reshape(B, T, 1, B_DH), k_gain), pos)[:, :, 0]
    v_new = bv
    qi = rope_partial(bqi.reshape(B, T, IDX_HEADS, IDX_DIM), pos)
    ki_new = rope_partial(bki.reshape(B, T, 1, IDX_DIM), pos)[:, :, 0]
    wi = bw * (IDX_HEADS ** -0.5)
    k_all = jnp.concatenate([past_k.astype(k_new.dtype), k_new], axis=1)
    v_all = jnp.concatenate([past_v.astype(v_new.dtype), v_new], axis=1)
    ki_all = jnp.concatenate([past_ki.astype(ki_new.dtype), ki_new], axis=1)
    k_chunk = jnp.concatenate([past_chunk, q_chunk])
    ob = dsa_attend(q, k_all, v_all, qi, ki_all, wi, q_chunk, k_chunk, topk)
    merged = jnp.concatenate([oa.astype(u.dtype), ob.reshape(B, T, B_QW)], axis=-1)
    return merged @ w_out, k_new, v_new, ki_new, s_fin.astype(s0.dtype)


def pool_mixer(u, hist, pos, w_grp, scale):
    B, T, D = u.shape
    ue = jnp.concatenate([hist.astype(u.dtype), u], axis=1).astype(jnp.float32)
    cs = jnp.cumsum(ue, axis=1)
    cs = jnp.concatenate([jnp.zeros((B, 1, D), jnp.float32), cs], axis=1)
    end = cs[:, POOL_HIST + 1:]
    uf = ue[:, POOL_HIST:]
    outs = []
    for g, w in enumerate(POOL_WINDOWS):
        sl = slice(g * POOL_GROUP, (g + 1) * POOL_GROUP)
        start = cs[:, POOL_HIST + 1 - w: POOL_HIST + 1 - w + T, sl]
        cnt = jnp.minimum(w, pos + 1).astype(jnp.float32)[None, :, None]
        mean = (end[..., sl] - start) / cnt
        outs.append(jnp.einsum('btc,cd->btd', mean - uf[..., sl], w_grp[g].astype(jnp.float32)))
    y = jnp.concatenate(outs, axis=-1) * scale.astype(jnp.float32)
    return y.astype(u.dtype), ue[:, -POOL_HIST:].astype(u.dtype)


def setup_inputs(seed: int = 0) -> dict:
    key = jax.random.key(seed)
    ks = jax.random.split(key, 25)
    f32 = jnp.float32

    def nrm(k, shape, scale=1.0):
        return jax.random.normal(k, shape, f32) * scale

    def gain(k, shape):
        return 1.0 + 0.05 * jax.random.normal(k, shape, f32)

    return {
        'x_prompt': nrm(ks[0], (BATCH, SEQ, D_MODEL)),
        'x_sample': nrm(ks[1], (DEC_BATCH, DEC_SEQ, D_MODEL)),
        'cache_k': nrm(ks[2], (N_AB, DEC_BATCH, PAST_LEN, B_DH)),
        'cache_v': nrm(ks[3], (N_AB, DEC_BATCH, PAST_LEN, B_DH)),
        'cache_ki': nrm(ks[4], (N_AB, DEC_BATCH, PAST_LEN, IDX_DIM)),
        'state_hgrn': nrm(ks[5], (N_AB, DEC_BATCH, A_HEADS, A_DK, A_DV), 0.5),
        'state_pool': nrm(ks[6], (N_POOL, DEC_BATCH, POOL_HIST, D_MODEL)),
        'meta_tokens': nrm(ks[7], (N_META, D_MODEL)),
        'norm_ffn1': gain(ks[8], (DEPTH, D_MODEL)),
        'ffn1_wg': nrm(ks[9], (DEPTH, D_MODEL, D_FF), D_MODEL ** -0.5),
        'ffn1_wu': nrm(ks[10], (DEPTH, D_MODEL, D_FF), D_MODEL ** -0.5),
        'ffn1_wd': nrm(ks[11], (DEPTH, D_FF, D_MODEL), D_FF ** -0.5),
        'norm_mix': gain(ks[12], (DEPTH, D_MODEL)),
        'norm_ffn2': gain(ks[13], (DEPTH, D_MODEL)),
        'ffn2_wg': nrm(ks[14], (DEPTH, D_MODEL, D_FF), D_MODEL ** -0.5),
        'ffn2_wu': nrm(ks[15], (DEPTH, D_MODEL, D_FF), D_MODEL ** -0.5),
        'ffn2_wd': nrm(ks[16], (DEPTH, D_FF, D_MODEL), D_FF ** -0.5),
        'ab_w_in': nrm(ks[17], (N_AB, D_MODEL, IN_COLS), D_MODEL ** -0.5),
        'ab_w_out': nrm(ks[18], (N_AB, MIX_W, D_MODEL), MIX_W ** -0.5),
        'hgrn_lb_logits': nrm(ks[19], (N_AB, A_KW), 0.5),
        'hgrn_out_norm': gain(ks[20], (N_AB, A_VW)),
        'attn_q_norm': gain(ks[21], (N_AB, B_DH)),
        'attn_k_norm': gain(ks[22], (N_AB, B_DH)),
        'pool_w': nrm(ks[23], (N_POOL, len(POOL_WINDOWS), POOL_GROUP, POOL_GROUP), POOL_GROUP ** -0.5),
        'pool_scale': gain(ks[24], (N_POOL, D_MODEL)),
    }


def reference(x_prompt, x_sample, cache_k, cache_v, cache_ki, state_hgrn, state_pool,
              meta_tokens, norm_ffn1, ffn1_wg, ffn1_wu, ffn1_wd, norm_mix, norm_ffn2,
              ffn2_wg, ffn2_wu, ffn2_wd, ab_w_in, ab_w_out, hgrn_lb_logits, hgrn_out_norm,
              attn_q_norm, attn_k_norm, pool_w, pool_scale):
    Bp, Sp, D = x_prompt.shape
    Bs, Ts, _ = x_sample.shape
    P = cache_k.shape[2]
    hp = jnp.concatenate([jnp.broadcast_to(meta_tokens.astype(x_prompt.dtype)[None], (Bp, N_META, D)), x_prompt], axis=1)
    Tp = hp.shape[1]
    pos_p = jnp.arange(Tp, dtype=jnp.int32)
    chunk_p = jnp.where(pos_p < N_META, -1, (pos_p - N_META) // CHUNK)
    pos_s = P + jnp.arange(Ts, dtype=jnp.int32)
    chunk_s = pos_s // CHUNK
    past_chunk = jnp.arange(P, dtype=jnp.int32) // CHUNK
    topk_p = min(TOPK_MAX, Sp // 4)
    topk_s = min(TOPK_MAX, (P + Ts) // 4)
    empty_kv = jnp.zeros((Bp, 0, B_DH), hp.dtype)
    empty_ki = jnp.zeros((Bp, 0, IDX_DIM), hp.dtype)
    empty_c = jnp.zeros((0,), jnp.int32)
    s0_p = jnp.zeros((Bp, A_HEADS, A_DK, A_DV), jnp.float32)
    hist_p = jnp.zeros((Bp, POOL_HIST, D), hp.dtype)
    lb_soft = jax.nn.softmax(hgrn_lb_logits.astype(jnp.float32), axis=0)
    lb_all = jnp.cumsum(lb_soft, axis=0) - lb_soft[0]

    kp_l, vp_l, kip_l, sp_l, pp_l = [], [], [], [], []
    ks_l, vs_l, kis_l, ss_l, ps_l = [], [], [], [], []
    hs = x_sample
    for l in range(DEPTH):
        hp = hp + 0.5 * swiglu(rms_norm(hp, norm_ffn1[l]), ffn1_wg[l], ffn1_wu[l], ffn1_wd[l])
        hs = hs + 0.5 * swiglu(rms_norm(hs, norm_ffn1[l]), ffn1_wg[l], ffn1_wu[l], ffn1_wd[l])
        up = rms_norm(hp, norm_mix[l])
        us = rms_norm(hs, norm_mix[l])
        j = l // 2
        if l % 2 == 0:
            mp, kp, vp, kip, sp = ab_mixer(up, pos_p, chunk_p, empty_kv, empty_kv, empty_ki, empty_c, s0_p,
                                           ab_w_in[j], ab_w_out[j], lb_all[j], hgrn_out_norm[j],
                                           attn_q_norm[j], attn_k_norm[j], topk_p)
            ms, ks, vs, kis, ss = ab_mixer(us, pos_s, chunk_s, cache_k[j], cache_v[j], cache_ki[j], past_chunk,
                                           state_hgrn[j], ab_w_in[j], ab_w_out[j], lb_all[j], hgrn_out_norm[j],
                                           attn_q_norm[j], attn_k_norm[j], topk_s)
            kp_l.append(kp); vp_l.append(vp); kip_l.append(kip); sp_l.append(sp)
            ks_l.append(ks); vs_l.append(vs); kis_l.append(kis); ss_l.append(ss)
        else:
            mp, pp = pool_mixer(up, hist_p, pos_p, pool_w[j], pool_scale[j])
            ms, ps = pool_mixer(us, state_pool[j], pos_s, pool_w[j], pool_scale[j])
            pp_l.append(pp); ps_l.append(ps)
        hp = hp + mp
        hs = hs + ms
        hp = hp + 0.5 * swiglu(rms_norm(hp, norm_ffn2[l]), ffn2_wg[l], ffn2_wu[l], ffn2_wd[l])
        hs = hs + 0.5 * swiglu(rms_norm(hs, norm_ffn2[l]), ffn2_wg[l], ffn2_wu[l], ffn2_wd[l])

    y_prompt = hp[:, N_META:]
    return (y_prompt, hs,
            jnp.stack(kp_l), jnp.stack(vp_l), jnp.stack(kip_l), jnp.stack(sp_l), jnp.stack(pp_l),
            jnp.stack(ks_l), jnp.stack(vs_l), jnp.stack(kis_l), jnp.stack(ss_l), jnp.stack(ps_l))
```

```python
import functools

import numpy as np
import jax
import jax.numpy as jnp
from jax import lax
from jax.experimental import pallas as pl
from jax.experimental.pallas import tpu as pltpu

F32 = jnp.float32
BF16 = jnp.bfloat16

D_MODEL = 1024
D_FF = 2816
DEPTH = 4
CHUNK = 64
N_META = 16
A_HEADS = 4
A_DK = 128
A_VW = 512
B_HEADS = 4
B_DH = 128
B_QW = 512
IDX_HEADS = 4
IDX_DIM = 64
TOPK = 256
ROPE_THETA = 500000.0
ROT_FRAC = 4
POOL_WINDOWS = (2, 4, 8, 16)
POOL_GROUP = D_MODEL // len(POOL_WINDOWS)
POOL_HIST = max(POOL_WINDOWS) - 1
RMS_EPS = 1e-6
NEG_BIG = -1e30

LANES = 128
HALO = 16
FRONT_PAD = 2 * CHUNK - N_META
IN_COLS_PAD = 3200
A_COLS = 2048
VMEM_LIMIT = 56 * 1024 * 1024
INT_MIN = -2147483648


def _cparams(sem):
    return pltpu.CompilerParams(dimension_semantics=sem, vmem_limit_bytes=VMEM_LIMIT)


def _sigmoid(x):
    return 1.0 / (1.0 + jnp.exp(-x))


def _rms(x, g):
    return x * lax.rsqrt(jnp.mean(x * x, axis=-1, keepdims=True) + RMS_EPS) * g


def _ffn_kernel(*refs, nf, emit_u):
    if emit_u:
        x_ref, g_ref, wg_ref, wu_ref, wd_ref, g2_ref, y_ref, u_ref, xn_ref, acc_ref = refs
    else:
        x_ref, g_ref, wg_ref, wu_ref, wd_ref, y_ref, xn_ref, acc_ref = refs
    f = pl.program_id(2)

    @pl.when(f == 0)
    def _():
        xn_ref[...] = _rms(x_ref[...], g_ref[...]).astype(BF16)

    xn = xn_ref[...]
    gate = jnp.dot(xn, wg_ref[...], preferred_element_type=F32)
    up = jnp.dot(xn, wu_ref[...], preferred_element_type=F32)
    act = (gate * _sigmoid(gate) * up).astype(BF16)
    part = jnp.dot(act, wd_ref[...], preferred_element_type=F32)

    @pl.when(f == 0)
    def _():
        acc_ref[...] = part

    @pl.when(f > 0)
    def _():
        acc_ref[...] += part

    @pl.when(f == nf - 1)
    def _():
        y = x_ref[...] + 0.5 * acc_ref[...]
        y_ref[...] = y
        if emit_u:
            u_ref[...] = _rms(y, g2_ref[...]).astype(u_ref.dtype)


def ffn(x, g, wg, wu, wd, g2=None, u_dtype=BF16, tm=528, tf=1408):
    G, R, D = x.shape
    F = wg.shape[1]
    tm = min(tm, R)
    assert R % tm == 0 and F % tf == 0
    nf = F // tf
    emit_u = g2 is not None
    row = pl.BlockSpec((None, tm, D), lambda b, i, f: (b, i, 0))
    vec = pl.BlockSpec((1, D), lambda b, i, f: (0, 0))
    in_specs = [row, vec,
                pl.BlockSpec((D, tf), lambda b, i, f: (0, f)),
                pl.BlockSpec((D, tf), lambda b, i, f: (0, f)),
                pl.BlockSpec((tf, D), lambda b, i, f: (f, 0))]
    args = [x, g.reshape(1, D), wg, wu, wd]
    out_shape = [jax.ShapeDtypeStruct((G, R, D), F32)]
    out_specs = [row]
    if emit_u:
        in_specs.append(vec)
        args.append(g2.reshape(1, D))
        out_shape.append(jax.ShapeDtypeStruct((G, R, D), u_dtype))
        out_specs.append(row)
    out = pl.pallas_call(
        functools.partial(_ffn_kernel, nf=nf, emit_u=emit_u),
        grid=(G, R // tm, nf),
        in_specs=in_specs,
        out_specs=out_specs,
        out_shape=out_shape,
        scratch_shapes=[pltpu.VMEM((tm, D), BF16), pltpu.VMEM((tm, D), F32)],
        compiler_params=_cparams(("parallel", "parallel", "arbitrary")),
        name="ffn",
    )(*args)
    return tuple(out) if emit_u else out[0]


def _rope(x, c, s_lo, s_hi, half):
    return x * c + pltpu.roll(x, half, 1) * s_lo + pltpu.roll(x, LANES - half, 1) * s_hi


def _inproj_kernel(u_ref, w_ref, qg_ref, kg_ref, cb_ref, sbl_ref, sbh_ref, ci_ref, sil_ref, sih_ref,
                   za_ref, q_ref, kv_ref, qi_ref, slab_ref):
    u = u_ref[...]
    za_ref[...] = jnp.dot(u, w_ref[:, 0:A_COLS], preferred_element_type=F32)
    hb = B_DH // ROT_FRAC // 2
    hi = IDX_DIM // ROT_FRAC // 2
    cb, sbl, sbh = cb_ref[...], sbl_ref[...], sbh_ref[...]
    ci, sil, sih = ci_ref[...], sil_ref[...], sih_ref[...]
    zq = jnp.dot(u, w_ref[:, A_COLS:A_COLS + B_QW], preferred_element_type=F32)
    for h in range(B_HEADS):
        xh = _rms(zq[:, h * B_DH:(h + 1) * B_DH], qg_ref[...])
        q_ref[:, h * B_DH:(h + 1) * B_DH] = _rope(xh, cb, sbl, sbh, hb).astype(q_ref.dtype)
    c0 = A_COLS + B_QW
    zkv = jnp.dot(u, w_ref[:, c0:c0 + 2 * B_DH], preferred_element_type=F32)
    kv_ref[:, 0:B_DH] = _rope(_rms(zkv[:, 0:B_DH], kg_ref[...]), cb, sbl, sbh, hb)
    kv_ref[:, B_DH:2 * B_DH] = zkv[:, B_DH:2 * B_DH]
    c1 = c0 + 2 * B_DH
    zi = jnp.dot(u, w_ref[:, c1:c1 + IDX_HEADS * IDX_DIM + LANES], preferred_element_type=F32)
    for t in range(IDX_HEADS * IDX_DIM // LANES):
        qi_ref[:, t * LANES:(t + 1) * LANES] = _rope(zi[:, t * LANES:(t + 1) * LANES], ci, sil, sih, hi).astype(qi_ref.dtype)
    tail = zi[:, IDX_HEADS * IDX_DIM:]
    lane = lax.broadcasted_iota(jnp.int32, tail.shape, 1)
    slab_ref[...] = jnp.where(lane < IDX_DIM, _rope(tail, ci, sil, sih, hi), tail)


def inproj(u, w_pad, q_gain, k_gain, tabs, tm):
    G, R, D = u.shape
    assert R % tm == 0
    row = lambda c: pl.BlockSpec((None, tm, c), lambda b, i: (b, i, 0))
    tab = pl.BlockSpec((tm, LANES), lambda b, i: (i, 0))
    vec = pl.BlockSpec((1, LANES), lambda b, i: (0, 0))
    return pl.pallas_call(
        _inproj_kernel,
        grid=(G, R // tm),
        in_specs=[row(D), pl.BlockSpec((D, IN_COLS_PAD), lambda b, i: (0, 0)), vec, vec] + [tab] * 6,
        out_specs=[row(A_COLS), row(B_QW), row(2 * B_DH), row(IDX_HEADS * IDX_DIM), row(LANES)],
        out_shape=[jax.ShapeDtypeStruct((G, R, A_COLS), F32),
                   jax.ShapeDtypeStruct((G, R, B_QW), BF16),
                   jax.ShapeDtypeStruct((G, R, 2 * B_DH), F32),
                   jax.ShapeDtypeStruct((G, R, IDX_HEADS * IDX_DIM), BF16),
                   jax.ShapeDtypeStruct((G, R, LANES), F32)],
        compiler_params=_cparams(("parallel", "parallel")),
        name="inproj",
    )(u, w_pad, q_gain.reshape(1, LANES), k_gain.reshape(1, LANES), *tabs)


def rope_tables(pos):
    pos = pos.astype(F32)
    out = []
    for d in (B_DH, IDX_DIM):
        half = d // ROT_FRAC // 2
        inv = ROPE_THETA ** (-jnp.arange(half, dtype=F32) / half)
        ang = pos[:, None] * inv[None, :]
        cos, sin = jnp.cos(ang), jnp.sin(ang)
        n = pos.shape[0]
        z = lambda w: jnp.zeros((n, w), F32)
        c = jnp.concatenate([cos, cos, jnp.ones((n, d - 2 * half), F32)], axis=1)
        s_lo = jnp.concatenate([z(half), sin, z(d - 2 * half)], axis=1)
        s_hi = jnp.concatenate([-sin, z(d - half)], axis=1)
        rep = LANES // d
        out += [jnp.tile(c, (1, rep)), jnp.tile(s_lo, (1, rep)), jnp.tile(s_hi, (1, rep))]
    return out


def _hgrn_consts(C):
    levels = []
    m = C // 2
    while m >= 1:
        levels.append(m)
        m //= 2
    t = np.arange(C)[:, None]
    r = np.arange(C)[None, :]
    mats = [(r <= t), (r > t)]
    lvl = np.full((C, C), -1, np.int32)
    lvl[np.arange(C), np.arange(C)] = len(levels)
    for li, m in enumerate(levels):
        anchor = (t // (2 * m)) * (2 * m) + m - 1
        second = t > anchor
        mats.append(np.where(second, (r > anchor) & (r <= t), (r > t) & (r <= anchor)))
        same = (t // (2 * m)) == (r // (2 * m))
        split = same & ((t % (2 * m)) >= m) & ((r % (2 * m)) < m)
        lvl[split] = li
    dmat = np.concatenate([x.astype(np.float32) for x in mats], axis=0)
    return levels, jnp.asarray(dmat, BF16), jnp.asarray(lvl)


def _nt(a, b):
    return lax.dot_general(a, b, (((1,), (1,)), ((), ())), preferred_element_type=F32)


def _tn(a, b):
    return lax.dot_general(a, b, (((0,), (0,)), ((), ())), preferred_element_type=F32)


def _hgrn_kernel(z_ref, d_ref, lvl_ref, par_ref, s0_ref, oa_ref, sf_ref, st_ref, *, C, levels):
    c = pl.program_id(1)
    dk = A_DK

    @pl.when(c == 0)
    def _():
        st_ref[...] = s0_ref[...]

    W = A_HEADS * dk
    aq = z_ref[:, 0:W]
    af = z_ref[:, W:2 * W]
    ai = z_ref[:, 2 * W:3 * W]
    ag = z_ref[:, 3 * W:4 * W]
    log_lb, log_1m, one_m, lb_pos, o_gain = (par_ref[i:i + 1, :] for i in range(5))
    e = jnp.exp(-jnp.abs(af))
    ls = jnp.minimum(af, 0.0) - jnp.log1p(e)
    b2 = log_1m + ls
    lae = jnp.maximum(log_lb, b2) + jnp.log1p(jnp.exp(-jnp.abs(log_lb - b2)))
    log_f = jnp.where(lb_pos > 0.0, lae, ls)
    ka = one_m * (jnp.where(af >= 0.0, e, 1.0) / (1.0 + e))
    qa = aq * _sigmoid(aq)
    g_hi = log_f.astype(BF16)
    r1 = log_f - g_hi.astype(F32)
    g_mid = r1.astype(BF16)
    g_lo = (r1 - g_mid.astype(F32)).astype(BF16)
    dm = d_ref[...]
    X = (jnp.dot(dm, g_hi, preferred_element_type=F32) + jnp.dot(dm, g_mid, preferred_element_type=F32)
         + jnp.dot(dm, g_lo, preferred_element_type=F32))
    lvl = lvl_ref[...]
    rows = lax.broadcasted_iota(jnp.int32, (C, 1), 0)
    nl = len(levels)
    for h in range(A_HEADS):
        sl = slice(h * dk, (h + 1) * dk)
        q, k, v = qa[:, sl], ka[:, sl], ai[:, sl]
        b = X[0:C, sl]
        st = st_ref[h]
        o = _nt((q * jnp.exp(b)).astype(BF16), st.astype(BF16))
        att = jnp.where(lvl == nl, _nt(q.astype(BF16), k.astype(BF16)), 0.0)
        for li, m in enumerate(levels):
            E = jnp.exp(X[(2 + li) * C:(3 + li) * C, sl])
            second = (rows // m) % 2 == 1
            qs = jnp.where(second, q * E, 0.0).astype(BF16)
            ks = jnp.where(second, 0.0, k * E).astype(BF16)
            att = att + jnp.where(lvl == li, _nt(qs, ks), 0.0)
        vb = v.astype(BF16)
        o = o + jnp.dot(att.astype(BF16), vb, preferred_element_type=F32)
        ke = (k * jnp.exp(X[C:2 * C, sl])).astype(BF16)
        st_ref[h] = st * jnp.exp(b[C - 1:C, :]) + _tn(vb, ke)
        on = o * lax.rsqrt(jnp.mean(o * o, axis=-1, keepdims=True) + RMS_EPS)
        oa_ref[:, sl] = (on * o_gain[:, sl] * _sigmoid(ag[:, sl])).astype(oa_ref.dtype)

    @pl.when(c == pl.num_programs(1) - 1)
    def _():
        sf_ref[...] = st_ref[...]


def hgrn(za, par, s0t, C):
    B, T, _ = za.shape
    assert T % C == 0
    levels, dmat, lvl = _hgrn_consts(C)
    W = A_HEADS * A_DK
    st_spec = pl.BlockSpec((None, A_HEADS, A_DK, A_DK), lambda b, c: (b, 0, 0, 0))
    return pl.pallas_call(
        functools.partial(_hgrn_kernel, C=C, levels=levels),
        grid=(B, T // C),
        in_specs=[pl.BlockSpec((None, C, A_COLS), lambda b, c: (b, c, 0)),
                  pl.BlockSpec(dmat.shape, lambda b, c: (0, 0)),
                  pl.BlockSpec((C, C), lambda b, c: (0, 0)),
                  pl.BlockSpec((8, W), lambda b, c: (0, 0)),
                  st_spec],
        out_specs=[pl.BlockSpec((None, C, W), lambda b, c: (b, c, 0)), st_spec],
        out_shape=[jax.ShapeDtypeStruct((B, T, W), BF16),
                   jax.ShapeDtypeStruct((B, A_HEADS, A_DK, A_DK), F32)],
        scratch_shapes=[pltpu.VMEM((A_HEADS, A_DK, A_DK), F32)],
        compiler_params=_cparams(("parallel", "arbitrary")),
        name="hgrn",
    )(za, dmat, lvl, par, s0t)


def _dsa_kernel(nkb_ref, qt_ref, qit_ref, wit_ref, qc_ref, kc_ref, k_ref, vt_ref, ki_ref, tri_ref,
                ob_ref, key_ref, acc_ref, *, topk):
    i = pl.program_id(1)
    nk = nkb_ref[i]
    qc = qc_ref[...]
    qit = qit_ref[...]
    rhs_i = jnp.concatenate([qit[h * IDX_DIM:(h + 1) * IDX_DIM, :] for h in range(IDX_HEADS)], axis=1)
    qt = qt_ref[...]
    rhs_q = jnp.concatenate([qt[h * B_DH:(h + 1) * B_DH, :] for h in range(B_HEADS)], axis=1)
    wsc = wit_ref[...] * (IDX_HEADS ** -0.5 * IDX_DIM ** -0.5)

    def score_body(j, carry):
        sc = jnp.dot(ki_ref[j], rhs_i, preferred_element_type=F32)
        s = jnp.maximum(sc[:, 0:LANES], 0.0) * wsc[0:1, :]
        for h in range(1, IDX_HEADS):
            s = s + jnp.maximum(sc[:, h * LANES:(h + 1) * LANES], 0.0) * wsc[h:h + 1, :]
        s = jnp.where(kc_ref[j] <= qc, s + 0.0, NEG_BIG)
        bits = lax.bitcast_convert_type(s, jnp.int32)
        key_ref[j] = bits ^ ((bits >> 31) & 0x7FFFFFFF)
        return carry

    lax.fori_loop(0, nk, score_body, 0)

    def count(pred):
        def body(j, acc):
            return acc + jnp.where(pred(key_ref[j]), 1, 0)
        acc = lax.fori_loop(0, nk, body, jnp.zeros((LANES, LANES), jnp.int32))
        return jnp.sum(acc, axis=0, keepdims=True)

    c0 = count(lambda kv: kv >= 0)
    thr0 = jnp.where(c0 >= topk, 0, INT_MIN).astype(jnp.int32)

    def bit_body(it, thr):
        cand = thr | jnp.left_shift(jnp.int32(1), 30 - it)
        cnt = count(lambda kv: kv >= cand)
        return jnp.where(cnt >= topk, cand, thr)

    thr = lax.fori_loop(0, 31, bit_body, thr0)
    need = (topk - count(lambda kv: kv > thr)).astype(F32)

    acc_ref[...] = jnp.zeros_like(acc_ref)
    scale = B_DH ** -0.5
    tri = tri_ref[...]

    def att_body(j, carry):
        ms, ls, tie_seen = carry
        kv = key_ref[j]
        eq = kv == thr
        eqb = jnp.where(eq, 1.0, 0.0).astype(BF16)
        before = jnp.dot(tri, eqb, preferred_element_type=F32) + tie_seen
        take = jnp.where(kv > thr, 1, jnp.where(eq, jnp.where(before < need, 1, 0), 0))
        sel = jnp.where(kc_ref[j] <= qc, take, 0) > 0
        tie_seen = tie_seen + jnp.sum(eqb.astype(F32), axis=0, keepdims=True)
        s_all = jnp.dot(k_ref[j], rhs_q, preferred_element_type=F32)
        new_ms, new_ls, ps, alphas = [], [], [], []
        for h in range(B_HEADS):
            s = jnp.where(sel, s_all[:, h * LANES:(h + 1) * LANES] * scale, NEG_BIG)
            m_new = jnp.maximum(ms[h], jnp.max(s, axis=0, keepdims=True))
            alpha = jnp.exp(ms[h] - m_new)
            p = jnp.where(sel, jnp.exp(s - m_new), 0.0)
            new_ls.append(alpha * ls[h] + jnp.sum(p, axis=0, keepdims=True))
            new_ms.append(m_new)
            alphas.append(alpha)
            ps.append(p.astype(BF16))
        pv = jnp.dot(vt_ref[j], jnp.concatenate(ps, axis=1), preferred_element_type=F32)
        for h in range(B_HEADS):
            acc_ref[h] = alphas[h] * acc_ref[h] + pv[:, h * LANES:(h + 1) * LANES]
        return tuple(new_ms), tuple(new_ls), tie_seen

    init_m = tuple(jnp.full((1, LANES), NEG_BIG, F32) for _ in range(B_HEADS))
    init_l = tuple(jnp.zeros((1, LANES), F32) for _ in range(B_HEADS))
    _, ls, _ = lax.fori_loop(0, nk, att_body, (init_m, init_l, jnp.zeros((1, LANES), F32)))
    for h in range(B_HEADS):
        inv = jnp.where(ls[h] > 0.0, 1.0 / jnp.where(ls[h] > 0.0, ls[h], 1.0), 0.0)
        ob_ref[:, h * B_DH:(h + 1) * B_DH] = (acc_ref[h] * inv).T.astype(ob_ref.dtype)


def dsa(q, qi, wi, k_all, v_all, ki_all, q_chunk, k_chunk, nkb, topk=TOPK):
    B, Tq, _ = q.shape
    Tk = k_all.shape[1]
    assert Tq % LANES == 0 and Tk % LANES == 0
    nq, nkt = Tq // LANES, Tk // LANES
    qt = jnp.swapaxes(q, 1, 2)
    qit = jnp.swapaxes(qi, 1, 2)
    wit = jnp.swapaxes(wi, 1, 2)
    k4 = k_all.astype(BF16).reshape(B, nkt, LANES, B_DH)
    vt4 = jnp.swapaxes(v_all.astype(BF16).reshape(B, nkt, LANES, B_DH), 2, 3)
    ki4 = ki_all.astype(BF16).reshape(B, nkt, LANES, IDX_DIM)
    kc = jnp.broadcast_to(k_chunk.astype(jnp.int32).reshape(nkt, LANES, 1), (nkt, LANES, LANES))
    qc = q_chunk.astype(jnp.int32).reshape(nq, 1, LANES)
    tri = jnp.asarray(np.tril(np.ones((LANES, LANES), np.float32), -1), BF16)
    grid_spec = pltpu.PrefetchScalarGridSpec(
        num_scalar_prefetch=1,
        grid=(B, nq),
        in_specs=[pl.BlockSpec((None, B_QW, LANES), lambda b, i, n: (b, 0, i)),
                  pl.BlockSpec((None, IDX_HEADS * IDX_DIM, LANES), lambda b, i, n: (b, 0, i)),
                  pl.BlockSpec((None, IDX_HEADS, LANES), lambda b, i, n: (b, 0, i)),
                  pl.BlockSpec((None, 1, LANES), lambda b, i, n: (i, 0, 0)),
                  pl.BlockSpec((nkt, LANES, LANES), lambda b, i, n: (0, 0, 0)),
                  pl.BlockSpec((None, nkt, LANES, B_DH), lambda b, i, n: (b, 0, 0, 0)),
                  pl.BlockSpec((None, nkt, B_DH, LANES), lambda b, i, n: (b, 0, 0, 0)),
                  pl.BlockSpec((None, nkt, LANES, IDX_DIM), lambda b, i, n: (b, 0, 0, 0)),
                  pl.BlockSpec((LANES, LANES), lambda b, i, n: (0, 0))],
        out_specs=pl.BlockSpec((None, LANES, B_QW), lambda b, i, n: (b, i, 0)),
        scratch_shapes=[pltpu.VMEM((nkt, LANES, LANES), jnp.int32),
                        pltpu.VMEM((B_HEADS, B_DH, LANES), F32)],
    )
    return pl.pallas_call(
        functools.partial(_dsa_kernel, topk=topk),
        grid_spec=grid_spec,
        out_shape=jax.ShapeDtypeStruct((B, Tq, B_QW), BF16),
        compiler_params=_cparams(("parallel", "arbitrary")),
        name="dsa",
    )(nkb, qt, qit, wit, qc, kc, k4, vt4, ki4, tri)


def _outproj_kernel(h_ref, oa_ref, ob_ref, w_ref, y_ref):
    y = jnp.dot(oa_ref[...], w_ref[0:A_VW, :], preferred_element_type=F32)
    y = y + jnp.dot(ob_ref[...], w_ref[A_VW:A_VW + B_QW, :], preferred_element_type=F32)
    y_ref[...] = h_ref[...] + y


def outproj(h, oa, ob, w, tm):
    G, R, D = h.shape
    assert R % tm == 0
    row = lambda c: pl.BlockSpec((None, tm, c), lambda b, i: (b, i, 0))
    return pl.pallas_call(
        _outproj_kernel,
        grid=(G, R // tm),
        in_specs=[row(D), row(A_VW), row(B_QW), pl.BlockSpec((A_VW + B_QW, D), lambda b, i: (0, 0))],
        out_specs=row(D),
        out_shape=jax.ShapeDtypeStruct((G, R, D), F32),
        compiler_params=_cparams(("parallel", "parallel")),
        name="outproj",
    )(h, oa, ob, w)


def _pool_kernel(h_ref, u_ref, halo_ref, w_ref, sc_ref, y_ref, *, tm, pos0, zero_first_halo):
    i = pl.program_id(1)
    halo = halo_ref[...]
    if zero_first_halo:
        halo = jnp.where(i > 0, halo, 0.0)
    u = u_ref[...]
    ue = jnp.concatenate([halo, u], axis=0)
    pos = pos0 + i * tm + lax.broadcasted_iota(jnp.int32, (tm, 1), 0)
    run = ue
    outs = []
    for g, w in enumerate(POOL_WINDOWS):
        lo = g * POOL_GROUP
        if g:
            run = run[:, POOL_GROUP:]
        run = run + pltpu.roll(run, w // 2, 0)
        cnt = jnp.clip(pos + 1, 1, w).astype(F32)
        mean = run[HALO:, 0:POOL_GROUP] / cnt
        d = (mean - u[:, lo:lo + POOL_GROUP]).astype(BF16)
        outs.append(jnp.dot(d, w_ref[g], preferred_element_type=F32))
    y = jnp.concatenate(outs, axis=1) * sc_ref[...]
    y_ref[...] = h_ref[...] + y


def pool(h, u, halo_src, w_grp, scale, tm, pos0, zero_first_halo):
    G, R, D = h.shape
    assert R % tm == 0 and tm % HALO == 0
    row = pl.BlockSpec((None, tm, D), lambda b, i: (b, i, 0))
    step = tm // HALO
    halo = pl.BlockSpec((None, HALO, D), lambda b, i: (b, jnp.maximum(i * step - 1, 0), 0))
    return pl.pallas_call(
        functools.partial(_pool_kernel, tm=tm, pos0=pos0, zero_first_halo=zero_first_halo),
        grid=(G, R // tm),
        in_specs=[row, row, halo,
                  pl.BlockSpec((len(POOL_WINDOWS), POOL_GROUP, POOL_GROUP), lambda b, i: (0, 0, 0)),
                  pl.BlockSpec((1, D), lambda b, i: (0, 0))],
        out_specs=row,
        out_shape=jax.ShapeDtypeStruct((G, R, D), F32),
        compiler_params=_cparams(("parallel", "parallel")),
        name="pool",
    )(h, u, halo_src, w_grp, scale.reshape(1, D))


def _hgrn_params(lb, o_gain):
    lb_pos = lb > 0
    lb_safe = jnp.where(lb_pos, lb, 0.5)
    rows = [jnp.log(lb_safe), jnp.log1p(-lb_safe), 1.0 - lb, lb_pos.astype(F32), o_gain.astype(F32)]
    rows += [jnp.zeros_like(lb)] * 3
    return jnp.stack(rows).astype(F32)


def kernel(x_prompt, x_sample, cache_k, cache_v, cache_ki, state_hgrn, state_pool, meta_tokens, norm_ffn1, ffn1_wg, ffn1_wu, ffn1_wd, norm_mix, norm_ffn2, ffn2_wg, ffn2_wu, ffn2_wd, ab_w_in, ab_w_out, hgrn_lb_logits, hgrn_out_norm, attn_q_norm, attn_k_norm, pool_w, pool_scale):
    Bp, Sp, D = x_prompt.shape
    Bs, Ts, _ = x_sample.shape
    P = cache_k.shape[2]
    Tp = FRONT_PAD + N_META + Sp
    assert Tp % LANES == 0 and Ts == CHUNK and P % LANES == 0
    tm_p = Tp // 8
    Rs = Bs * Ts

    hp = jnp.concatenate([jnp.zeros((Bp, FRONT_PAD, D), F32),
                          jnp.broadcast_to(meta_tokens.astype(F32)[None], (Bp, N_META, D)), x_prompt], axis=1)
    hs = x_sample.reshape(1, Rs, D)

    row_p = jnp.arange(Tp, dtype=jnp.int32)
    pos_p = row_p - FRONT_PAD
    chunk_p = jnp.where(pos_p < 0, -2, jnp.where(pos_p < N_META, -1, (pos_p - N_META) // CHUNK))
    kchunk_p = jnp.where(pos_p < 0, jnp.int32(2 ** 30), chunk_p)
    nkb_p = jnp.arange(Tp // LANES, dtype=jnp.int32) + 1
    tabs_p = rope_tables(pos_p)

    pos_s = P + jnp.arange(Ts, dtype=jnp.int32)
    Tks = P + 2 * CHUNK
    qchunk_s = jnp.concatenate([pos_s // CHUNK, jnp.full((LANES - Ts,), -2, jnp.int32)])
    kchunk_s = jnp.concatenate([jnp.arange(P + Ts, dtype=jnp.int32) // CHUNK,
                                jnp.full((Tks - P - Ts,), 2 ** 30, jnp.int32)])
    nkb_s = jnp.full((1,), Tks // LANES, jnp.int32)
    tabs_s = [jnp.tile(t, (Bs, 1)) for t in rope_tables(pos_s)]

    lb_soft = jax.nn.softmax(hgrn_lb_logits.astype(F32), axis=0)
    lb_all = jnp.cumsum(lb_soft, axis=0) - lb_soft[0]

    cast = lambda w: w.astype(BF16)
    kp_l, vp_l, kip_l, sp_l, pp_l = [], [], [], [], []
    ks_l, vs_l, kis_l, ss_l, ps_l = [], [], [], [], []
    for l in range(DEPTH):
        j = l // 2
        even = l % 2 == 0
        w1 = (cast(ffn1_wg[l]), cast(ffn1_wu[l]), cast(ffn1_wd[l]))
        w2 = (cast(ffn2_wg[l]), cast(ffn2_wu[l]), cast(ffn2_wd[l]))
        udt = BF16 if even else F32
        hp, up = ffn(hp, norm_ffn1[l], *w1, g2=norm_mix[l], u_dtype=udt, tm=tm_p)
        hs, us = ffn(hs, norm_ffn1[l], *w1, g2=norm_mix[l], u_dtype=udt, tm=Rs)
        if even:
            w_in = jnp.pad(cast(ab_w_in[j]), ((0, 0), (0, IN_COLS_PAD - ab_w_in.shape[2])))
            w_out = cast(ab_w_out[j])
            par = _hgrn_params(lb_all[j], hgrn_out_norm[j])
            za, q, kv, qi, slab = inproj(up, w_in, attn_q_norm[j], attn_k_norm[j], tabs_p, tm_p)
            s0 = jnp.zeros((Bp, A_HEADS, A_DK, A_DK), F32)
            oa, sft = hgrn(za, par, s0, LANES)
            k_new, v_new, ki_new = kv[..., :B_DH], kv[..., B_DH:], slab[..., :IDX_DIM]
            wi = slab[..., IDX_DIM:IDX_DIM + IDX_HEADS]
            ob = dsa(q, qi, wi, k_new, v_new, ki_new, chunk_p, kchunk_p, nkb_p)
            hp = outproj(hp, oa, ob, w_out, tm_p)
            kp_l.append(k_new[:, FRONT_PAD:]); vp_l.append(v_new[:, FRONT_PAD:]); kip_l.append(ki_new[:, FRONT_PAD:])
            sp_l.append(jnp.swapaxes(sft, 2, 3))
            za, q, kv, qi, slab = inproj(us, w_in, attn_q_norm[j], attn_k_norm[j], tabs_s, Rs)
            s0 = jnp.swapaxes(state_hgrn[j].astype(F32), 2, 3)
            oa, sft = hgrn(za.reshape(Bs, Ts, A_COLS), par, s0, Ts)
            kv, slab = kv.reshape(Bs, Ts, -1), slab.reshape(Bs, Ts, -1)
            k_new, v_new, ki_new = kv[..., :B_DH], kv[..., B_DH:], slab[..., :IDX_DIM]
            wi = slab[..., IDX_DIM:IDX_DIM + IDX_HEADS]
            padq = lambda a: jnp.pad(a.reshape(Bs, Ts, -1), ((0, 0), (0, LANES - Ts), (0, 0)))
            padk = lambda past, new: jnp.pad(jnp.concatenate([past.astype(F32), new], axis=1),
                                             ((0, 0), (0, Tks - P - Ts), (0, 0)))
            ob = dsa(padq(q), padq(qi), padq(wi), padk(cache_k[j], k_new), padk(cache_v[j], v_new),
                     padk(cache_ki[j], ki_new), qchunk_s, kchunk_s, nkb_s)
            hs = outproj(hs, oa.reshape(1, Rs, A_VW), ob[:, :Ts].reshape(1, Rs, B_QW), w_out, Rs)
            ks_l.append(k_new); vs_l.append(v_new); kis_l.append(ki_new)
            ss_l.append(jnp.swapaxes(sft, 2, 3).astype(state_hgrn.dtype))
        else:
            pw = cast(pool_w[j])
            hp = pool(hp, up, up, pw, pool_scale[j], tm_p, -FRONT_PAD, True)
            pp_l.append(up[:, Tp - POOL_HIST:])
            us3 = us.reshape(Bs, Ts, D)
            hist = state_pool[j].astype(F32)
            halo = jnp.pad(hist, ((0, 0), (HALO - POOL_HIST, 0), (0, 0)))
            hs = pool(hs.reshape(Bs, Ts, D), us3, halo, pw, pool_scale[j], Ts, P, False).reshape(1, Rs, D)
            ps_l.append(jnp.concatenate([hist, us3], axis=1)[:, -POOL_HIST:])
        hp = ffn(hp, norm_ffn2[l], *w2, tm=tm_p)
        hs = ffn(hs, norm_ffn2[l], *w2, tm=Rs)

    y_prompt = hp[:, FRONT_PAD + N_META:]
    return (y_prompt, hs.reshape(Bs, Ts, D),
            jnp.stack(kp_l), jnp.stack(vp_l), jnp.stack(kip_l), jnp.stack(sp_l), jnp.stack(pp_l),
            jnp.stack(ks_l), jnp.stack(vs_l), jnp.stack(kis_l), jnp.stack(ss_l), jnp.stack(ps_l))
```

```python
import functools

import numpy as np
import jax
import jax.numpy as jnp
from jax import lax
from jax.experimental import pallas as pl
from jax.experimental.pallas import tpu as pltpu

F32 = jnp.float32
BF16 = jnp.bfloat16

D_MODEL = 1024
D_FF = 2816
DEPTH = 4
CHUNK = 64
N_META = 16
A_HEADS = 4
A_DK = 128
A_VW = 512
B_HEADS = 4
B_DH = 128
B_QW = 512
IDX_HEADS = 4
IDX_DIM = 64
TOPK = 256
ROPE_THETA = 500000.0
ROT_FRAC = 4
POOL_WINDOWS = (2, 4, 8, 16)
POOL_GROUP = D_MODEL // len(POOL_WINDOWS)
POOL_HIST = max(POOL_WINDOWS) - 1
RMS_EPS = 1e-6
NEG_BIG = -1e30

LANES = 128
SUBLANES = 8
BF16_ROWS = 16
MXU_N = 256
FFN_CHUNKS = 2
KT = 256
QB = 256
NO_CHUNK = 2 ** 30
NO_QUERY = -2
LOG2E = 1.4426950408889634
HALO = 16
FRONT_PAD = 2 * CHUNK - N_META
IN_COLS_PAD = 3200
A_COLS = 2048
VMEM_LIMIT = 56 * 1024 * 1024
INT_MIN = -2147483648


def _cparams(sem):
    return pltpu.CompilerParams(dimension_semantics=sem, vmem_limit_bytes=VMEM_LIMIT)


def _sigmoid(x):
    return 1.0 / (1.0 + jnp.exp(-x))


def _rms(x, g):
    return x * lax.rsqrt(jnp.mean(x * x, axis=-1, keepdims=True) + RMS_EPS) * g


def _ffn_kernel(*refs, row_tiles, ff_chunks, emit_u):
    if emit_u:
        x_ref, g_ref, wg_ref, wu_ref, wd_ref, g2_ref, y_ref, u_ref = refs
    else:
        x_ref, g_ref, wg_ref, wu_ref, wd_ref, y_ref = refs
    for r0, r1 in row_tiles:
        x = x_ref[r0:r1, :]
        xn = _rms(x, g_ref[...]).astype(BF16)
        acc = None
        for c0, c1 in ff_chunks:
            gate = jnp.dot(xn, wg_ref[:, c0:c1], preferred_element_type=F32)
            up = jnp.dot(xn, wu_ref[:, c0:c1], preferred_element_type=F32)
            act = (gate * _sigmoid(gate) * up).astype(BF16)
            part = jnp.dot(act, wd_ref[c0:c1, :], preferred_element_type=F32)
            acc = part if acc is None else acc + part
        y = x + 0.5 * acc
        y_ref[r0:r1, :] = y
        if emit_u:
            u_ref[r0:r1, :] = _rms(y, g2_ref[...]).astype(u_ref.dtype)


def _ffn_tiling(tm, F):
    half = (tm // (2 * BF16_ROWS)) * BF16_ROWS
    row_tiles = [(0, half), (half, tm)] if half else [(0, tm)]
    n_col = -(-F // MXU_N)
    per = -(-n_col // FFN_CHUNKS)
    edges = [min(c * per * MXU_N, F) for c in range(FFN_CHUNKS + 1)]
    return row_tiles, [(a, b) for a, b in zip(edges[:-1], edges[1:]) if b > a]


def ffn(x, g, wg, wu, wd, g2=None, u_dtype=BF16, tm=528):
    G, R, D = x.shape
    F = wg.shape[1]
    tm = min(tm, R)
    assert R % tm == 0
    row_tiles, ff_chunks = _ffn_tiling(tm, F)
    emit_u = g2 is not None
    row = pl.BlockSpec((None, tm, D), lambda b, i: (b, i, 0))
    vec = pl.BlockSpec((1, D), lambda b, i: (0, 0))
    resident = lambda shape: pl.BlockSpec(shape, lambda b, i: (0, 0), pipeline_mode=pl.Buffered(1))
    in_specs = [row, vec, resident((D, F)), resident((D, F)), resident((F, D))]
    args = [x, g.reshape(1, D), wg, wu, wd]
    out_shape = [jax.ShapeDtypeStruct((G, R, D), F32)]
    out_specs = [row]
    if emit_u:
        in_specs.append(vec)
        args.append(g2.reshape(1, D))
        out_shape.append(jax.ShapeDtypeStruct((G, R, D), u_dtype))
        out_specs.append(row)
    out = pl.pallas_call(
        functools.partial(_ffn_kernel, row_tiles=row_tiles, ff_chunks=ff_chunks, emit_u=emit_u),
        grid=(G, R // tm),
        in_specs=in_specs,
        out_specs=out_specs,
        out_shape=out_shape,
        compiler_params=_cparams(("parallel", "parallel")),
        name="ffn",
    )(*args)
    return tuple(out) if emit_u else out[0]


def _rope(x, c, s_lo, s_hi, half):
    return x * c + pltpu.roll(x, half, 1) * s_lo + pltpu.roll(x, LANES - half, 1) * s_hi


def _inproj_kernel(u_ref, w_ref, qg_ref, kg_ref, cb_ref, sbl_ref, sbh_ref, ci_ref, sil_ref, sih_ref,
                   za_ref, q_ref, kv_ref, qi_ref, slab_ref):
    u = u_ref[...]
    za_ref[...] = jnp.dot(u, w_ref[:, 0:A_COLS], preferred_element_type=F32)
    hb = B_DH // ROT_FRAC // 2
    hi = IDX_DIM // ROT_FRAC // 2
    cb, sbl, sbh = cb_ref[...], sbl_ref[...], sbh_ref[...]
    ci, sil, sih = ci_ref[...], sil_ref[...], sih_ref[...]
    zq = jnp.dot(u, w_ref[:, A_COLS:A_COLS + B_QW], preferred_element_type=F32)
    for h in range(B_HEADS):
        xh = _rms(zq[:, h * B_DH:(h + 1) * B_DH], qg_ref[...])
        q_ref[:, h * B_DH:(h + 1) * B_DH] = _rope(xh, cb, sbl, sbh, hb).astype(q_ref.dtype)
    c0 = A_COLS + B_QW
    zkv = jnp.dot(u, w_ref[:, c0:c0 + 2 * B_DH], preferred_element_type=F32)
    kv_ref[:, 0:B_DH] = _rope(_rms(zkv[:, 0:B_DH], kg_ref[...]), cb, sbl, sbh, hb)
    kv_ref[:, B_DH:2 * B_DH] = zkv[:, B_DH:2 * B_DH]
    c1 = c0 + 2 * B_DH
    zi = jnp.dot(u, w_ref[:, c1:c1 + IDX_HEADS * IDX_DIM + LANES], preferred_element_type=F32)
    for t in range(IDX_HEADS * IDX_DIM // LANES):
        qi_ref[:, t * LANES:(t + 1) * LANES] = _rope(zi[:, t * LANES:(t + 1) * LANES], ci, sil, sih, hi).astype(qi_ref.dtype)
    tail = zi[:, IDX_HEADS * IDX_DIM:]
    lane = lax.broadcasted_iota(jnp.int32, tail.shape, 1)
    slab_ref[...] = jnp.where(lane < IDX_DIM, _rope(tail, ci, sil, sih, hi), tail)


def inproj(u, w_pad, q_gain, k_gain, tabs, tm):
    G, R, D = u.shape
    assert R % tm == 0
    row = lambda c: pl.BlockSpec((None, tm, c), lambda b, i: (b, i, 0))
    tab = pl.BlockSpec((tm, LANES), lambda b, i: (i, 0))
    vec = pl.BlockSpec((1, LANES), lambda b, i: (0, 0))
    return pl.pallas_call(
        _inproj_kernel,
        grid=(G, R // tm),
        in_specs=[row(D), pl.BlockSpec((D, IN_COLS_PAD), lambda b, i: (0, 0)), vec, vec] + [tab] * 6,
        out_specs=[row(A_COLS), row(B_QW), row(2 * B_DH), row(IDX_HEADS * IDX_DIM), row(LANES)],
        out_shape=[jax.ShapeDtypeStruct((G, R, A_COLS), F32),
                   jax.ShapeDtypeStruct((G, R, B_QW), BF16),
                   jax.ShapeDtypeStruct((G, R, 2 * B_DH), F32),
                   jax.ShapeDtypeStruct((G, R, IDX_HEADS * IDX_DIM), BF16),
                   jax.ShapeDtypeStruct((G, R, LANES), F32)],
        compiler_params=_cparams(("parallel", "parallel")),
        name="inproj",
    )(u, w_pad, q_gain.reshape(1, LANES), k_gain.reshape(1, LANES), *tabs)


def rope_tables(pos):
    pos = pos.astype(F32)
    out = []
    for d in (B_DH, IDX_DIM):
        half = d // ROT_FRAC // 2
        inv = ROPE_THETA ** (-jnp.arange(half, dtype=F32) / half)
        ang = pos[:, None] * inv[None, :]
        cos, sin = jnp.cos(ang), jnp.sin(ang)
        n = pos.shape[0]
        z = lambda w: jnp.zeros((n, w), F32)
        c = jnp.concatenate([cos, cos, jnp.ones((n, d - 2 * half), F32)], axis=1)
        s_lo = jnp.concatenate([z(half), sin, z(d - 2 * half)], axis=1)
        s_hi = jnp.concatenate([-sin, z(d - half)], axis=1)
        rep = LANES // d
        out += [jnp.tile(c, (1, rep)), jnp.tile(s_lo, (1, rep)), jnp.tile(s_hi, (1, rep))]
    return out


def _hgrn_consts(C):
    levels = []
    m = C // 2
    while m >= 1:
        levels.append(m)
        m //= 2
    t = np.arange(C)[:, None]
    r = np.arange(C)[None, :]
    mats = [(r <= t), (r > t)]
    lvl = np.full((C, C), -1, np.int32)
    lvl[np.arange(C), np.arange(C)] = len(levels)
    for li, m in enumerate(levels):
        anchor = (t // (2 * m)) * (2 * m) + m - 1
        second = t > anchor
        mats.append(np.where(second, (r > anchor) & (r <= t), (r > t) & (r <= anchor)))
        same = (t // (2 * m)) == (r // (2 * m))
        split = same & ((t % (2 * m)) >= m) & ((r % (2 * m)) < m)
        lvl[split] = li
    dmat = np.concatenate([x.astype(np.float32) for x in mats], axis=0)
    return levels, jnp.asarray(dmat, BF16), jnp.asarray(lvl)


def _nt(a, b):
    return lax.dot_general(a, b, (((1,), (1,)), ((), ())), preferred_element_type=F32)


def _tn(a, b):
    return lax.dot_general(a, b, (((0,), (0,)), ((), ())), preferred_element_type=F32)


def _hgrn_kernel(z_ref, d_ref, lvl_ref, par_ref, s0_ref, oa_ref, sf_ref, st_ref, *, C, levels):
    c = pl.program_id(1)
    dk = A_DK

    @pl.when(c == 0)
    def _():
        st_ref[...] = s0_ref[...]

    W = A_HEADS * dk
    aq = z_ref[:, 0:W]
    af = z_ref[:, W:2 * W]
    ai = z_ref[:, 2 * W:3 * W]
    ag = z_ref[:, 3 * W:4 * W]
    log_lb, log_1m, one_m, lb_pos, o_gain = (par_ref[i:i + 1, :] for i in range(5))
    e = jnp.exp(-jnp.abs(af))
    ls = jnp.minimum(af, 0.0) - jnp.log1p(e)
    b2 = log_1m + ls
    lae = jnp.maximum(log_lb, b2) + jnp.log1p(jnp.exp(-jnp.abs(log_lb - b2)))
    log_f = jnp.where(lb_pos > 0.0, lae, ls)
    ka = one_m * (jnp.where(af >= 0.0, e, 1.0) / (1.0 + e))
    qa = aq * _sigmoid(aq)
    g_hi = log_f.astype(BF16)
    r1 = log_f - g_hi.astype(F32)
    g_mid = r1.astype(BF16)
    g_lo = (r1 - g_mid.astype(F32)).astype(BF16)
    dm = d_ref[...]
    X = (jnp.dot(dm, g_hi, preferred_element_type=F32) + jnp.dot(dm, g_mid, preferred_element_type=F32)
         + jnp.dot(dm, g_lo, preferred_element_type=F32))
    lvl = lvl_ref[...]
    rows = lax.broadcasted_iota(jnp.int32, (C, 1), 0)
    nl = len(levels)
    for h in range(A_HEADS):
        sl = slice(h * dk, (h + 1) * dk)
        q, k, v = qa[:, sl], ka[:, sl], ai[:, sl]
        b = X[0:C, sl]
        st = st_ref[h]
        o = _nt((q * jnp.exp(b)).astype(BF16), st.astype(BF16))
        att = jnp.where(lvl == nl, _nt(q.astype(BF16), k.astype(BF16)), 0.0)
        for li, m in enumerate(levels):
            E = jnp.exp(X[(2 + li) * C:(3 + li) * C, sl])
            second = (rows // m) % 2 == 1
            qs = jnp.where(second, q * E, 0.0).astype(BF16)
            ks = jnp.where(second, 0.0, k * E).astype(BF16)
            att = att + jnp.where(lvl == li, _nt(qs, ks), 0.0)
        vb = v.astype(BF16)
        o = o + jnp.dot(att.astype(BF16), vb, preferred_element_type=F32)
        ke = (k * jnp.exp(X[C:2 * C, sl])).astype(BF16)
        st_ref[h] = st * jnp.exp(b[C - 1:C, :]) + _tn(vb, ke)
        on = o * lax.rsqrt(jnp.mean(o * o, axis=-1, keepdims=True) + RMS_EPS)
        oa_ref[:, sl] = (on * o_gain[:, sl] * _sigmoid(ag[:, sl])).astype(oa_ref.dtype)

    @pl.when(c == pl.num_programs(1) - 1)
    def _():
        sf_ref[...] = st_ref[...]


def hgrn(za, par, s0t, C):
    B, T, _ = za.shape
    assert T % C == 0
    levels, dmat, lvl = _hgrn_consts(C)
    W = A_HEADS * A_DK
    st_spec = pl.BlockSpec((None, A_HEADS, A_DK, A_DK), lambda b, c: (b, 0, 0, 0))
    return pl.pallas_call(
        functools.partial(_hgrn_kernel, C=C, levels=levels),
        grid=(B, T // C),
        in_specs=[pl.BlockSpec((None, C, A_COLS), lambda b, c: (b, c, 0)),
                  pl.BlockSpec(dmat.shape, lambda b, c: (0, 0)),
                  pl.BlockSpec((C, C), lambda b, c: (0, 0)),
                  pl.BlockSpec((8, W), lambda b, c: (0, 0)),
                  st_spec],
        out_specs=[pl.BlockSpec((None, C, W), lambda b, c: (b, c, 0)), st_spec],
        out_shape=[jax.ShapeDtypeStruct((B, T, W), BF16),
                   jax.ShapeDtypeStruct((B, A_HEADS, A_DK, A_DK), F32)],
        scratch_shapes=[pltpu.VMEM((A_HEADS, A_DK, A_DK), F32)],
        compiler_params=_cparams(("parallel", "arbitrary")),
        name="hgrn",
    )(za, dmat, lvl, par, s0t)


def _dsa_kernel(nkb_ref, qt_ref, qit_ref, wit_ref, qc_ref, kc_ref, k_ref, vt_ref, ki_ref, tri_ref,
                ob_ref, key_ref, acc_ref, *, topk):
    i = pl.program_id(1)
    nk = nkb_ref[i]
    qc = qc_ref[...]
    qit = qit_ref[...]
    rhs_i = jnp.concatenate([qit[h * IDX_DIM:(h + 1) * IDX_DIM, :] for h in range(IDX_HEADS)], axis=1)
    qt = qt_ref[...]
    rhs_q = jnp.concatenate([qt[h * B_DH:(h + 1) * B_DH, :] for h in range(B_HEADS)], axis=1)
    wsc = wit_ref[...] * (IDX_HEADS ** -0.5 * IDX_DIM ** -0.5)

    def score_body(j, carry):
        sc = jnp.dot(ki_ref[j], rhs_i, preferred_element_type=F32)
        s = jnp.maximum(sc[:, 0:QB], 0.0) * wsc[0:1, :]
        for h in range(1, IDX_HEADS):
            s = s + jnp.maximum(sc[:, h * QB:(h + 1) * QB], 0.0) * wsc[h:h + 1, :]
        s = jnp.where(kc_ref[j] <= qc, s + 0.0, NEG_BIG)
        bits = lax.bitcast_convert_type(s, jnp.int32)
        key_ref[j] = bits ^ ((bits >> 31) & 0x7FFFFFFF)
        return carry

    lax.fori_loop(0, nk, score_body, 0)

    def count(pred):
        def body(j, acc):
            hit = jnp.where(pred(key_ref[j]), 1, 0)
            parts = [hit[r * SUBLANES:(r + 1) * SUBLANES, :] for r in range(KT // SUBLANES)]
            while len(parts) > 1:
                parts = [parts[r] + parts[r + 1] for r in range(0, len(parts), 2)]
            return acc + parts[0]
        acc = lax.fori_loop(0, nk, body, jnp.zeros((SUBLANES, QB), jnp.int32))
        return jnp.sum(acc, axis=0, keepdims=True)

    c0 = count(lambda kv: kv >= 0)
    thr0 = jnp.where(c0 >= topk, 0, INT_MIN).astype(jnp.int32)
    above0 = jnp.where(c0 >= topk, 0, c0)

    def bit_body(it, carry):
        thr, above = carry
        cand = thr | jnp.left_shift(jnp.int32(1), 30 - it)
        cnt = count(lambda kv: kv >= cand)
        ok = cnt >= topk
        return jnp.where(ok, cand, thr), jnp.where(ok, above, cnt)

    thr, above = lax.fori_loop(0, 31, bit_body, (thr0, above0))
    need = (topk - above).astype(F32)

    acc_ref[...] = jnp.zeros_like(acc_ref)
    c2 = B_DH ** -0.5 * LOG2E
    tri = tri_ref[...]

    def att_body(j, carry):
        ms, ls, tie_seen = carry
        kv = key_ref[j]
        eq = kv == thr
        eqf = jnp.where(eq, 1.0, 0.0)
        before = jnp.dot(tri, eqf.astype(BF16), preferred_element_type=F32) + tie_seen
        take = jnp.where(kv > thr, 1, jnp.where(eq, jnp.where(before < need, 1, 0), 0))
        sel = jnp.where(kc_ref[j] <= qc, take, 0) > 0
        tie_seen = tie_seen + jnp.sum(eqf, axis=0, keepdims=True)
        s_all = jnp.dot(k_ref[j], rhs_q, preferred_element_type=F32)
        new_ms, new_ls, ps, alphas = [], [], [], []
        for h in range(B_HEADS):
            s = jnp.where(sel, s_all[:, h * QB:(h + 1) * QB], NEG_BIG)
            m_new = jnp.maximum(ms[h], jnp.max(s, axis=0, keepdims=True))
            alpha = jnp.exp2((ms[h] - m_new) * c2)
            p = jnp.where(sel, jnp.exp2((s - m_new) * c2), 0.0)
            new_ls.append(alpha * ls[h] + jnp.sum(p, axis=0, keepdims=True))
            new_ms.append(m_new)
            alphas.append(alpha)
            ps.append(p.astype(BF16))
        pv = jnp.dot(vt_ref[j], jnp.concatenate(ps, axis=1), preferred_element_type=F32)
        for h in range(B_HEADS):
            acc_ref[h] = alphas[h] * acc_ref[h] + pv[:, h * QB:(h + 1) * QB]
        return tuple(new_ms), tuple(new_ls), tie_seen

    init_m = tuple(jnp.full((1, QB), NEG_BIG, F32) for _ in range(B_HEADS))
    init_l = tuple(jnp.zeros((1, QB), F32) for _ in range(B_HEADS))
    _, ls, _ = lax.fori_loop(0, nk, att_body, (init_m, init_l, jnp.zeros((1, QB), F32)))
    for h in range(B_HEADS):
        inv = jnp.where(ls[h] > 0.0, 1.0 / jnp.where(ls[h] > 0.0, ls[h], 1.0), 0.0)
        ob_ref[:, h * B_DH:(h + 1) * B_DH] = (acc_ref[h] * inv).T.astype(ob_ref.dtype)


def dsa(q, qi, wi, k_all, v_all, ki_all, q_chunk, k_chunk, nkeys, topk=TOPK):
    B, Tq, _ = q.shape
    Tk = k_all.shape[1]
    assert Tq % LANES == 0 and QB % LANES == 0
    nq, nkt = -(-Tq // QB), -(-Tk // KT)
    padq = lambda a: jnp.pad(a, ((0, 0), (0, nq * QB - Tq), (0, 0)))
    padk = lambda a: jnp.pad(a, ((0, 0), (0, nkt * KT - Tk), (0, 0)))
    qt = jnp.swapaxes(padq(q), 1, 2)
    qit = jnp.swapaxes(padq(qi), 1, 2)
    wit = jnp.swapaxes(padq(wi), 1, 2)
    k4 = padk(k_all.astype(BF16)).reshape(B, nkt, KT, B_DH)
    vt4 = jnp.swapaxes(padk(v_all.astype(BF16)).reshape(B, nkt, KT, B_DH), 2, 3)
    ki4 = padk(ki_all.astype(BF16)).reshape(B, nkt, KT, IDX_DIM)
    kc = jnp.pad(k_chunk.astype(jnp.int32), (0, nkt * KT - Tk), constant_values=NO_CHUNK)
    kc = jnp.broadcast_to(kc.reshape(nkt, KT, 1), (nkt, KT, QB))
    qc = jnp.pad(q_chunk.astype(jnp.int32), (0, nq * QB - Tq), constant_values=NO_QUERY).reshape(nq, 1, QB)
    per = QB // LANES
    nkeys = jnp.pad(nkeys.astype(jnp.int32), (0, nq * per - nkeys.shape[0])).reshape(nq, per).max(axis=1)
    nkb = (nkeys + (KT - 1)) // KT
    tri = jnp.asarray(np.tril(np.ones((KT, KT), np.float32), -1), BF16)
    grid_spec = pltpu.PrefetchScalarGridSpec(
        num_scalar_prefetch=1,
        grid=(B, nq),
        in_specs=[pl.BlockSpec((None, B_QW, QB), lambda b, i, n: (b, 0, i)),
                  pl.BlockSpec((None, IDX_HEADS * IDX_DIM, QB), lambda b, i, n: (b, 0, i)),
                  pl.BlockSpec((None, IDX_HEADS, QB), lambda b, i, n: (b, 0, i)),
                  pl.BlockSpec((None, 1, QB), lambda b, i, n: (i, 0, 0)),
                  pl.BlockSpec((nkt, KT, QB), lambda b, i, n: (0, 0, 0)),
                  pl.BlockSpec((None, nkt, KT, B_DH), lambda b, i, n: (b, 0, 0, 0)),
                  pl.BlockSpec((None, nkt, B_DH, KT), lambda b, i, n: (b, 0, 0, 0)),
                  pl.BlockSpec((None, nkt, KT, IDX_DIM), lambda b, i, n: (b, 0, 0, 0)),
                  pl.BlockSpec((KT, KT), lambda b, i, n: (0, 0))],
        out_specs=pl.BlockSpec((None, QB, B_QW), lambda b, i, n: (b, i, 0)),
        scratch_shapes=[pltpu.VMEM((nkt, KT, QB), jnp.int32),
                        pltpu.VMEM((B_HEADS, B_DH, QB), F32)],
    )
    ob = pl.pallas_call(
        functools.partial(_dsa_kernel, topk=topk),
        grid_spec=grid_spec,
        out_shape=jax.ShapeDtypeStruct((B, nq * QB, B_QW), BF16),
        compiler_params=_cparams(("parallel", "arbitrary")),
        name="dsa",
    )(nkb, qt, qit, wit, qc, kc, k4, vt4, ki4, tri)
    return ob[:, :Tq]


def _outproj_kernel(h_ref, oa_ref, ob_ref, w_ref, y_ref):
    y = jnp.dot(oa_ref[...], w_ref[0:A_VW, :], preferred_element_type=F32)
    y = y + jnp.dot(ob_ref[...], w_ref[A_VW:A_VW + B_QW, :], preferred_element_type=F32)
    y_ref[...] = h_ref[...] + y


def outproj(h, oa, ob, w, tm):
    G, R, D = h.shape
    assert R % tm == 0
    row = lambda c: pl.BlockSpec((None, tm, c), lambda b, i: (b, i, 0))
    return pl.pallas_call(
        _outproj_kernel,
        grid=(G, R // tm),
        in_specs=[row(D), row(A_VW), row(B_QW), pl.BlockSpec((A_VW + B_QW, D), lambda b, i: (0, 0))],
        out_specs=row(D),
        out_shape=jax.ShapeDtypeStruct((G, R, D), F32),
        compiler_params=_cparams(("parallel", "parallel")),
        name="outproj",
    )(h, oa, ob, w)


def _pool_kernel(h_ref, u_ref, halo_ref, w_ref, sc_ref, y_ref, *, tm, pos0, zero_first_halo):
    i = pl.program_id(1)
    halo = halo_ref[...]
    if zero_first_halo:
        halo = jnp.where(i > 0, halo, 0.0)
    u = u_ref[...]
    ue = jnp.concatenate([halo, u], axis=0)
    pos = pos0 + i * tm + lax.broadcasted_iota(jnp.int32, (tm, 1), 0)
    run = ue
    outs = []
    for g, w in enumerate(POOL_WINDOWS):
        lo = g * POOL_GROUP
        if g:
            run = run[:, POOL_GROUP:]
        run = run + pltpu.roll(run, w // 2, 0)
        cnt = jnp.clip(pos + 1, 1, w).astype(F32)
        mean = run[HALO:, 0:POOL_GROUP] / cnt
        d = (mean - u[:, lo:lo + POOL_GROUP]).astype(BF16)
        outs.append(jnp.dot(d, w_ref[g], preferred_element_type=F32))
    y = jnp.concatenate(outs, axis=1) * sc_ref[...]
    y_ref[...] = h_ref[...] + y


def pool(h, u, halo_src, w_grp, scale, tm, pos0, zero_first_halo):
    G, R, D = h.shape
    assert R % tm == 0 and tm % HALO == 0
    row = pl.BlockSpec((None, tm, D), lambda b, i: (b, i, 0))
    step = tm // HALO
    halo = pl.BlockSpec((None, HALO, D), lambda b, i: (b, jnp.maximum(i * step - 1, 0), 0))
    return pl.pallas_call(
        functools.partial(_pool_kernel, tm=tm, pos0=pos0, zero_first_halo=zero_first_halo),
        grid=(G, R // tm),
        in_specs=[row, row, halo,
                  pl.BlockSpec((len(POOL_WINDOWS), POOL_GROUP, POOL_GROUP), lambda b, i: (0, 0, 0)),
                  pl.BlockSpec((1, D), lambda b, i: (0, 0))],
        out_specs=row,
        out_shape=jax.ShapeDtypeStruct((G, R, D), F32),
        compiler_params=_cparams(("parallel", "parallel")),
        name="pool",
    )(h, u, halo_src, w_grp, scale.reshape(1, D))


def _hgrn_params(lb, o_gain):
    lb_pos = lb > 0
    lb_safe = jnp.where(lb_pos, lb, 0.5)
    rows = [jnp.log(lb_safe), jnp.log1p(-lb_safe), 1.0 - lb, lb_pos.astype(F32), o_gain.astype(F32)]
    rows += [jnp.zeros_like(lb)] * 3
    return jnp.stack(rows).astype(F32)


def kernel(x_prompt, x_sample, cache_k, cache_v, cache_ki, state_hgrn, state_pool, meta_tokens, norm_ffn1, ffn1_wg, ffn1_wu, ffn1_wd, norm_mix, norm_ffn2, ffn2_wg, ffn2_wu, ffn2_wd, ab_w_in, ab_w_out, hgrn_lb_logits, hgrn_out_norm, attn_q_norm, attn_k_norm, pool_w, pool_scale):
    Bp, Sp, D = x_prompt.shape
    Bs, Ts, _ = x_sample.shape
    P = cache_k.shape[2]
    Tp = FRONT_PAD + N_META + Sp
    assert Tp % LANES == 0 and Ts == CHUNK and P % LANES == 0
    tm_p = Tp // 8
    Rs = Bs * Ts

    hp = jnp.concatenate([jnp.zeros((Bp, FRONT_PAD, D), F32),
                          jnp.broadcast_to(meta_tokens.astype(F32)[None], (Bp, N_META, D)), x_prompt], axis=1)
    hs = x_sample.reshape(1, Rs, D)

    row_p = jnp.arange(Tp, dtype=jnp.int32)
    pos_p = row_p - FRONT_PAD
    chunk_p = jnp.where(pos_p < 0, -2, jnp.where(pos_p < N_META, -1, (pos_p - N_META) // CHUNK))
    kchunk_p = jnp.where(pos_p < 0, jnp.int32(NO_CHUNK), chunk_p)
    nkeys_p = (jnp.arange(Tp // LANES, dtype=jnp.int32) + 1) * LANES
    tabs_p = rope_tables(pos_p)

    pos_s = P + jnp.arange(Ts, dtype=jnp.int32)
    qchunk_s = jnp.concatenate([pos_s // CHUNK, jnp.full((LANES - Ts,), -2, jnp.int32)])
    kchunk_s = jnp.arange(P + Ts, dtype=jnp.int32) // CHUNK
    nkeys_s = jnp.full((1,), P + Ts, jnp.int32)
    tabs_s = [jnp.tile(t, (Bs, 1)) for t in rope_tables(pos_s)]

    lb_soft = jax.nn.softmax(hgrn_lb_logits.astype(F32), axis=0)
    lb_all = jnp.cumsum(lb_soft, axis=0) - lb_soft[0]

    cast = lambda w: w.astype(BF16)
    kp_l, vp_l, kip_l, sp_l, pp_l = [], [], [], [], []
    ks_l, vs_l, kis_l, ss_l, ps_l = [], [], [], [], []
    for l in range(DEPTH):
        j = l // 2
        even = l % 2 == 0
        w1 = (cast(ffn1_wg[l]), cast(ffn1_wu[l]), cast(ffn1_wd[l]))
        w2 = (cast(ffn2_wg[l]), cast(ffn2_wu[l]), cast(ffn2_wd[l]))
        udt = BF16 if even else F32
        hp, up = ffn(hp, norm_ffn1[l], *w1, g2=norm_mix[l], u_dtype=udt, tm=tm_p)
        hs, us = ffn(hs, norm_ffn1[l], *w1, g2=norm_mix[l], u_dtype=udt, tm=Rs)
        if even:
            w_in = jnp.pad(cast(ab_w_in[j]), ((0, 0), (0, IN_COLS_PAD - ab_w_in.shape[2])))
            w_out = cast(ab_w_out[j])
            par = _hgrn_params(lb_all[j], hgrn_out_norm[j])
            za, q, kv, qi, slab = inproj(up, w_in, attn_q_norm[j], attn_k_norm[j], tabs_p, tm_p)
            s0 = jnp.zeros((Bp, A_HEADS, A_DK, A_DK), F32)
            oa, sft = hgrn(za, par, s0, LANES)
            k_new, v_new, ki_new = kv[..., :B_DH], kv[..., B_DH:], slab[..., :IDX_DIM]
            wi = slab[..., IDX_DIM:IDX_DIM + IDX_HEADS]
            ob = dsa(q, qi, wi, k_new, v_new, ki_new, chunk_p, kchunk_p, nkeys_p)
            hp = outproj(hp, oa, ob, w_out, tm_p)
            kp_l.append(k_new[:, FRONT_PAD:]); vp_l.append(v_new[:, FRONT_PAD:]); kip_l.append(ki_new[:, FRONT_PAD:])
            sp_l.append(jnp.swapaxes(sft, 2, 3))
            za, q, kv, qi, slab = inproj(us, w_in, attn_q_norm[j], attn_k_norm[j], tabs_s, Rs)
            s0 = jnp.swapaxes(state_hgrn[j].astype(F32), 2, 3)
            oa, sft = hgrn(za.reshape(Bs, Ts, A_COLS), par, s0, Ts)
            kv, slab = kv.reshape(Bs, Ts, -1), slab.reshape(Bs, Ts, -1)
            k_new, v_new, ki_new = kv[..., :B_DH], kv[..., B_DH:], slab[..., :IDX_DIM]
            wi = slab[..., IDX_DIM:IDX_DIM + IDX_HEADS]
            padq = lambda a: jnp.pad(a.reshape(Bs, Ts, -1), ((0, 0), (0, LANES - Ts), (0, 0)))
            cat = lambda past, new: jnp.concatenate([past.astype(F32), new], axis=1)
            ob = dsa(padq(q), padq(qi), padq(wi), cat(cache_k[j], k_new), cat(cache_v[j], v_new),
                     cat(cache_ki[j], ki_new), qchunk_s, kchunk_s, nkeys_s)
            hs = outproj(hs, oa.reshape(1, Rs, A_VW), ob[:, :Ts].reshape(1, Rs, B_QW), w_out, Rs)
            ks_l.append(k_new); vs_l.append(v_new); kis_l.append(ki_new)
            ss_l.append(jnp.swapaxes(sft, 2, 3).astype(state_hgrn.dtype))
        else:
            pw = cast(pool_w[j])
            hp = pool(hp, up, up, pw, pool_scale[j], tm_p, -FRONT_PAD, True)
            pp_l.append(up[:, Tp - POOL_HIST:])
            us3 = us.reshape(Bs, Ts, D)
            hist = state_pool[j].astype(F32)
            halo = jnp.pad(hist, ((0, 0), (HALO - POOL_HIST, 0), (0, 0)))
            hs = pool(hs.reshape(Bs, Ts, D), us3, halo, pw, pool_scale[j], Ts, P, False).reshape(1, Rs, D)
            ps_l.append(jnp.concatenate([hist, us3], axis=1)[:, -POOL_HIST:])
        hp = ffn(hp, norm_ffn2[l], *w2, tm=tm_p)
        hs = ffn(hs, norm_ffn2[l], *w2, tm=Rs)

    y_prompt = hp[:, FRONT_PAD + N_META:]
    return (y_prompt, hs.reshape(Bs, Ts, D),
            jnp.stack(kp_l), jnp.stack(vp_l), jnp.stack(kip_l), jnp.stack(sp_l), jnp.stack(pp_l),
            jnp.stack(ks_l), jnp.stack(vs_l), jnp.stack(kis_l), jnp.stack(ss_l), jnp.stack(ps_l))
```

```python
import functools

import numpy as np
import jax
import jax.numpy as jnp
from jax import lax
from jax.experimental import pallas as pl
from jax.experimental.pallas import tpu as pltpu

F32 = jnp.float32
BF16 = jnp.bfloat16

D_MODEL = 1024
D_FF = 2816
DEPTH = 4
CHUNK = 64
N_META = 16
A_HEADS = 4
A_DK = 128
A_VW = 512
B_HEADS = 4
B_DH = 128
B_QW = 512
IDX_HEADS = 4
IDX_DIM = 64
TOPK = 256
ROPE_THETA = 500000.0
ROT_FRAC = 4
POOL_WINDOWS = (2, 4, 8, 16)
POOL_GROUP = D_MODEL // len(POOL_WINDOWS)
POOL_HIST = max(POOL_WINDOWS) - 1
RMS_EPS = 1e-6
NEG_BIG = -1e30

LANES = 128
SUBLANES = 8
BF16_ROWS = 16
MXU_N = 256
FFN_CHUNKS = 2
CAST_ROWS = 256
KT = 256
QB = 256
NO_CHUNK = 2 ** 30
NO_QUERY = -2
LOG2E = 1.4426950408889634
HALO = 16
FRONT_PAD = 2 * CHUNK - N_META
IN_COLS_PAD = 3200
A_COLS = 2048
VMEM_LIMIT = 56 * 1024 * 1024
HI_MASK = -65536
MIN_NORMAL_BITS = 0x00800000


def _cparams(sem):
    return pltpu.CompilerParams(dimension_semantics=sem, vmem_limit_bytes=VMEM_LIMIT)


def _sigmoid(x):
    return 1.0 / (1.0 + jnp.exp(-x))


def _rms(x, g):
    return x * lax.rsqrt(jnp.mean(x * x, axis=-1, keepdims=True) + RMS_EPS) * g


def _ffn_kernel(*refs, row_tiles, ff_chunks, emit_u, mix):
    refs = list(refs)
    x_ref = refs.pop(0)
    if mix:
        oa_ref, ob_ref, wo_ref = refs.pop(0), refs.pop(0), refs.pop(0)
    g_ref, wg_ref, wu_ref, wd_ref = refs[:4]
    if emit_u:
        g2_ref, y_ref, u_ref = refs[4:]
    else:
        (y_ref,) = refs[4:]
    for r0, r1 in row_tiles:
        x = x_ref[r0:r1, :]
        if mix:
            x = x + jnp.dot(oa_ref[r0:r1, :], wo_ref[0:A_VW, :], preferred_element_type=F32)
            x = x + jnp.dot(ob_ref[r0:r1, :], wo_ref[A_VW:A_VW + B_QW, :], preferred_element_type=F32)
        xn = _rms(x, g_ref[...]).astype(BF16)
        acc = None
        for c0, c1 in ff_chunks:
            gate = jnp.dot(xn, wg_ref[:, c0:c1], preferred_element_type=F32)
            up = jnp.dot(xn, wu_ref[:, c0:c1], preferred_element_type=F32)
            act = (gate * _sigmoid(gate) * up).astype(BF16)
            part = jnp.dot(act, wd_ref[c0:c1, :], preferred_element_type=F32)
            acc = part if acc is None else acc + part
        y = x + 0.5 * acc
        y_ref[r0:r1, :] = y
        if emit_u:
            u_ref[r0:r1, :] = _rms(y, g2_ref[...]).astype(u_ref.dtype)


def _ffn_tiling(tm, F):
    half = (tm // (2 * BF16_ROWS)) * BF16_ROWS
    row_tiles = [(0, half), (half, tm)] if half else [(0, tm)]
    n_col = -(-F // MXU_N)
    per = -(-n_col // FFN_CHUNKS)
    edges = [min(c * per * MXU_N, F) for c in range(FFN_CHUNKS + 1)]
    return row_tiles, [(a, b) for a, b in zip(edges[:-1], edges[1:]) if b > a]


def ffn(x, g, wg, wu, wd, layer, g2=None, u_dtype=BF16, tm=528, mix=None):
    G, R, D = x.shape
    F = wg.shape[2]
    tm = min(tm, R)
    assert R % tm == 0
    row_tiles, ff_chunks = _ffn_tiling(tm, F)
    emit_u = g2 is not None
    row = lambda c: pl.BlockSpec((None, tm, c), lambda b, i: (b, i, 0))
    vec = pl.BlockSpec((1, D), lambda b, i: (0, 0))
    resident = lambda shape: pl.BlockSpec(shape, lambda b, i: (0, 0), pipeline_mode=pl.Buffered(1))
    in_specs, args = [row(D)], [x]
    if mix is not None:
        oa, ob, w_out = mix
        in_specs += [row(A_VW), row(B_QW), resident(w_out.shape)]
        args += [oa, ob, w_out]
    stacked = lambda r, c: pl.BlockSpec((None, r, c), lambda b, i: (layer, 0, 0), pipeline_mode=pl.Buffered(1))
    in_specs += [vec, stacked(D, F), stacked(D, F), stacked(F, D)]
    args += [g.reshape(1, D), wg, wu, wd]
    row = row(D)
    out_shape = [jax.ShapeDtypeStruct((G, R, D), F32)]
    out_specs = [row]
    if emit_u:
        in_specs.append(vec)
        args.append(g2.reshape(1, D))
        out_shape.append(jax.ShapeDtypeStruct((G, R, D), u_dtype))
        out_specs.append(row)
    out = pl.pallas_call(
        functools.partial(_ffn_kernel, row_tiles=row_tiles, ff_chunks=ff_chunks, emit_u=emit_u,
                          mix=mix is not None),
        grid=(G, R // tm),
        in_specs=in_specs,
        out_specs=out_specs,
        out_shape=out_shape,
        compiler_params=_cparams(("parallel", "parallel")),
        name="ffn",
    )(*args)
    return tuple(out) if emit_u else out[0]


def _rope(x, c, s_lo, s_hi, half):
    return x * c + pltpu.roll(x, half, 1) * s_lo + pltpu.roll(x, LANES - half, 1) * s_hi


def _inproj_kernel(u_ref, w_ref, qg_ref, kg_ref, cb_ref, sbl_ref, sbh_ref, ci_ref, sil_ref, sih_ref,
                   za_ref, q_ref, kv_ref, qi_ref, slab_ref):
    u = u_ref[...]
    za_ref[...] = jnp.dot(u, w_ref[:, 0:A_COLS], preferred_element_type=F32)
    hb = B_DH // ROT_FRAC // 2
    hi = IDX_DIM // ROT_FRAC // 2
    cb, sbl, sbh = cb_ref[...], sbl_ref[...], sbh_ref[...]
    ci, sil, sih = ci_ref[...], sil_ref[...], sih_ref[...]
    zq = jnp.dot(u, w_ref[:, A_COLS:A_COLS + B_QW], preferred_element_type=F32)
    for h in range(B_HEADS):
        xh = _rms(zq[:, h * B_DH:(h + 1) * B_DH], qg_ref[...])
        q_ref[:, h * B_DH:(h + 1) * B_DH] = _rope(xh, cb, sbl, sbh, hb).astype(q_ref.dtype)
    c0 = A_COLS + B_QW
    zkv = jnp.dot(u, w_ref[:, c0:c0 + 2 * B_DH], preferred_element_type=F32)
    kv_ref[:, 0:B_DH] = _rope(_rms(zkv[:, 0:B_DH], kg_ref[...]), cb, sbl, sbh, hb)
    kv_ref[:, B_DH:2 * B_DH] = zkv[:, B_DH:2 * B_DH]
    c1 = c0 + 2 * B_DH
    zi = jnp.dot(u, w_ref[:, c1:c1 + IDX_HEADS * IDX_DIM + LANES], preferred_element_type=F32)
    for t in range(IDX_HEADS * IDX_DIM // LANES):
        qi_ref[:, t * LANES:(t + 1) * LANES] = _rope(zi[:, t * LANES:(t + 1) * LANES], ci, sil, sih, hi).astype(qi_ref.dtype)
    tail = zi[:, IDX_HEADS * IDX_DIM:]
    lane = lax.broadcasted_iota(jnp.int32, tail.shape, 1)
    slab_ref[...] = jnp.where(lane < IDX_DIM, _rope(tail, ci, sil, sih, hi), tail)


def inproj(u, w_pad, q_gain, k_gain, tabs, tm):
    G, R, D = u.shape
    assert R % tm == 0
    row = lambda c: pl.BlockSpec((None, tm, c), lambda b, i: (b, i, 0))
    tab = pl.BlockSpec((tm, LANES), lambda b, i: (i, 0))
    vec = pl.BlockSpec((1, LANES), lambda b, i: (0, 0))
    return pl.pallas_call(
        _inproj_kernel,
        grid=(G, R // tm),
        in_specs=[row(D), pl.BlockSpec((D, IN_COLS_PAD), lambda b, i: (0, 0)), vec, vec] + [tab] * 6,
        out_specs=[row(A_COLS), row(B_QW), row(2 * B_DH), row(IDX_HEADS * IDX_DIM), row(LANES)],
        out_shape=[jax.ShapeDtypeStruct((G, R, A_COLS), F32),
                   jax.ShapeDtypeStruct((G, R, B_QW), BF16),
                   jax.ShapeDtypeStruct((G, R, 2 * B_DH), F32),
                   jax.ShapeDtypeStruct((G, R, IDX_HEADS * IDX_DIM), BF16),
                   jax.ShapeDtypeStruct((G, R, LANES), F32)],
        compiler_params=_cparams(("parallel", "parallel")),
        name="inproj",
    )(u, w_pad, q_gain.reshape(1, LANES), k_gain.reshape(1, LANES), *tabs)


def rope_tables(pos):
    pos = pos.astype(F32)
    out = []
    for d in (B_DH, IDX_DIM):
        half = d // ROT_FRAC // 2
        inv = ROPE_THETA ** (-jnp.arange(half, dtype=F32) / half)
        ang = pos[:, None] * inv[None, :]
        cos, sin = jnp.cos(ang), jnp.sin(ang)
        n = pos.shape[0]
        z = lambda w: jnp.zeros((n, w), F32)
        c = jnp.concatenate([cos, cos, jnp.ones((n, d - 2 * half), F32)], axis=1)
        s_lo = jnp.concatenate([z(half), sin, z(d - 2 * half)], axis=1)
        s_hi = jnp.concatenate([-sin, z(d - half)], axis=1)
        rep = LANES // d
        out += [jnp.tile(c, (1, rep)), jnp.tile(s_lo, (1, rep)), jnp.tile(s_hi, (1, rep))]
    return out


def _hgrn_consts(C):
    levels = []
    m = C // 2
    while m >= 1:
        levels.append(m)
        m //= 2
    t = np.arange(C)[:, None]
    r = np.arange(C)[None, :]
    mats = [(r <= t)]
    lvl = np.full((C, C), -1, np.int32)
    lvl[np.arange(C), np.arange(C)] = len(levels)
    for li, m in enumerate(levels):
        anchor = (t // (2 * m)) * (2 * m) + m - 1
        second = t > anchor
        mats.append(np.where(second, (r > anchor) & (r <= t), (r > t) & (r <= anchor)))
        same = (t // (2 * m)) == (r // (2 * m))
        split = same & ((t % (2 * m)) >= m) & ((r % (2 * m)) < m)
        lvl[split] = li
    dmat = np.concatenate([x.astype(np.float32) for x in mats], axis=0)
    return levels, jnp.asarray(dmat, BF16), jnp.asarray(lvl)


def _nt(a, b):
    return lax.dot_general(a, b, (((1,), (1,)), ((), ())), preferred_element_type=F32)


def _tn(a, b):
    return lax.dot_general(a, b, (((0,), (0,)), ((), ())), preferred_element_type=F32)


def _hgrn_kernel(z_ref, d_ref, lvl_ref, par_ref, s0_ref, oa_ref, sf_ref, st_ref, *, C, levels):
    c = pl.program_id(1)
    dk = A_DK

    @pl.when(c == 0)
    def _():
        st_ref[...] = s0_ref[...]

    W = A_HEADS * dk
    aq = z_ref[:, 0:W]
    af = z_ref[:, W:2 * W]
    ai = z_ref[:, 2 * W:3 * W]
    ag = z_ref[:, 3 * W:4 * W]
    log_lb, log_1m, one_m, lb_pos, o_gain = (par_ref[i:i + 1, :] for i in range(5))
    e = jnp.exp(-jnp.abs(af))
    ls = jnp.minimum(af, 0.0) - jnp.log1p(e)
    b2 = log_1m + ls
    lae = jnp.maximum(log_lb, b2) + jnp.log1p(jnp.exp(-jnp.abs(log_lb - b2)))
    log_f = jnp.where(lb_pos > 0.0, lae, ls)
    ka = one_m * (jnp.where(af >= 0.0, e, 1.0) / (1.0 + e))
    qa = aq * _sigmoid(aq)
    g_hi = log_f.astype(BF16)
    r1 = log_f - g_hi.astype(F32)
    g_mid = r1.astype(BF16)
    g_lo = (r1 - g_mid.astype(F32)).astype(BF16)
    dm = d_ref[...]
    X = (jnp.dot(dm, g_hi, preferred_element_type=F32) + jnp.dot(dm, g_mid, preferred_element_type=F32)
         + jnp.dot(dm, g_lo, preferred_element_type=F32))
    lvl = lvl_ref[...]
    rows = lax.broadcasted_iota(jnp.int32, (C, 1), 0)
    nl = len(levels)
    b = X[0:C, :]
    b_last = b[C - 1:C, :]
    qe = (qa * jnp.exp(b)).astype(BF16)
    ke = (ka * jnp.exp(b_last - b)).astype(BF16)
    decay = jnp.exp(b_last)
    qb, kb, vb = qa.astype(BF16), ka.astype(BF16), ai.astype(BF16)
    zs = [(jnp.where((rows // m) % 2 == 1, qa, ka) * jnp.exp(X[(1 + li) * C:(2 + li) * C, :])).astype(BF16)
          for li, m in enumerate(levels)]
    gate = o_gain * _sigmoid(ag)
    for h in range(A_HEADS):
        sl = slice(h * dk, (h + 1) * dk)
        st = st_ref[h]
        o = _nt(qe[:, sl], st.astype(BF16))
        att = jnp.where(lvl == nl, _nt(qb[:, sl], kb[:, sl]), 0.0)
        for li in range(nl):
            att = att + jnp.where(lvl == li, _nt(zs[li][:, sl], zs[li][:, sl]), 0.0)
        o = o + jnp.dot(att.astype(BF16), vb[:, sl], preferred_element_type=F32)
        st_ref[h] = st * decay[:, sl] + _tn(vb[:, sl], ke[:, sl])
        on = o * lax.rsqrt(jnp.mean(o * o, axis=-1, keepdims=True) + RMS_EPS)
        oa_ref[:, sl] = (on * gate[:, sl]).astype(oa_ref.dtype)

    @pl.when(c == pl.num_programs(1) - 1)
    def _():
        sf_ref[...] = st_ref[...]


def hgrn(za, par, s0t, C):
    B, T, _ = za.shape
    assert T % C == 0
    levels, dmat, lvl = _hgrn_consts(C)
    W = A_HEADS * A_DK
    st_spec = pl.BlockSpec((None, A_HEADS, A_DK, A_DK), lambda b, c: (b, 0, 0, 0))
    return pl.pallas_call(
        functools.partial(_hgrn_kernel, C=C, levels=levels),
        grid=(B, T // C),
        in_specs=[pl.BlockSpec((None, C, A_COLS), lambda b, c: (b, c, 0)),
                  pl.BlockSpec(dmat.shape, lambda b, c: (0, 0)),
                  pl.BlockSpec((C, C), lambda b, c: (0, 0)),
                  pl.BlockSpec((8, W), lambda b, c: (0, 0)),
                  st_spec],
        out_specs=[pl.BlockSpec((None, C, W), lambda b, c: (b, c, 0)), st_spec],
        out_shape=[jax.ShapeDtypeStruct((B, T, W), BF16),
                   jax.ShapeDtypeStruct((B, A_HEADS, A_DK, A_DK), F32)],
        scratch_shapes=[pltpu.VMEM((A_HEADS, A_DK, A_DK), F32)],
        compiler_params=_cparams(("parallel", "arbitrary")),
        name="hgrn",
    )(za, dmat, lvl, par, s0t)


def _tree_sum(parts):
    while len(parts) > 1:
        parts = [parts[r] + parts[r + 1] if r + 1 < len(parts) else parts[r] for r in range(0, len(parts), 2)]
    return parts[0]


def _fori_pairs(n, tile, init):
    acc = lax.fori_loop(0, n // 2, lambda p, a: a + (tile(2 * p) + tile(2 * p + 1)), init)
    return lax.fori_loop(n // 2 * 2, n, lambda j, a: a + tile(j), acc)


def _dsa_kernel(nkb_ref,qt_ref, qit_ref, wit_ref, qc_ref, kc_ref, k_ref, vt_ref, ki_ref, tri_ref,
                ob_ref, key_ref, hi_ref, acc_ref, *, topk):
    i = pl.program_id(1)
    nk = nkb_ref[i]
    qc = qc_ref[...]
    qit = qit_ref[...]
    rhs_i = jnp.concatenate([qit[h * IDX_DIM:(h + 1) * IDX_DIM, :] for h in range(IDX_HEADS)], axis=1)
    qt = qt_ref[...]
    rhs_q = jnp.concatenate([qt[h * B_DH:(h + 1) * B_DH, :] for h in range(B_HEADS)], axis=1)
    wsc = wit_ref[...] * (IDX_HEADS ** -0.5 * IDX_DIM ** -0.5)

    def score_body(j, carry):
        sc = jnp.dot(ki_ref[j], rhs_i, preferred_element_type=F32)
        s = jnp.maximum(sc[:, 0:QB], 0.0) * wsc[0:1, :]
        for h in range(1, IDX_HEADS):
            s = s + jnp.maximum(sc[:, h * QB:(h + 1) * QB], 0.0) * wsc[h:h + 1, :]
        s = jnp.where(kc_ref[j] <= qc, s, NEG_BIG)
        bits = lax.bitcast_convert_type(s, jnp.int32)
        bits = jnp.where((bits & 0x7FFFFFFF) < MIN_NORMAL_BITS, 0, bits)
        key_ref[j] = bits ^ ((bits >> 31) & 0x7FFFFFFF)
        hi_ref[j] = lax.bitcast_convert_type(bits & HI_MASK, F32).astype(BF16)
        return carry

    lax.fori_loop(0, nk, score_body, 0)

    def count_hi(cand16):
        cand16 = jnp.where((cand16 >= 1) & (cand16 < MIN_NORMAL_BITS >> 16), MIN_NORMAL_BITS >> 16, cand16)
        pat = cand16 ^ ((cand16 >> 15) & 0x7FFF)
        cand = lax.bitcast_convert_type(jnp.left_shift(pat, 16), F32).astype(BF16)

        def tile(j):
            hit = jnp.where(hi_ref[j] >= cand, jnp.bfloat16(1), jnp.bfloat16(0))
            parts = [hit[r * BF16_ROWS:(r + 1) * BF16_ROWS, :] for r in range(KT // BF16_ROWS)]
            return _tree_sum(parts).astype(F32)
        acc = _fori_pairs(nk, tile, jnp.zeros((BF16_ROWS, QB), F32))
        return jnp.sum(acc, axis=0, keepdims=True).astype(jnp.int32)

    def count(pred):
        def tile(j):
            hit = jnp.where(pred(key_ref[j]), 1, 0)
            return _tree_sum([hit[r * SUBLANES:(r + 1) * SUBLANES, :] for r in range(KT // SUBLANES)])
        acc = _fori_pairs(nk, tile, jnp.zeros((SUBLANES, QB), jnp.int32))
        return jnp.sum(acc, axis=0, keepdims=True)

    c0 = count_hi(jnp.zeros((1, QB), jnp.int32))
    hi0 = jnp.where(c0 >= topk, 0, -(2 ** 15)).astype(jnp.int32)
    above0 = jnp.where(c0 >= topk, 0, c0)

    def hi_body(it, carry):
        thr, above = carry
        cand = thr | jnp.left_shift(jnp.int32(1), 14 - it)
        cnt = count_hi(cand)
        ok = cnt >= topk
        return jnp.where(ok, cand, thr), jnp.where(ok, above, cnt)

    hi16, above1 = lax.fori_loop(0, 15, hi_body, (hi0, above0))

    def lo_body(it, carry):
        thr, above = carry
        cand = thr | jnp.left_shift(jnp.int32(1), 15 - it)
        cnt = count(lambda kv: kv >= cand)
        ok = cnt >= topk
        return jnp.where(ok, cand, thr), jnp.where(ok, above, cnt)

    thr, above = lax.fori_loop(0, 16, lo_body, (jnp.left_shift(hi16, 16), above1))
    need = (topk - above).astype(F32)

    acc_ref[...] = jnp.zeros_like(acc_ref)
    c2 = B_DH ** -0.5 * LOG2E
    tri = tri_ref[...]

    def att_body(j, carry):
        ms, ls, tie_seen = carry
        kv = key_ref[j]
        eq = kv == thr
        eqf = jnp.where(eq, 1.0, 0.0)
        before = jnp.dot(tri, eqf.astype(BF16), preferred_element_type=F32) + tie_seen
        take = jnp.where(kv > thr, 1, jnp.where(eq, jnp.where(before < need, 1, 0), 0))
        sel = jnp.where(kc_ref[j] <= qc, take, 0) > 0
        tie_seen = tie_seen + jnp.sum(eqf, axis=0, keepdims=True)
        s_all = jnp.dot(k_ref[j], rhs_q, preferred_element_type=F32)
        new_ms, new_ls, ps, alphas = [], [], [], []
        for h in range(B_HEADS):
            s = jnp.where(sel, s_all[:, h * QB:(h + 1) * QB], NEG_BIG)
            m_new = jnp.maximum(ms[h], jnp.max(s, axis=0, keepdims=True))
            alpha = jnp.exp2((ms[h] - m_new) * c2)
            p = jnp.where(sel, jnp.exp2((s - m_new) * c2), 0.0)
            new_ls.append(alpha * ls[h] + jnp.sum(p, axis=0, keepdims=True))
            new_ms.append(m_new)
            alphas.append(alpha)
            ps.append(p.astype(BF16))
        pv = jnp.dot(vt_ref[j], jnp.concatenate(ps, axis=1), preferred_element_type=F32)
        for h in range(B_HEADS):
            acc_ref[h] = alphas[h] * acc_ref[h] + pv[:, h * QB:(h + 1) * QB]
        return tuple(new_ms), tuple(new_ls), tie_seen

    init_m = tuple(jnp.full((1, QB), NEG_BIG, F32) for _ in range(B_HEADS))
    init_l = tuple(jnp.zeros((1, QB), F32) for _ in range(B_HEADS))
    _, ls, _ = lax.fori_loop(0, nk, att_body, (init_m, init_l, jnp.zeros((1, QB), F32)))
    for h in range(B_HEADS):
        inv = jnp.where(ls[h] > 0.0, 1.0 / jnp.where(ls[h] > 0.0, ls[h], 1.0), 0.0)
        ob_ref[:, h * B_DH:(h + 1) * B_DH] = (acc_ref[h] * inv).T.astype(ob_ref.dtype)


def dsa(q, qi, wi, k_all, v_all, ki_all, q_chunk, k_chunk, nkeys, topk=TOPK):
    B, Tq, _ = q.shape
    Tk = k_all.shape[1]
    assert Tq % LANES == 0 and QB % LANES == 0
    nq, nkt = -(-Tq // QB), -(-Tk // KT)
    padq = lambda a: jnp.pad(a, ((0, 0), (0, nq * QB - Tq), (0, 0)))
    padk = lambda a: jnp.pad(a, ((0, 0), (0, nkt * KT - Tk), (0, 0)))
    qt = jnp.swapaxes(padq(q), 1, 2)
    qit = jnp.swapaxes(padq(qi), 1, 2)
    wit = jnp.swapaxes(padq(wi), 1, 2)
    k4 = padk(k_all.astype(BF16)).reshape(B, nkt, KT, B_DH)
    vt4 = jnp.swapaxes(padk(v_all.astype(BF16)).reshape(B, nkt, KT, B_DH), 2, 3)
    ki4 = padk(ki_all.astype(BF16)).reshape(B, nkt, KT, IDX_DIM)
    kc = jnp.pad(k_chunk.astype(jnp.int32), (0, nkt * KT - Tk), constant_values=NO_CHUNK)
    kc = jnp.broadcast_to(kc.reshape(nkt, KT, 1), (nkt, KT, QB))
    qc = jnp.pad(q_chunk.astype(jnp.int32), (0, nq * QB - Tq), constant_values=NO_QUERY).reshape(nq, 1, QB)
    per = QB // LANES
    nkeys = jnp.pad(nkeys.astype(jnp.int32), (0, nq * per - nkeys.shape[0])).reshape(nq, per).max(axis=1)
    nkb = (nkeys + (KT - 1)) // KT
    tri = jnp.asarray(np.tril(np.ones((KT, KT), np.float32), -1), BF16)
    grid_spec = pltpu.PrefetchScalarGridSpec(
        num_scalar_prefetch=1,
        grid=(B, nq),
        in_specs=[pl.BlockSpec((None, B_QW, QB), lambda b, i, n: (b, 0, i)),
                  pl.BlockSpec((None, IDX_HEADS * IDX_DIM, QB), lambda b, i, n: (b, 0, i)),
                  pl.BlockSpec((None, IDX_HEADS, QB), lambda b, i, n: (b, 0, i)),
                  pl.BlockSpec((None, 1, QB), lambda b, i, n: (i, 0, 0)),
                  pl.BlockSpec((nkt, KT, QB), lambda b, i, n: (0, 0, 0)),
                  pl.BlockSpec((None, nkt, KT, B_DH), lambda b, i, n: (b, 0, 0, 0)),
                  pl.BlockSpec((None, nkt, B_DH, KT), lambda b, i, n: (b, 0, 0, 0)),
                  pl.BlockSpec((None, nkt, KT, IDX_DIM), lambda b, i, n: (b, 0, 0, 0)),
                  pl.BlockSpec((KT, KT), lambda b, i, n: (0, 0))],
        out_specs=pl.BlockSpec((None, QB, B_QW), lambda b, i, n: (b, i, 0)),
        scratch_shapes=[pltpu.VMEM((nkt, KT, QB), jnp.int32),
                        pltpu.VMEM((nkt, KT, QB), BF16),
                        pltpu.VMEM((B_HEADS, B_DH, QB), F32)],
    )
    ob = pl.pallas_call(
        functools.partial(_dsa_kernel, topk=topk),
        grid_spec=grid_spec,
        out_shape=jax.ShapeDtypeStruct((B, nq * QB, B_QW), BF16),
        compiler_params=_cparams(("parallel", "arbitrary")),
        name="dsa",
    )(nkb, qt, qit, wit, qc, kc, k4, vt4, ki4, tri)
    return ob[:, :Tq]


def _cast_kernel(x_ref, o_ref):
    o_ref[...] = x_ref[...].astype(o_ref.dtype)


def cast_bf16(w):
    L, R, C = w.shape
    assert R % CAST_ROWS == 0
    spec = pl.BlockSpec((None, CAST_ROWS, C), lambda l, i: (l, i, 0))
    return pl.pallas_call(
        _cast_kernel,
        grid=(L, R // CAST_ROWS),
        in_specs=[spec],
        out_specs=spec,
        out_shape=jax.ShapeDtypeStruct(w.shape, BF16),
        compiler_params=_cparams(("parallel", "parallel")),
        name="cast",
    )(w)


def _pool_kernel(h_ref, u_ref, halo_ref, w_ref, sc_ref, y_ref, *, tm, pos0, zero_first_halo):
    i = pl.program_id(1)
    halo = halo_ref[...]
    if zero_first_halo:
        halo = jnp.where(i > 0, halo, 0.0)
    u = u_ref[...]
    ue = jnp.concatenate([halo, u], axis=0)
    pos = pos0 + i * tm + lax.broadcasted_iota(jnp.int32, (tm, 1), 0)
    run = ue
    outs = []
    for g, w in enumerate(POOL_WINDOWS):
        lo = g * POOL_GROUP
        if g:
            run = run[:, POOL_GROUP:]
        run = run + pltpu.roll(run, w // 2, 0)
        cnt = jnp.clip(pos + 1, 1, w).astype(F32)
        mean = run[HALO:, 0:POOL_GROUP] / cnt
        d = (mean - u[:, lo:lo + POOL_GROUP]).astype(BF16)
        outs.append(jnp.dot(d, w_ref[g], preferred_element_type=F32))
    y = jnp.concatenate(outs, axis=1) * sc_ref[...]
    y_ref[...] = h_ref[...] + y


def pool(h, u, halo_src, w_grp, scale, tm, pos0, zero_first_halo):
    G, R, D = h.shape
    assert R % tm == 0 and tm % HALO == 0
    row = pl.BlockSpec((None, tm, D), lambda b, i: (b, i, 0))
    step = tm // HALO
    halo = pl.BlockSpec((None, HALO, D), lambda b, i: (b, jnp.maximum(i * step - 1, 0), 0))
    return pl.pallas_call(
        functools.partial(_pool_kernel, tm=tm, pos0=pos0, zero_first_halo=zero_first_halo),
        grid=(G, R // tm),
        in_specs=[row, row, halo,
                  pl.BlockSpec((len(POOL_WINDOWS), POOL_GROUP, POOL_GROUP), lambda b, i: (0, 0, 0)),
                  pl.BlockSpec((1, D), lambda b, i: (0, 0))],
        out_specs=row,
        out_shape=jax.ShapeDtypeStruct((G, R, D), F32),
        compiler_params=_cparams(("parallel", "parallel")),
        name="pool",
    )(h, u, halo_src, w_grp, scale.reshape(1, D))


def _hgrn_params(lb, o_gain):
    lb_pos = lb > 0
    lb_safe = jnp.where(lb_pos, lb, 0.5)
    rows = [jnp.log(lb_safe), jnp.log1p(-lb_safe), 1.0 - lb, lb_pos.astype(F32), o_gain.astype(F32)]
    rows += [jnp.zeros_like(lb)] * 3
    return jnp.stack(rows).astype(F32)


def kernel(x_prompt, x_sample, cache_k, cache_v, cache_ki, state_hgrn, state_pool, meta_tokens, norm_ffn1, ffn1_wg, ffn1_wu, ffn1_wd, norm_mix, norm_ffn2, ffn2_wg, ffn2_wu, ffn2_wd, ab_w_in, ab_w_out, hgrn_lb_logits, hgrn_out_norm, attn_q_norm, attn_k_norm, pool_w, pool_scale):
    Bp, Sp, D = x_prompt.shape
    Bs, Ts, _ = x_sample.shape
    P = cache_k.shape[2]
    Tp = FRONT_PAD + N_META + Sp
    assert Tp % LANES == 0 and Ts == CHUNK and P % LANES == 0
    tm_p = Tp // 8
    Rs = Bs * Ts

    hp = jnp.concatenate([jnp.zeros((Bp, FRONT_PAD, D), F32),
                          jnp.broadcast_to(meta_tokens.astype(F32)[None], (Bp, N_META, D)), x_prompt], axis=1)
    hs = x_sample.reshape(1, Rs, D)

    row_p = jnp.arange(Tp, dtype=jnp.int32)
    pos_p = row_p - FRONT_PAD
    chunk_p = jnp.where(pos_p < 0, -2, jnp.where(pos_p < N_META, -1, (pos_p - N_META) // CHUNK))
    kchunk_p = jnp.where(pos_p < 0, jnp.int32(NO_CHUNK), chunk_p)
    nkeys_p = (jnp.arange(Tp // LANES, dtype=jnp.int32) + 1) * LANES
    tabs_p = rope_tables(pos_p)

    pos_s = P + jnp.arange(Ts, dtype=jnp.int32)
    qchunk_s = jnp.concatenate([pos_s // CHUNK, jnp.full((LANES - Ts,), -2, jnp.int32)])
    kchunk_s = jnp.arange(P + Ts, dtype=jnp.int32) // CHUNK
    nkeys_s = jnp.full((1,), P + Ts, jnp.int32)
    tabs_s = [jnp.tile(t, (Bs, 1)) for t in rope_tables(pos_s)]

    lb_soft = jax.nn.softmax(hgrn_lb_logits.astype(F32), axis=0)
    lb_all = jnp.cumsum(lb_soft, axis=0) - lb_soft[0]

    cast = lambda w: w.astype(BF16)
    ffn1_w = [cast_bf16(w) for w in (ffn1_wg, ffn1_wu, ffn1_wd)]
    ffn2_w = [cast_bf16(w) for w in (ffn2_wg, ffn2_wu, ffn2_wd)]
    kp_l, vp_l, kip_l, sp_l, pp_l = [], [], [], [], []
    ks_l, vs_l, kis_l, ss_l, ps_l = [], [], [], [], []
    for l in range(DEPTH):
        j = l // 2
        even = l % 2 == 0
        udt = BF16 if even else F32
        hp, up = ffn(hp, norm_ffn1[l], *ffn1_w, l, g2=norm_mix[l], u_dtype=udt, tm=tm_p)
        hs, us = ffn(hs, norm_ffn1[l], *ffn1_w, l, g2=norm_mix[l], u_dtype=udt, tm=Rs)
        if even:
            w_in = jnp.pad(cast(ab_w_in[j]), ((0, 0), (0, IN_COLS_PAD - ab_w_in.shape[2])))
            w_out = cast(ab_w_out[j])
            par = _hgrn_params(lb_all[j], hgrn_out_norm[j])
            za, q, kv, qi, slab = inproj(up, w_in, attn_q_norm[j], attn_k_norm[j], tabs_p, tm_p)
            s0 = jnp.zeros((Bp, A_HEADS, A_DK, A_DK), F32)
            oa, sft = hgrn(za, par, s0, LANES)
            k_new, v_new, ki_new = kv[..., :B_DH], kv[..., B_DH:], slab[..., :IDX_DIM]
            wi = slab[..., IDX_DIM:IDX_DIM + IDX_HEADS]
            ob = dsa(q, qi, wi, k_new, v_new, ki_new, chunk_p, kchunk_p, nkeys_p)
            mix_p = (oa, ob, w_out)
            kp_l.append(k_new[:, FRONT_PAD:]); vp_l.append(v_new[:, FRONT_PAD:]); kip_l.append(ki_new[:, FRONT_PAD:])
            sp_l.append(jnp.swapaxes(sft, 2, 3))
            za, q, kv, qi, slab = inproj(us, w_in, attn_q_norm[j], attn_k_norm[j], tabs_s, Rs)
            s0 = jnp.swapaxes(state_hgrn[j].astype(F32), 2, 3)
            oa, sft = hgrn(za.reshape(Bs, Ts, A_COLS), par, s0, Ts)
            kv, slab = kv.reshape(Bs, Ts, -1), slab.reshape(Bs, Ts, -1)
            k_new, v_new, ki_new = kv[..., :B_DH], kv[..., B_DH:], slab[..., :IDX_DIM]
            wi = slab[..., IDX_DIM:IDX_DIM + IDX_HEADS]
            padq = lambda a: jnp.pad(a.reshape(Bs, Ts, -1), ((0, 0), (0, LANES - Ts), (0, 0)))
            cat = lambda past, new: jnp.concatenate([past.astype(F32), new], axis=1)
            ob = dsa(padq(q), padq(qi), padq(wi), cat(cache_k[j], k_new), cat(cache_v[j], v_new),
                     cat(cache_ki[j], ki_new), qchunk_s, kchunk_s, nkeys_s)
            mix_s = (oa.reshape(1, Rs, A_VW), ob[:, :Ts].reshape(1, Rs, B_QW), w_out)
            ks_l.append(k_new); vs_l.append(v_new); kis_l.append(ki_new)
            ss_l.append(jnp.swapaxes(sft, 2, 3).astype(state_hgrn.dtype))
        else:
            pw = cast(pool_w[j])
            hp = pool(hp, up, up, pw, pool_scale[j], tm_p, -FRONT_PAD, True)
            pp_l.append(up[:, Tp - POOL_HIST:])
            us3 = us.reshape(Bs, Ts, D)
            hist = state_pool[j].astype(F32)
            halo = jnp.pad(hist, ((0, 0), (HALO - POOL_HIST, 0), (0, 0)))
            hs = pool(hs.reshape(Bs, Ts, D), us3, halo, pw, pool_scale[j], Ts, P, False).reshape(1, Rs, D)
            ps_l.append(jnp.concatenate([hist, us3], axis=1)[:, -POOL_HIST:])
            mix_p = mix_s = None
        hp = ffn(hp, norm_ffn2[l], *ffn2_w, l, tm=tm_p, mix=mix_p)
        hs = ffn(hs, norm_ffn2[l], *ffn2_w, l, tm=Rs, mix=mix_s)

    y_prompt = hp[:, FRONT_PAD + N_META:]
    return (y_prompt, hs.reshape(Bs, Ts, D),
            jnp.stack(kp_l), jnp.stack(vp_l), jnp.stack(kip_l), jnp.stack(sp_l), jnp.stack(pp_l),
            jnp.stack(ks_l), jnp.stack(vs_l), jnp.stack(kis_l), jnp.stack(ss_l), jnp.stack(ps_l))
```

```python
import functools

import numpy as np
import jax
import jax.numpy as jnp
from jax import lax
from jax.experimental import pallas as pl
from jax.experimental.pallas import tpu as pltpu

F32 = jnp.float32
BF16 = jnp.bfloat16

D_MODEL = 1024
D_FF = 2816
DEPTH = 4
CHUNK = 64
N_META = 16
A_HEADS = 4
A_DK = 128
A_VW = 512
B_HEADS = 4
B_DH = 128
B_QW = 512
IDX_HEADS = 4
IDX_DIM = 64
TOPK = 256
ROPE_THETA = 500000.0
ROT_FRAC = 4
POOL_WINDOWS = (2, 4, 8, 16)
POOL_GROUP = D_MODEL // len(POOL_WINDOWS)
POOL_HIST = max(POOL_WINDOWS) - 1
RMS_EPS = 1e-6
NEG_BIG = -1e30

LANES = 128
SUBLANES = 8
BF16_ROWS = 16
MXU_N = 256
FFN_CHUNKS = 2
CAST_ROWS = 256
KT = 256
QB = 256
NO_CHUNK = 2 ** 30
NO_QUERY = -2
LOG2E = 1.4426950408889634
HALO = 16
FRONT_PAD = 2 * CHUNK - N_META
IN_COLS_PAD = 3200
A_COLS = 2048
VMEM_LIMIT = 56 * 1024 * 1024
HI_MASK = -65536
MIN_NORMAL_BITS = 0x00800000


def _cparams(sem):
    return pltpu.CompilerParams(dimension_semantics=sem, vmem_limit_bytes=VMEM_LIMIT)


def _sigmoid(x):
    return 1.0 / (1.0 + jnp.exp(-x))


def _rms(x, g):
    return x * lax.rsqrt(jnp.mean(x * x, axis=-1, keepdims=True) + RMS_EPS) * g


def _ffn_kernel(*refs, row_tiles, ff_chunks, emit_u, mix):
    refs = list(refs)
    x_ref = refs.pop(0)
    if mix:
        oa_ref, ob_ref, wo_ref = refs.pop(0), refs.pop(0), refs.pop(0)
    g_ref, wg_ref, wu_ref, wd_ref = refs[:4]
    if emit_u:
        g2_ref, y_ref, u_ref = refs[4:]
    else:
        (y_ref,) = refs[4:]
    for r0, r1 in row_tiles:
        x = x_ref[r0:r1, :]
        if mix:
            x = x + jnp.dot(oa_ref[r0:r1, :], wo_ref[0:A_VW, :], preferred_element_type=F32)
            x = x + jnp.dot(ob_ref[r0:r1, :], wo_ref[A_VW:A_VW + B_QW, :], preferred_element_type=F32)
        xn = _rms(x, g_ref[...]).astype(BF16)
        acc = None
        for c0, c1 in ff_chunks:
            gate = jnp.dot(xn, wg_ref[:, c0:c1], preferred_element_type=F32)
            up = jnp.dot(xn, wu_ref[:, c0:c1], preferred_element_type=F32)
            act = (gate * _sigmoid(gate) * up).astype(BF16)
            part = jnp.dot(act, wd_ref[c0:c1, :], preferred_element_type=F32)
            acc = part if acc is None else acc + part
        y = x + 0.5 * acc
        y_ref[r0:r1, :] = y
        if emit_u:
            u_ref[r0:r1, :] = _rms(y, g2_ref[...]).astype(u_ref.dtype)


def _ffn_tiling(tm, F):
    half = (tm // (2 * BF16_ROWS)) * BF16_ROWS
    row_tiles = [(0, half), (half, tm)] if half else [(0, tm)]
    n_col = -(-F // MXU_N)
    per = -(-n_col // FFN_CHUNKS)
    edges = [min(c * per * MXU_N, F) for c in range(FFN_CHUNKS + 1)]
    return row_tiles, [(a, b) for a, b in zip(edges[:-1], edges[1:]) if b > a]


def ffn(x, g, wg, wu, wd, layer, g2=None, u_dtype=BF16, tm=528, mix=None):
    G, R, D = x.shape
    F = wg.shape[2]
    tm = min(tm, R)
    assert R % tm == 0
    row_tiles, ff_chunks = _ffn_tiling(tm, F)
    emit_u = g2 is not None
    row = lambda c: pl.BlockSpec((None, tm, c), lambda b, i: (b, i, 0))
    vec = pl.BlockSpec((1, D), lambda b, i: (0, 0))
    resident = lambda shape: pl.BlockSpec(shape, lambda b, i: (0, 0), pipeline_mode=pl.Buffered(1))
    in_specs, args = [row(D)], [x]
    if mix is not None:
        oa, ob, w_out = mix
        in_specs += [row(A_VW), row(B_QW), resident(w_out.shape)]
        args += [oa, ob, w_out]
    stacked = lambda r, c: pl.BlockSpec((None, r, c), lambda b, i: (layer, 0, 0), pipeline_mode=pl.Buffered(1))
    in_specs += [vec, stacked(D, F), stacked(D, F), stacked(F, D)]
    args += [g.reshape(1, D), wg, wu, wd]
    row = row(D)
    out_shape = [jax.ShapeDtypeStruct((G, R, D), F32)]
    out_specs = [row]
    if emit_u:
        in_specs.append(vec)
        args.append(g2.reshape(1, D))
        out_shape.append(jax.ShapeDtypeStruct((G, R, D), u_dtype))
        out_specs.append(row)
    out = pl.pallas_call(
        functools.partial(_ffn_kernel, row_tiles=row_tiles, ff_chunks=ff_chunks, emit_u=emit_u,
                          mix=mix is not None),
        grid=(G, R // tm),
        in_specs=in_specs,
        out_specs=out_specs,
        out_shape=out_shape,
        compiler_params=_cparams(("parallel", "parallel")),
        name="ffn",
    )(*args)
    return tuple(out) if emit_u else out[0]


def _rope(x, c, s_lo, s_hi, half):
    return x * c + pltpu.roll(x, half, 1) * s_lo + pltpu.roll(x, LANES - half, 1) * s_hi


def _inproj_kernel(u_ref, w_ref, qg_ref, kg_ref, cb_ref, sbl_ref, sbh_ref, ci_ref, sil_ref, sih_ref,
                   za_ref, q_ref, kv_ref, qi_ref, slab_ref):
    u = u_ref[...]
    hb = B_DH // ROT_FRAC // 2
    hi = IDX_DIM // ROT_FRAC // 2
    cb, sbl, sbh = cb_ref[...], sbl_ref[...], sbh_ref[...]
    ci, sil, sih = ci_ref[...], sil_ref[...], sih_ref[...]
    zq = jnp.dot(u, w_ref[:, A_COLS:A_COLS + B_QW], preferred_element_type=F32)
    for h in range(B_HEADS):
        xh = _rms(zq[:, h * B_DH:(h + 1) * B_DH], qg_ref[...])
        q_ref[:, h * B_DH:(h + 1) * B_DH] = _rope(xh, cb, sbl, sbh, hb).astype(q_ref.dtype)
    c0 = A_COLS + B_QW
    zkv = jnp.dot(u, w_ref[:, c0:c0 + 2 * B_DH], preferred_element_type=F32)
    kv_ref[:, 0:B_DH] = _rope(_rms(zkv[:, 0:B_DH], kg_ref[...]), cb, sbl, sbh, hb)
    kv_ref[:, B_DH:2 * B_DH] = zkv[:, B_DH:2 * B_DH]
    c1 = c0 + 2 * B_DH
    zi = jnp.dot(u, w_ref[:, c1:c1 + IDX_HEADS * IDX_DIM + LANES], preferred_element_type=F32)
    for t in range(IDX_HEADS * IDX_DIM // LANES):
        qi_ref[:, t * LANES:(t + 1) * LANES] = _rope(zi[:, t * LANES:(t + 1) * LANES], ci, sil, sih, hi).astype(qi_ref.dtype)
    tail = zi[:, IDX_HEADS * IDX_DIM:]
    lane = lax.broadcasted_iota(jnp.int32, tail.shape, 1)
    slab_ref[...] = jnp.where(lane < IDX_DIM, _rope(tail, ci, sil, sih, hi), tail)
    za_ref[...] = jnp.dot(u, w_ref[:, 0:A_COLS], preferred_element_type=F32)


def inproj(u, w_pad, q_gain, k_gain, tabs, tm):
    G, R, D = u.shape
    assert R % tm == 0
    row = lambda c: pl.BlockSpec((None, tm, c), lambda b, i: (b, i, 0))
    tab = pl.BlockSpec((tm, LANES), lambda b, i: (i, 0))
    vec = pl.BlockSpec((1, LANES), lambda b, i: (0, 0))
    return pl.pallas_call(
        _inproj_kernel,
        grid=(G, R // tm),
        in_specs=[row(D), pl.BlockSpec((D, IN_COLS_PAD), lambda b, i: (0, 0)), vec, vec] + [tab] * 6,
        out_specs=[row(A_COLS), row(B_QW), row(2 * B_DH), row(IDX_HEADS * IDX_DIM), row(LANES)],
        out_shape=[jax.ShapeDtypeStruct((G, R, A_COLS), F32),
                   jax.ShapeDtypeStruct((G, R, B_QW), BF16),
                   jax.ShapeDtypeStruct((G, R, 2 * B_DH), F32),
                   jax.ShapeDtypeStruct((G, R, IDX_HEADS * IDX_DIM), BF16),
                   jax.ShapeDtypeStruct((G, R, LANES), F32)],
        compiler_params=_cparams(("parallel", "parallel")),
        name="inproj",
    )(u, w_pad, q_gain.reshape(1, LANES), k_gain.reshape(1, LANES), *tabs)


def rope_tables(pos):
    pos = pos.astype(F32)
    out = []
    for d in (B_DH, IDX_DIM):
        half = d // ROT_FRAC // 2
        inv = ROPE_THETA ** (-jnp.arange(half, dtype=F32) / half)
        ang = pos[:, None] * inv[None, :]
        cos, sin = jnp.cos(ang), jnp.sin(ang)
        n = pos.shape[0]
        z = lambda w: jnp.zeros((n, w), F32)
        c = jnp.concatenate([cos, cos, jnp.ones((n, d - 2 * half), F32)], axis=1)
        s_lo = jnp.concatenate([z(half), sin, z(d - 2 * half)], axis=1)
        s_hi = jnp.concatenate([-sin, z(d - half)], axis=1)
        rep = LANES // d
        out += [jnp.tile(c, (1, rep)), jnp.tile(s_lo, (1, rep)), jnp.tile(s_hi, (1, rep))]
    return out


def _hgrn_consts(C):
    levels = []
    m = C // 2
    while m >= 1:
        levels.append(m)
        m //= 2
    t = np.arange(C)[:, None]
    r = np.arange(C)[None, :]
    mats = [(r <= t)]
    lvl = np.full((C, C), -1, np.int32)
    lvl[np.arange(C), np.arange(C)] = len(levels)
    for li, m in enumerate(levels):
        anchor = (t // (2 * m)) * (2 * m) + m - 1
        second = t > anchor
        mats.append(np.where(second, (r > anchor) & (r <= t), (r > t) & (r <= anchor)))
        same = (t // (2 * m)) == (r // (2 * m))
        split = same & ((t % (2 * m)) >= m) & ((r % (2 * m)) < m)
        lvl[split] = li
    dmat = np.concatenate([x.astype(np.float32) for x in mats], axis=0)
    return levels, jnp.asarray(dmat, BF16), jnp.asarray(lvl)


def _nt(a, b):
    return lax.dot_general(a, b, (((1,), (1,)), ((), ())), preferred_element_type=F32)


def _tn(a, b):
    return lax.dot_general(a, b, (((0,), (0,)), ((), ())), preferred_element_type=F32)


def _hgrn_kernel(zf_ref, zn_ref, zv_ref, d_ref, lvl_ref, par_ref, s0_ref, oa_ref, sf_ref,
                 st_ref, qa_ref, ka_ref, g_ref, *, C, levels):
    c = pl.program_id(1)
    dk = A_DK
    W = A_HEADS * dk
    log_lb, log_1m, one_m, lb_pos, o_gain = (par_ref[i:i + 1, :] for i in range(5))

    def stash_gates(z_ref, slot):
        aq = z_ref[:, 0:W]
        af = z_ref[:, W:2 * W]
        e = jnp.exp(-jnp.abs(af))
        ls = jnp.minimum(af, 0.0) - jnp.log1p(e)
        b2 = log_1m + ls
        lae = jnp.maximum(log_lb, b2) + jnp.log1p(jnp.exp(-jnp.abs(log_lb - b2)))
        log_f = jnp.where(lb_pos > 0.0, lae, ls)
        ka_ref[slot] = one_m * (jnp.where(af >= 0.0, e, 1.0) / (1.0 + e))
        qa_ref[slot] = aq * _sigmoid(aq)
        g_hi = log_f.astype(BF16)
        r1 = log_f - g_hi.astype(F32)
        g_mid = r1.astype(BF16)
        g_ref[slot, 0] = g_hi
        g_ref[slot, 1] = g_mid
        g_ref[slot, 2] = (r1 - g_mid.astype(F32)).astype(BF16)

    @pl.when(c == 0)
    def _():
        st_ref[...] = s0_ref[...]
        stash_gates(zf_ref, 0)

    cur = c % 2
    qa, ka = qa_ref[cur], ka_ref[cur]
    ai = zv_ref[:, 0:W]
    ag = zv_ref[:, W:2 * W]
    dm = d_ref[...]
    X = (jnp.dot(dm, g_ref[cur, 0], preferred_element_type=F32) + jnp.dot(dm, g_ref[cur, 1], preferred_element_type=F32)
         + jnp.dot(dm, g_ref[cur, 2], preferred_element_type=F32))
    lvl = lvl_ref[...]
    rows = lax.broadcasted_iota(jnp.int32, (C, 1), 0)
    nl = len(levels)
    b = X[0:C, :]
    b_last = b[C - 1:C, :]
    qe = (qa * jnp.exp(b)).astype(BF16)
    ke = (ka * jnp.exp(b_last - b)).astype(BF16)
    decay = jnp.exp(b_last)
    qb, kb, vb = qa.astype(BF16), ka.astype(BF16), ai.astype(BF16)
    zs = [(jnp.where((rows // m) % 2 == 1, qa, ka) * jnp.exp(X[(1 + li) * C:(2 + li) * C, :])).astype(BF16)
          for li, m in enumerate(levels)]
    gate = o_gain * _sigmoid(ag)
    for h in range(A_HEADS):
        sl = slice(h * dk, (h + 1) * dk)
        st = st_ref[h]
        o = _nt(qe[:, sl], st.astype(BF16))
        att = jnp.where(lvl == nl, _nt(qb[:, sl], kb[:, sl]), 0.0)
        for li in range(nl):
            att = att + jnp.where(lvl == li, _nt(zs[li][:, sl], zs[li][:, sl]), 0.0)
        o = o + jnp.dot(att.astype(BF16), vb[:, sl], preferred_element_type=F32)
        st_ref[h] = st * decay[:, sl] + _tn(vb[:, sl], ke[:, sl])
        on = o * lax.rsqrt(jnp.mean(o * o, axis=-1, keepdims=True) + RMS_EPS)
        oa_ref[:, sl] = (on * gate[:, sl]).astype(oa_ref.dtype)

    stash_gates(zn_ref, 1 - cur)

    @pl.when(c == pl.num_programs(1) - 1)
    def _():
        sf_ref[...] = st_ref[...]


def hgrn(za, par, s0t, C):
    B, T, _ = za.shape
    assert T % C == 0
    levels, dmat, lvl = _hgrn_consts(C)
    W = A_HEADS * A_DK
    st_spec = pl.BlockSpec((None, A_HEADS, A_DK, A_DK), lambda b, c: (b, 0, 0, 0))
    nc = T // C
    half = lambda index: pl.BlockSpec((None, C, 2 * W), index)
    return pl.pallas_call(
        functools.partial(_hgrn_kernel, C=C, levels=levels),
        grid=(B, nc),
        in_specs=[half(lambda b, c: (b, 0, 0)),
                  half(lambda b, c: (b, jnp.minimum(c + 1, nc - 1), 0)),
                  half(lambda b, c: (b, c, 1)),
                  pl.BlockSpec(dmat.shape, lambda b, c: (0, 0)),
                  pl.BlockSpec((C, C), lambda b, c: (0, 0)),
                  pl.BlockSpec((8, W), lambda b, c: (0, 0)),
                  st_spec],
        out_specs=[pl.BlockSpec((None, C, W), lambda b, c: (b, c, 0)), st_spec],
        out_shape=[jax.ShapeDtypeStruct((B, T, W), BF16),
                   jax.ShapeDtypeStruct((B, A_HEADS, A_DK, A_DK), F32)],
        scratch_shapes=[pltpu.VMEM((A_HEADS, A_DK, A_DK), F32),
                        pltpu.VMEM((2, C, W), F32), pltpu.VMEM((2, C, W), F32), pltpu.VMEM((2, 3, C, W), BF16)],
        compiler_params=_cparams(("parallel", "arbitrary")),
        name="hgrn",
    )(za, za, za, dmat, lvl, par, s0t)


def _tree_sum(parts):
    while len(parts) > 1:
        parts = [parts[r] + parts[r + 1] if r + 1 < len(parts) else parts[r] for r in range(0, len(parts), 2)]
    return parts[0]


def _fori_pairs(n, tile, init):
    acc = lax.fori_loop(0, n // 2, lambda p, a: a + (tile(2 * p) + tile(2 * p + 1)), init)
    return lax.fori_loop(n // 2 * 2, n, lambda j, a: a + tile(j), acc)


def _dsa_kernel(nkb_ref,qt_ref, qit_ref, wit_ref, qc_ref, kc_ref, k_ref, vt_ref, ki_ref, tri_ref,
                ob_ref, key_ref, hi_ref, acc_ref, *, topk):
    i = pl.program_id(1)
    nk = nkb_ref[i]
    qc = qc_ref[...]
    qit = qit_ref[...]
    rhs_i = jnp.concatenate([qit[h * IDX_DIM:(h + 1) * IDX_DIM, :] for h in range(IDX_HEADS)], axis=1)
    qt = qt_ref[...]
    rhs_q = jnp.concatenate([qt[h * B_DH:(h + 1) * B_DH, :] for h in range(B_HEADS)], axis=1)
    wsc = wit_ref[...] * (IDX_HEADS ** -0.5 * IDX_DIM ** -0.5)

    def score_tile(j):
        sc = jnp.dot(ki_ref[j], rhs_i, preferred_element_type=F32)
        s = jnp.maximum(sc[:, 0:QB], 0.0) * wsc[0:1, :]
        for h in range(1, IDX_HEADS):
            s = s + jnp.maximum(sc[:, h * QB:(h + 1) * QB], 0.0) * wsc[h:h + 1, :]
        s = jnp.where(kc_ref[j] <= qc, s, NEG_BIG)
        bits = lax.bitcast_convert_type(s, jnp.int32)
        bits = jnp.where((bits & 0x7FFFFFFF) < MIN_NORMAL_BITS, 0, bits)
        key_ref[j] = bits ^ ((bits >> 31) & 0x7FFFFFFF)
        hi_ref[j] = lax.bitcast_convert_type(bits & HI_MASK, F32).astype(BF16)

    @pl.loop(0, nk // 2)
    def _(p):
        score_tile(2 * p)
        score_tile(2 * p + 1)

    @pl.loop(nk // 2 * 2, nk)
    def _(j):
        score_tile(j)

    def count_hi(cand16):
        cand16 = jnp.where((cand16 >= 1) & (cand16 < MIN_NORMAL_BITS >> 16), MIN_NORMAL_BITS >> 16, cand16)
        pat = cand16 ^ ((cand16 >> 15) & 0x7FFF)
        cand = lax.bitcast_convert_type(jnp.left_shift(pat, 16), F32).astype(BF16)

        def tile(j):
            hit = jnp.where(hi_ref[j] >= cand, jnp.bfloat16(1), jnp.bfloat16(0))
            parts = [hit[r * BF16_ROWS:(r + 1) * BF16_ROWS, :] for r in range(KT // BF16_ROWS)]
            return _tree_sum(parts).astype(F32)
        acc = _fori_pairs(nk, tile, jnp.zeros((BF16_ROWS, QB), F32))
        return jnp.sum(acc, axis=0, keepdims=True).astype(jnp.int32)

    def count(pred):
        def tile(j):
            hit = jnp.where(pred(key_ref[j]), 1, 0)
            return _tree_sum([hit[r * SUBLANES:(r + 1) * SUBLANES, :] for r in range(KT // SUBLANES)])
        acc = _fori_pairs(nk, tile, jnp.zeros((SUBLANES, QB), jnp.int32))
        return jnp.sum(acc, axis=0, keepdims=True)

    c0 = count_hi(jnp.zeros((1, QB), jnp.int32))
    hi0 = jnp.where(c0 >= topk, 0, -(2 ** 15)).astype(jnp.int32)
    above0 = jnp.where(c0 >= topk, 0, c0)

    def hi_body(it, carry):
        thr, above = carry
        cand = thr | jnp.left_shift(jnp.int32(1), 14 - it)
        cnt = count_hi(cand)
        ok = cnt >= topk
        return jnp.where(ok, cand, thr), jnp.where(ok, above, cnt)

    hi16, above1 = lax.fori_loop(0, 15, hi_body, (hi0, above0))

    def lo_body(it, carry):
        thr, above = carry
        cand = thr | jnp.left_shift(jnp.int32(1), 15 - it)
        cnt = count(lambda kv: kv >= cand)
        ok = cnt >= topk
        return jnp.where(ok, cand, thr), jnp.where(ok, above, cnt)

    thr, above = lax.fori_loop(0, 16, lo_body, (jnp.left_shift(hi16, 16), above1))
    need = (topk - above).astype(F32)

    acc_ref[...] = jnp.zeros_like(acc_ref)
    c2 = B_DH ** -0.5 * LOG2E
    tri = tri_ref[...]

    def att_step(tiles, carry):
        ms, ls, tie_seen = carry
        sels, s_alls = [], []
        for j in tiles:
            kv = key_ref[j]
            eq = kv == thr
            eqf = jnp.where(eq, 1.0, 0.0)
            before = jnp.dot(tri, eqf.astype(BF16), preferred_element_type=F32) + tie_seen
            take = jnp.where(kv > thr, 1, jnp.where(eq, jnp.where(before < need, 1, 0), 0))
            sels.append(jnp.where(kc_ref[j] <= qc, take, 0) > 0)
            tie_seen = tie_seen + jnp.sum(eqf, axis=0, keepdims=True)
            s_alls.append(jnp.dot(k_ref[j], rhs_q, preferred_element_type=F32))
        new_ms, new_ls, alphas = [], [], []
        ps = [[] for _ in tiles]
        for h in range(B_HEADS):
            ss = [jnp.where(sel, s_all[:, h * QB:(h + 1) * QB], NEG_BIG) for sel, s_all in zip(sels, s_alls)]
            m_new = ms[h]
            for s in ss:
                m_new = jnp.maximum(m_new, jnp.max(s, axis=0, keepdims=True))
            alpha = jnp.exp2((ms[h] - m_new) * c2)
            l_new = alpha * ls[h]
            for t, (sel, s) in enumerate(zip(sels, ss)):
                p = jnp.where(sel, jnp.exp2((s - m_new) * c2), 0.0)
                l_new = l_new + jnp.sum(p, axis=0, keepdims=True)
                ps[t].append(p.astype(BF16))
            new_ls.append(l_new)
            new_ms.append(m_new)
            alphas.append(alpha)
        pv = None
        for t, j in enumerate(tiles):
            part = jnp.dot(vt_ref[j], jnp.concatenate(ps[t], axis=1), preferred_element_type=F32)
            pv = part if pv is None else pv + part
        for h in range(B_HEADS):
            acc_ref[h] = alphas[h] * acc_ref[h] + pv[:, h * QB:(h + 1) * QB]
        return tuple(new_ms), tuple(new_ls), tie_seen

    init_m = tuple(jnp.full((1, QB), NEG_BIG, F32) for _ in range(B_HEADS))
    init_l = tuple(jnp.zeros((1, QB), F32) for _ in range(B_HEADS))
    carry = (init_m, init_l, jnp.zeros((1, QB), F32))
    carry = lax.fori_loop(0, nk // 2, lambda p, c: att_step([2 * p, 2 * p + 1], c), carry)
    _, ls, _ = lax.fori_loop(nk // 2 * 2, nk, lambda j, c: att_step([j], c), carry)
    for h in range(B_HEADS):
        inv = jnp.where(ls[h] > 0.0, 1.0 / jnp.where(ls[h] > 0.0, ls[h], 1.0), 0.0)
        ob_ref[:, h * B_DH:(h + 1) * B_DH] = (acc_ref[h] * inv).T.astype(ob_ref.dtype)


def dsa(q, qi, wi, k_all, v_all, ki_all, q_chunk, k_chunk, nkeys, topk=TOPK):
    B, Tq, _ = q.shape
    Tk = k_all.shape[1]
    assert Tq % LANES == 0 and QB % LANES == 0
    nq, nkt = -(-Tq // QB), -(-Tk // KT)
    padq = lambda a: jnp.pad(a, ((0, 0), (0, nq * QB - Tq), (0, 0)))
    padk = lambda a: jnp.pad(a, ((0, 0), (0, nkt * KT - Tk), (0, 0)))
    qt = jnp.swapaxes(padq(q), 1, 2)
    qit = jnp.swapaxes(padq(qi), 1, 2)
    wit = jnp.swapaxes(padq(wi), 1, 2)
    k4 = padk(k_all.astype(BF16)).reshape(B, nkt, KT, B_DH)
    vt4 = jnp.swapaxes(padk(v_all.astype(BF16)).reshape(B, nkt, KT, B_DH), 2, 3)
    ki4 = padk(ki_all.astype(BF16)).reshape(B, nkt, KT, IDX_DIM)
    kc = jnp.pad(k_chunk.astype(jnp.int32), (0, nkt * KT - Tk), constant_values=NO_CHUNK)
    kc = jnp.broadcast_to(kc.reshape(nkt, KT, 1), (nkt, KT, QB))
    qc = jnp.pad(q_chunk.astype(jnp.int32), (0, nq * QB - Tq), constant_values=NO_QUERY).reshape(nq, 1, QB)
    per = QB // LANES
    nkeys = jnp.pad(nkeys.astype(jnp.int32), (0, nq * per - nkeys.shape[0])).reshape(nq, per).max(axis=1)
    nkb = (nkeys + (KT - 1)) // KT
    tri = jnp.asarray(np.tril(np.ones((KT, KT), np.float32), -1), BF16)
    grid_spec = pltpu.PrefetchScalarGridSpec(
        num_scalar_prefetch=1,
        grid=(B, nq),
        in_specs=[pl.BlockSpec((None, B_QW, QB), lambda b, i, n: (b, 0, i)),
                  pl.BlockSpec((None, IDX_HEADS * IDX_DIM, QB), lambda b, i, n: (b, 0, i)),
                  pl.BlockSpec((None, IDX_HEADS, QB), lambda b, i, n: (b, 0, i)),
                  pl.BlockSpec((None, 1, QB), lambda b, i, n: (i, 0, 0)),
                  pl.BlockSpec((nkt, KT, QB), lambda b, i, n: (0, 0, 0)),
                  pl.BlockSpec((None, nkt, KT, B_DH), lambda b, i, n: (b, 0, 0, 0)),
                  pl.BlockSpec((None, nkt, B_DH, KT), lambda b, i, n: (b, 0, 0, 0)),
                  pl.BlockSpec((None, nkt, KT, IDX_DIM), lambda b, i, n: (b, 0, 0, 0)),
                  pl.BlockSpec((KT, KT), lambda b, i, n: (0, 0))],
        out_specs=pl.BlockSpec((None, QB, B_QW), lambda b, i, n: (b, i, 0)),
        scratch_shapes=[pltpu.VMEM((nkt, KT, QB), jnp.int32),
                        pltpu.VMEM((nkt, KT, QB), BF16),
                        pltpu.VMEM((B_HEADS, B_DH, QB), F32)],
    )
    ob = pl.pallas_call(
        functools.partial(_dsa_kernel, topk=topk),
        grid_spec=grid_spec,
        out_shape=jax.ShapeDtypeStruct((B, nq * QB, B_QW), BF16),
        compiler_params=_cparams(("parallel", "arbitrary")),
        name="dsa",
    )(nkb, qt, qit, wit, qc, kc, k4, vt4, ki4, tri)
    return ob[:, :Tq]


def _cast_kernel(x_ref, o_ref):
    o_ref[...] = x_ref[...].astype(o_ref.dtype)


def cast_bf16(w):
    L, R, C = w.shape
    assert R % CAST_ROWS == 0
    spec = pl.BlockSpec((None, CAST_ROWS, C), lambda l, i: (l, i, 0))
    return pl.pallas_call(
        _cast_kernel,
        grid=(L, R // CAST_ROWS),
        in_specs=[spec],
        out_specs=spec,
        out_shape=jax.ShapeDtypeStruct(w.shape, BF16),
        compiler_params=_cparams(("parallel", "parallel")),
        name="cast",
    )(w)


def _pool_kernel(h_ref, u_ref, halo_ref, w_ref, sc_ref, y_ref, *, tm, pos0, zero_first_halo):
    i = pl.program_id(1)
    halo = halo_ref[...]
    if zero_first_halo:
        halo = jnp.where(i > 0, halo, 0.0)
    u = u_ref[...]
    ue = jnp.concatenate([halo, u], axis=0)
    pos = pos0 + i * tm + lax.broadcasted_iota(jnp.int32, (tm, 1), 0)
    run = ue
    outs = []
    for g, w in enumerate(POOL_WINDOWS):
        lo = g * POOL_GROUP
        if g:
            run = run[:, POOL_GROUP:]
        run = run + pltpu.roll(run, w // 2, 0)
        cnt = jnp.clip(pos + 1, 1, w).astype(F32)
        mean = run[HALO:, 0:POOL_GROUP] / cnt
        d = (mean - u[:, lo:lo + POOL_GROUP]).astype(BF16)
        outs.append(jnp.dot(d, w_ref[g], preferred_element_type=F32))
    y = jnp.concatenate(outs, axis=1) * sc_ref[...]
    y_ref[...] = h_ref[...] + y


def pool(h, u, halo_src, w_grp, scale, tm, pos0, zero_first_halo):
    G, R, D = h.shape
    assert R % tm == 0 and tm % HALO == 0
    row = pl.BlockSpec((None, tm, D), lambda b, i: (b, i, 0))
    step = tm // HALO
    halo = pl.BlockSpec((None, HALO, D), lambda b, i: (b, jnp.maximum(i * step - 1, 0), 0))
    return pl.pallas_call(
        functools.partial(_pool_kernel, tm=tm, pos0=pos0, zero_first_halo=zero_first_halo),
        grid=(G, R // tm),
        in_specs=[row, row, halo,
                  pl.BlockSpec((len(POOL_WINDOWS), POOL_GROUP, POOL_GROUP), lambda b, i: (0, 0, 0)),
                  pl.BlockSpec((1, D), lambda b, i: (0, 0))],
        out_specs=row,
        out_shape=jax.ShapeDtypeStruct((G, R, D), F32),
        compiler_params=_cparams(("parallel", "parallel")),
        name="pool",
    )(h, u, halo_src, w_grp, scale.reshape(1, D))


def _hgrn_params(lb, o_gain):
    lb_pos = lb > 0
    lb_safe = jnp.where(lb_pos, lb, 0.5)
    rows = [jnp.log(lb_safe), jnp.log1p(-lb_safe), 1.0 - lb, lb_pos.astype(F32), o_gain.astype(F32)]
    rows += [jnp.zeros_like(lb)] * 3
    return jnp.stack(rows).astype(F32)


def kernel(x_prompt, x_sample, cache_k, cache_v, cache_ki, state_hgrn, state_pool, meta_tokens, norm_ffn1, ffn1_wg, ffn1_wu, ffn1_wd, norm_mix, norm_ffn2, ffn2_wg, ffn2_wu, ffn2_wd, ab_w_in, ab_w_out, hgrn_lb_logits, hgrn_out_norm, attn_q_norm, attn_k_norm, pool_w, pool_scale):
    Bp, Sp, D = x_prompt.shape
    Bs, Ts, _ = x_sample.shape
    P = cache_k.shape[2]
    Tp = FRONT_PAD + N_META + Sp
    assert Tp % LANES == 0 and Ts == CHUNK and P % LANES == 0
    tm_p = Tp // 8
    Rs = Bs * Ts

    hp = jnp.concatenate([jnp.zeros((Bp, FRONT_PAD, D), F32),
                          jnp.broadcast_to(meta_tokens.astype(F32)[None], (Bp, N_META, D)), x_prompt], axis=1)
    hs = x_sample.reshape(1, Rs, D)

    row_p = jnp.arange(Tp, dtype=jnp.int32)
    pos_p = row_p - FRONT_PAD
    chunk_p = jnp.where(pos_p < 0, -2, jnp.where(pos_p < N_META, -1, (pos_p - N_META) // CHUNK))
    kchunk_p = jnp.where(pos_p < 0, jnp.int32(NO_CHUNK), chunk_p)
    nkeys_p = (jnp.arange(Tp // LANES, dtype=jnp.int32) + 1) * LANES
    tabs_p = rope_tables(pos_p)

    pos_s = P + jnp.arange(Ts, dtype=jnp.int32)
    qchunk_s = jnp.concatenate([pos_s // CHUNK, jnp.full((LANES - Ts,), -2, jnp.int32)])
    kchunk_s = jnp.arange(P + Ts, dtype=jnp.int32) // CHUNK
    nkeys_s = jnp.full((1,), P + Ts, jnp.int32)
    tabs_s = [jnp.tile(t, (Bs, 1)) for t in rope_tables(pos_s)]

    lb_soft = jax.nn.softmax(hgrn_lb_logits.astype(F32), axis=0)
    lb_all = jnp.cumsum(lb_soft, axis=0) - lb_soft[0]

    cast = lambda w: w.astype(BF16)
    ffn1_w = [cast_bf16(w) for w in (ffn1_wg, ffn1_wu, ffn1_wd)]
    ffn2_w = [cast_bf16(w) for w in (ffn2_wg, ffn2_wu, ffn2_wd)]
    kp_l, vp_l, kip_l, sp_l, pp_l = [], [], [], [], []
    ks_l, vs_l, kis_l, ss_l, ps_l = [], [], [], [], []
    for l in range(DEPTH):
        j = l // 2
        even = l % 2 == 0
        udt = BF16 if even else F32
        hp, up = ffn(hp, norm_ffn1[l], *ffn1_w, l, g2=norm_mix[l], u_dtype=udt, tm=tm_p)
        hs, us = ffn(hs, norm_ffn1[l], *ffn1_w, l, g2=norm_mix[l], u_dtype=udt, tm=Rs)
        if even:
            w_in = jnp.pad(cast(ab_w_in[j]), ((0, 0), (0, IN_COLS_PAD - ab_w_in.shape[2])))
            w_out = cast(ab_w_out[j])
            par = _hgrn_params(lb_all[j], hgrn_out_norm[j])
            za, q, kv, qi, slab = inproj(up, w_in, attn_q_norm[j], attn_k_norm[j], tabs_p, tm_p)
            s0 = jnp.zeros((Bp, A_HEADS, A_DK, A_DK), F32)
            oa, sft = hgrn(za, par, s0, LANES)
            k_new, v_new, ki_new = kv[..., :B_DH], kv[..., B_DH:], slab[..., :IDX_DIM]
            wi = slab[..., IDX_DIM:IDX_DIM + IDX_HEADS]
            ob = dsa(q, qi, wi, k_new, v_new, ki_new, chunk_p, kchunk_p, nkeys_p)
            mix_p = (oa, ob, w_out)
            kp_l.append(k_new[:, FRONT_PAD:]); vp_l.append(v_new[:, FRONT_PAD:]); kip_l.append(ki_new[:, FRONT_PAD:])
            sp_l.append(jnp.swapaxes(sft, 2, 3))
            za, q, kv, qi, slab = inproj(us, w_in, attn_q_norm[j], attn_k_norm[j], tabs_s, Rs)
            s0 = jnp.swapaxes(state_hgrn[j].astype(F32), 2, 3)
            oa, sft = hgrn(za.reshape(Bs, Ts, A_COLS), par, s0, Ts)
            kv, slab = kv.reshape(Bs, Ts, -1), slab.reshape(Bs, Ts, -1)
            k_new, v_new, ki_new = kv[..., :B_DH], kv[..., B_DH:], slab[..., :IDX_DIM]
            wi = slab[..., IDX_DIM:IDX_DIM + IDX_HEADS]
            padq = lambda a: jnp.pad(a.reshape(Bs, Ts, -1), ((0, 0), (0, LANES - Ts), (0, 0)))
            cat = lambda past, new: jnp.concatenate([past.astype(F32), new], axis=1)
            ob = dsa(padq(q), padq(qi), padq(wi), cat(cache_k[j], k_new), cat(cache_v[j], v_new),
                     cat(cache_ki[j], ki_new), qchunk_s, kchunk_s, nkeys_s)
            mix_s = (oa.reshape(1, Rs, A_VW), ob[:, :Ts].reshape(1, Rs, B_QW), w_out)
            ks_l.append(k_new); vs_l.append(v_new); kis_l.append(ki_new)
            ss_l.append(jnp.swapaxes(sft, 2, 3).astype(state_hgrn.dtype))
        else:
            pw = cast(pool_w[j])
            hp = pool(hp, up, up, pw, pool_scale[j], tm_p, -FRONT_PAD, True)
            pp_l.append(up[:, Tp - POOL_HIST:])
            us3 = us.reshape(Bs, Ts, D)
            hist = state_pool[j].astype(F32)
            halo = jnp.pad(hist, ((0, 0), (HALO - POOL_HIST, 0), (0, 0)))
            hs = pool(hs.reshape(Bs, Ts, D), us3, halo, pw, pool_scale[j], Ts, P, False).reshape(1, Rs, D)
            ps_l.append(jnp.concatenate([hist, us3], axis=1)[:, -POOL_HIST:])
            mix_p = mix_s = None
        hp = ffn(hp, norm_ffn2[l], *ffn2_w, l, tm=tm_p, mix=mix_p)
        hs = ffn(hs, norm_ffn2[l], *ffn2_w, l, tm=Rs, mix=mix_s)

    y_prompt = hp[:, FRONT_PAD + N_META:]
    return (y_prompt, hs.reshape(Bs, Ts, D),
            jnp.stack(kp_l), jnp.stack(vp_l), jnp.stack(kip_l), jnp.stack(sp_l), jnp.stack(pp_l),
            jnp.stack(ks_l), jnp.stack(vs_l), jnp.stack(kis_l), jnp.stack(ss_l), jnp.stack(ps_l))
```

```python
import functools

import numpy as np
import jax
import jax.numpy as jnp
from jax import lax
from jax.experimental import pallas as pl
from jax.experimental.pallas import tpu as pltpu

F32 = jnp.float32
BF16 = jnp.bfloat16

D_MODEL = 1024
D_FF = 2816
DEPTH = 4
CHUNK = 64
N_META = 16
A_HEADS = 4
A_DK = 128
A_VW = 512
B_HEADS = 4
B_DH = 128
B_QW = 512
IDX_HEADS = 4
IDX_DIM = 64
TOPK = 256
ROPE_THETA = 500000.0
ROT_FRAC = 4
POOL_WINDOWS = (2, 4, 8, 16)
POOL_GROUP = D_MODEL // len(POOL_WINDOWS)
POOL_HIST = max(POOL_WINDOWS) - 1
RMS_EPS = 1e-6
NEG_BIG = -1e30

LANES = 128
SUBLANES = 8
BF16_ROWS = 16
MXU_N = 256
FFN_CHUNKS = 2
CAST_ROWS = 256
KT = 256
QB = 256
NO_CHUNK = 2 ** 30
NO_QUERY = -2
NEVER = 3e38
LOG2E = 1.4426950408889634
HALO = 16
FRONT_PAD = 2 * CHUNK - N_META
IN_COLS_PAD = 3200
A_COLS = 2048
VMEM_LIMIT = 56 * 1024 * 1024
HI_MASK = -65536
MIN_NORMAL_BITS = 0x00800000


def _cparams(sem):
    return pltpu.CompilerParams(dimension_semantics=sem, vmem_limit_bytes=VMEM_LIMIT)


def _sigmoid(x):
    return 1.0 / (1.0 + jnp.exp(-x))


def _rms(x, g):
    return x * lax.rsqrt(jnp.mean(x * x, axis=-1, keepdims=True) + RMS_EPS) * g


def _ffn_kernel(*refs, row_tiles, ff_chunks, emit_u, mix, pool_pos0):
    refs = list(refs)
    x_ref = refs.pop(0)
    if mix:
        oa_ref, ob_ref, wo_ref = refs.pop(0), refs.pop(0), refs.pop(0)
    if pool_pos0 is not None:
        halo_ref, gm_ref, pw_ref, psc_ref = (refs.pop(0) for _ in range(4))
        xs_ref = refs.pop()
        hist_ref = refs.pop()
    g_ref, wg_ref, wu_ref, wd_ref = refs[:4]
    if emit_u:
        g2_ref, y_ref, u_ref = refs[4:]
    else:
        (y_ref,) = refs[4:]
    if pool_pos0 is not None:
        i = pl.program_id(1)
        tm = x_ref.shape[0]
        h = x_ref[...]
        u = _rms(h, gm_ref[...])
        halo = _rms(jnp.where(i > 0, halo_ref[...], 0.0), gm_ref[...])
        pos = pool_pos0 + i * tm + lax.broadcasted_iota(jnp.int32, (tm, 1), 0)
        xs_ref[...] = h + _pool_mix(u, halo, pw_ref, psc_ref[...], pos)
        hist_ref[...] = u[tm - HALO:tm, :]
        x_ref = xs_ref
    for r0, r1 in row_tiles:
        x = x_ref[r0:r1, :]
        if mix:
            x = x + jnp.dot(oa_ref[r0:r1, :], wo_ref[0:A_VW, :], preferred_element_type=F32)
            x = x + jnp.dot(ob_ref[r0:r1, :], wo_ref[A_VW:A_VW + B_QW, :], preferred_element_type=F32)
        xn = _rms(x, g_ref[...]).astype(BF16)
        acc = None
        for c0, c1 in ff_chunks:
            gate = jnp.dot(xn, wg_ref[:, c0:c1], preferred_element_type=F32)
            up = jnp.dot(xn, wu_ref[:, c0:c1], preferred_element_type=F32)
            act = (gate * _sigmoid(gate) * up).astype(BF16)
            part = jnp.dot(act, wd_ref[c0:c1, :], preferred_element_type=F32)
            acc = part if acc is None else acc + part
        y = x + 0.5 * acc
        y_ref[r0:r1, :] = y
        if emit_u:
            u_ref[r0:r1, :] = _rms(y, g2_ref[...]).astype(u_ref.dtype)


def _ffn_tiling(tm, F):
    half = (tm // (2 * BF16_ROWS)) * BF16_ROWS
    row_tiles = [(0, half), (half, tm)] if half else [(0, tm)]
    n_col = -(-F // MXU_N)
    per = -(-n_col // FFN_CHUNKS)
    edges = [min(c * per * MXU_N, F) for c in range(FFN_CHUNKS + 1)]
    return row_tiles, [(a, b) for a, b in zip(edges[:-1], edges[1:]) if b > a]


def ffn(x, g, wg, wu, wd, layer, g2=None, u_dtype=BF16, tm=528, mix=None, pool=None):
    G, R, D = x.shape
    F = wg.shape[2]
    tm = min(tm, R)
    assert R % tm == 0
    row_tiles, ff_chunks = _ffn_tiling(tm, F)
    emit_u = g2 is not None
    row = lambda c: pl.BlockSpec((None, tm, c), lambda b, i: (b, i, 0))
    vec = pl.BlockSpec((1, D), lambda b, i: (0, 0))
    resident = lambda shape: pl.BlockSpec(shape, lambda b, i: (0,) * len(shape), pipeline_mode=pl.Buffered(1))
    in_specs, args = [row(D)], [x]
    if mix is not None:
        oa, ob, w_out = mix
        in_specs += [row(A_VW), row(B_QW), resident(w_out.shape)]
        args += [oa, ob, w_out]
    scratch = []
    if pool is not None:
        g_mix, w_grp, scale, pos0 = pool
        assert tm % HALO == 0
        step = tm // HALO
        in_specs += [pl.BlockSpec((None, HALO, D), lambda b, i: (b, jnp.maximum(i * step - 1, 0), 0)),
                     vec, resident(w_grp.shape), vec]
        args += [x, g_mix.reshape(1, D), w_grp, scale.reshape(1, D)]
        scratch = [pltpu.VMEM((tm, D), F32)]
    stacked = lambda r, c: pl.BlockSpec((None, r, c), lambda b, i: (layer, 0, 0), pipeline_mode=pl.Buffered(1))
    in_specs += [vec, stacked(D, F), stacked(D, F), stacked(F, D)]
    args += [g.reshape(1, D), wg, wu, wd]
    row = row(D)
    out_shape = [jax.ShapeDtypeStruct((G, R, D), F32)]
    out_specs = [row]
    if emit_u:
        in_specs.append(vec)
        args.append(g2.reshape(1, D))
        out_shape.append(jax.ShapeDtypeStruct((G, R, D), u_dtype))
        out_specs.append(row)
    if pool is not None:
        out_shape.append(jax.ShapeDtypeStruct((G, HALO, D), F32))
        out_specs.append(pl.BlockSpec((None, HALO, D), lambda b, i: (b, 0, 0)))
    out = pl.pallas_call(
        functools.partial(_ffn_kernel, row_tiles=row_tiles, ff_chunks=ff_chunks, emit_u=emit_u,
                          mix=mix is not None, pool_pos0=None if pool is None else pool[3]),
        grid=(G, R // tm),
        in_specs=in_specs,
        out_specs=out_specs,
        out_shape=out_shape,
        scratch_shapes=scratch,
        compiler_params=_cparams(("parallel", "arbitrary" if pool is not None else "parallel")),
        name="ffn",
    )(*args)
    return tuple(out) if len(out) > 1 else out[0]


def _rope(x, c, s_lo, s_hi, half):
    return x * c + pltpu.roll(x, half, 1) * s_lo + pltpu.roll(x, LANES - half, 1) * s_hi


def _inproj_kernel(u_ref, w_ref, qg_ref, kg_ref, cb_ref, sbl_ref, sbh_ref, ci_ref, sil_ref, sih_ref,
                   za_ref, q_ref, kv_ref, qi_ref, slab_ref):
    u = u_ref[...]
    hb = B_DH // ROT_FRAC // 2
    hi = IDX_DIM // ROT_FRAC // 2
    cb, sbl, sbh = cb_ref[...], sbl_ref[...], sbh_ref[...]
    ci, sil, sih = ci_ref[...], sil_ref[...], sih_ref[...]
    zq = jnp.dot(u, w_ref[:, A_COLS:A_COLS + B_QW], preferred_element_type=F32)
    for h in range(B_HEADS):
        xh = _rms(zq[:, h * B_DH:(h + 1) * B_DH], qg_ref[...])
        q_ref[:, h * B_DH:(h + 1) * B_DH] = _rope(xh, cb, sbl, sbh, hb).astype(q_ref.dtype)
    c0 = A_COLS + B_QW
    zkv = jnp.dot(u, w_ref[:, c0:c0 + 2 * B_DH], preferred_element_type=F32)
    kv_ref[:, 0:B_DH] = _rope(_rms(zkv[:, 0:B_DH], kg_ref[...]), cb, sbl, sbh, hb)
    kv_ref[:, B_DH:2 * B_DH] = zkv[:, B_DH:2 * B_DH]
    c1 = c0 + 2 * B_DH
    zi = jnp.dot(u, w_ref[:, c1:c1 + IDX_HEADS * IDX_DIM + LANES], preferred_element_type=F32)
    for t in range(IDX_HEADS * IDX_DIM // LANES):
        qi_ref[:, t * LANES:(t + 1) * LANES] = _rope(zi[:, t * LANES:(t + 1) * LANES], ci, sil, sih, hi).astype(qi_ref.dtype)
    tail = zi[:, IDX_HEADS * IDX_DIM:]
    lane = lax.broadcasted_iota(jnp.int32, tail.shape, 1)
    slab_ref[...] = jnp.where(lane < IDX_DIM, _rope(tail, ci, sil, sih, hi), tail)
    za_ref[...] = jnp.dot(u, w_ref[:, 0:A_COLS], preferred_element_type=F32)


def inproj(u, w_pad, q_gain, k_gain, tabs, tm):
    G, R, D = u.shape
    assert R % tm == 0
    row = lambda c: pl.BlockSpec((None, tm, c), lambda b, i: (b, i, 0))
    tab = pl.BlockSpec((tm, LANES), lambda b, i: (i, 0))
    vec = pl.BlockSpec((1, LANES), lambda b, i: (0, 0))
    return pl.pallas_call(
        _inproj_kernel,
        grid=(G, R // tm),
        in_specs=[row(D), pl.BlockSpec((D, IN_COLS_PAD), lambda b, i: (0, 0)), vec, vec] + [tab] * 6,
        out_specs=[row(A_COLS), row(B_QW), row(2 * B_DH), row(IDX_HEADS * IDX_DIM), row(LANES)],
        out_shape=[jax.ShapeDtypeStruct((G, R, A_COLS), F32),
                   jax.ShapeDtypeStruct((G, R, B_QW), BF16),
                   jax.ShapeDtypeStruct((G, R, 2 * B_DH), F32),
                   jax.ShapeDtypeStruct((G, R, IDX_HEADS * IDX_DIM), BF16),
                   jax.ShapeDtypeStruct((G, R, LANES), F32)],
        compiler_params=_cparams(("parallel", "parallel")),
        name="inproj",
    )(u, w_pad, q_gain.reshape(1, LANES), k_gain.reshape(1, LANES), *tabs)


def rope_tables(pos):
    pos = pos.astype(F32)
    out = []
    for d in (B_DH, IDX_DIM):
        half = d // ROT_FRAC // 2
        inv = ROPE_THETA ** (-jnp.arange(half, dtype=F32) / half)
        ang = pos[:, None] * inv[None, :]
        cos, sin = jnp.cos(ang), jnp.sin(ang)
        n = pos.shape[0]
        z = lambda w: jnp.zeros((n, w), F32)
        c = jnp.concatenate([cos, cos, jnp.ones((n, d - 2 * half), F32)], axis=1)
        s_lo = jnp.concatenate([z(half), sin, z(d - 2 * half)], axis=1)
        s_hi = jnp.concatenate([-sin, z(d - half)], axis=1)
        rep = LANES // d
        out += [jnp.tile(c, (1, rep)), jnp.tile(s_lo, (1, rep)), jnp.tile(s_hi, (1, rep))]
    return out


def _hgrn_consts(C):
    levels = []
    m = C // 2
    while m >= 1:
        levels.append(m)
        m //= 2
    t = np.arange(C)[:, None]
    r = np.arange(C)[None, :]
    mats = [(r <= t)]
    lvl = np.full((C, C), -1, np.int32)
    lvl[np.arange(C), np.arange(C)] = len(levels)
    for li, m in enumerate(levels):
        anchor = (t // (2 * m)) * (2 * m) + m - 1
        second = t > anchor
        mats.append(np.where(second, (r > anchor) & (r <= t), (r > t) & (r <= anchor)))
        same = (t // (2 * m)) == (r // (2 * m))
        split = same & ((t % (2 * m)) >= m) & ((r % (2 * m)) < m)
        lvl[split] = li
    dmat = np.concatenate([x.astype(np.float32) for x in mats], axis=0)
    return levels, jnp.asarray(dmat, BF16), jnp.asarray(lvl)


def _nt(a, b):
    return lax.dot_general(a, b, (((1,), (1,)), ((), ())), preferred_element_type=F32)


def _tn(a, b):
    return lax.dot_general(a, b, (((0,), (0,)), ((), ())), preferred_element_type=F32)


def _hgrn_kernel(zf_ref, zn_ref, zv_ref, d_ref, lvl_ref, par_ref, s0_ref, oa_ref, sf_ref,
                 st_ref, qa_ref, ka_ref, g_ref, *, C, levels):
    c = pl.program_id(1)
    dk = A_DK
    W = A_HEADS * dk
    log_lb, log_1m, one_m, lb_pos, o_gain = (par_ref[i:i + 1, :] for i in range(5))

    def stash_gates(z_ref, slot):
        aq = z_ref[:, 0:W]
        af = z_ref[:, W:2 * W]
        e = jnp.exp(-jnp.abs(af))
        ls = jnp.minimum(af, 0.0) - jnp.log1p(e)
        b2 = log_1m + ls
        lae = jnp.maximum(log_lb, b2) + jnp.log1p(jnp.exp(-jnp.abs(log_lb - b2)))
        log_f = jnp.where(lb_pos > 0.0, lae, ls)
        ka_ref[slot] = one_m * (jnp.where(af >= 0.0, e, 1.0) / (1.0 + e))
        qa_ref[slot] = aq * _sigmoid(aq)
        g_hi = log_f.astype(BF16)
        r1 = log_f - g_hi.astype(F32)
        g_mid = r1.astype(BF16)
        g_ref[slot, 0] = g_hi
        g_ref[slot, 1] = g_mid
        g_ref[slot, 2] = (r1 - g_mid.astype(F32)).astype(BF16)

    @pl.when(c == 0)
    def _():
        st_ref[...] = s0_ref[...]
        stash_gates(zf_ref, 0)

    cur = c % 2
    qa, ka = qa_ref[cur], ka_ref[cur]
    ai = zv_ref[:, 0:W]
    ag = zv_ref[:, W:2 * W]
    dm = d_ref[...]
    X = (jnp.dot(dm, g_ref[cur, 0], preferred_element_type=F32) + jnp.dot(dm, g_ref[cur, 1], preferred_element_type=F32)
         + jnp.dot(dm, g_ref[cur, 2], preferred_element_type=F32))
    lvl = lvl_ref[...]
    rows = lax.broadcasted_iota(jnp.int32, (C, 1), 0)
    nl = len(levels)
    b = X[0:C, :]
    b_last = b[C - 1:C, :]
    qe = (qa * jnp.exp(b)).astype(BF16)
    ke = (ka * jnp.exp(b_last - b)).astype(BF16)
    decay = jnp.exp(b_last)
    qb, kb, vb = qa.astype(BF16), ka.astype(BF16), ai.astype(BF16)
    zs = [(jnp.where((rows // m) % 2 == 1, qa, ka) * jnp.exp(X[(1 + li) * C:(2 + li) * C, :])).astype(BF16)
          for li, m in enumerate(levels)]
    gate = o_gain * _sigmoid(ag)
    for h in range(A_HEADS):
        sl = slice(h * dk, (h + 1) * dk)
        st = st_ref[h]
        o = _nt(qe[:, sl], st.astype(BF16))
        att = jnp.where(lvl == nl, _nt(qb[:, sl], kb[:, sl]), 0.0)
        for li in range(nl):
            att = att + jnp.where(lvl == li, _nt(zs[li][:, sl], zs[li][:, sl]), 0.0)
        o = o + jnp.dot(att.astype(BF16), vb[:, sl], preferred_element_type=F32)
        st_ref[h] = st * decay[:, sl] + _tn(vb[:, sl], ke[:, sl])
        on = o * lax.rsqrt(jnp.mean(o * o, axis=-1, keepdims=True) + RMS_EPS)
        oa_ref[:, sl] = (on * gate[:, sl]).astype(oa_ref.dtype)

    stash_gates(zn_ref, 1 - cur)

    @pl.when(c == pl.num_programs(1) - 1)
    def _():
        sf_ref[...] = st_ref[...]


def hgrn(za, par, s0t, C):
    B, T, _ = za.shape
    assert T % C == 0
    levels, dmat, lvl = _hgrn_consts(C)
    W = A_HEADS * A_DK
    st_spec = pl.BlockSpec((None, A_HEADS, A_DK, A_DK), lambda b, c: (b, 0, 0, 0))
    nc = T // C
    half = lambda index: pl.BlockSpec((None, C, 2 * W), index)
    return pl.pallas_call(
        functools.partial(_hgrn_kernel, C=C, levels=levels),
        grid=(B, nc),
        in_specs=[half(lambda b, c: (b, 0, 0)),
                  half(lambda b, c: (b, jnp.minimum(c + 1, nc - 1), 0)),
                  half(lambda b, c: (b, c, 1)),
                  pl.BlockSpec(dmat.shape, lambda b, c: (0, 0)),
                  pl.BlockSpec((C, C), lambda b, c: (0, 0)),
                  pl.BlockSpec((8, W), lambda b, c: (0, 0)),
                  st_spec],
        out_specs=[pl.BlockSpec((None, C, W), lambda b, c: (b, c, 0)), st_spec],
        out_shape=[jax.ShapeDtypeStruct((B, T, W), BF16),
                   jax.ShapeDtypeStruct((B, A_HEADS, A_DK, A_DK), F32)],
        scratch_shapes=[pltpu.VMEM((A_HEADS, A_DK, A_DK), F32),
                        pltpu.VMEM((2, C, W), F32), pltpu.VMEM((2, C, W), F32), pltpu.VMEM((2, 3, C, W), BF16)],
        compiler_params=_cparams(("parallel", "arbitrary")),
        name="hgrn",
    )(za, za, za, dmat, lvl, par, s0t)


def _tree_sum(parts):
    while len(parts) > 1:
        parts = [parts[r] + parts[r + 1] if r + 1 < len(parts) else parts[r] for r in range(0, len(parts), 2)]
    return parts[0]


def _fori_pairs(n, tile, init):
    acc = lax.fori_loop(0, n // 2, lambda p, a: a + (tile(2 * p) + tile(2 * p + 1)), init)
    return lax.fori_loop(n // 2 * 2, n, lambda j, a: a + tile(j), acc)


def _dsa_kernel(nkb_ref,qt_ref, qit_ref, wit_ref, qc_ref, kc_ref, k_ref, vt_ref, ki_ref, tri_ref,
                ob_ref, key_ref, hi_ref, acc_ref, *, topk):
    i = pl.program_id(1)
    nk = nkb_ref[i]
    qc = qc_ref[...]
    qit = qit_ref[...]
    rhs_i = jnp.concatenate([qit[h * IDX_DIM:(h + 1) * IDX_DIM, :] for h in range(IDX_HEADS)], axis=1)
    qt = qt_ref[...]
    rhs_q = jnp.concatenate([qt[h * B_DH:(h + 1) * B_DH, :] for h in range(B_HEADS)], axis=1)
    wsc = wit_ref[...] * (IDX_HEADS ** -0.5 * IDX_DIM ** -0.5)

    def score_tile(j):
        sc = jnp.dot(ki_ref[j], rhs_i, preferred_element_type=F32)
        s = jnp.maximum(sc[:, 0:QB], 0.0) * wsc[0:1, :]
        for h in range(1, IDX_HEADS):
            s = s + jnp.maximum(sc[:, h * QB:(h + 1) * QB], 0.0) * wsc[h:h + 1, :]
        s = jnp.where(kc_ref[j] <= qc, s, NEG_BIG)
        bits = lax.bitcast_convert_type(s, jnp.int32)
        bits = jnp.where((bits & 0x7FFFFFFF) < MIN_NORMAL_BITS, 0, bits)
        key_ref[j] = bits ^ ((bits >> 31) & 0x7FFFFFFF)
        hi_ref[j] = lax.bitcast_convert_type(bits & HI_MASK, F32).astype(BF16)

    @pl.loop(0, nk // 2)
    def _(p):
        score_tile(2 * p)
        score_tile(2 * p + 1)

    @pl.loop(nk // 2 * 2, nk)
    def _(j):
        score_tile(j)

    def count_hi(cand16):
        cand16 = jnp.where((cand16 >= 1) & (cand16 < MIN_NORMAL_BITS >> 16), MIN_NORMAL_BITS >> 16, cand16)
        pat = cand16 ^ ((cand16 >> 15) & 0x7FFF)
        cand = lax.bitcast_convert_type(jnp.left_shift(pat, 16), F32).astype(BF16)

        def tile(j):
            hit = jnp.where(hi_ref[j] >= cand, jnp.bfloat16(1), jnp.bfloat16(0))
            parts = [hit[r * BF16_ROWS:(r + 1) * BF16_ROWS, :] for r in range(KT // BF16_ROWS)]
            return _tree_sum(parts).astype(F32)
        acc = _fori_pairs(nk, tile, jnp.zeros((BF16_ROWS, QB), F32))
        return jnp.sum(acc, axis=0, keepdims=True).astype(jnp.int32)

    def count(pred):
        def tile(j):
            hit = jnp.where(pred(key_ref[j]), 1, 0)
            return _tree_sum([hit[r * SUBLANES:(r + 1) * SUBLANES, :] for r in range(KT // SUBLANES)])
        acc = _fori_pairs(nk, tile, jnp.zeros((SUBLANES, QB), jnp.int32))
        return jnp.sum(acc, axis=0, keepdims=True)

    c0 = count_hi(jnp.zeros((1, QB), jnp.int32))
    hi0 = jnp.where(c0 >= topk, 0, -(2 ** 15)).astype(jnp.int32)
    above0 = jnp.where(c0 >= topk, 0, c0)

    def hi_body(it, carry):
        thr, above = carry
        cand = thr | jnp.left_shift(jnp.int32(1), 14 - it)
        cnt = count_hi(cand)
        ok = cnt >= topk
        return jnp.where(ok, cand, thr), jnp.where(ok, above, cnt)

    hi16, above1 = lax.fori_loop(0, 15, hi_body, (hi0, above0))

    def lo_body(it, carry):
        thr, above = carry
        cand = thr | jnp.left_shift(jnp.int32(1), 15 - it)
        cnt = count(lambda kv: kv >= cand)
        ok = cnt >= topk
        return jnp.where(ok, cand, thr), jnp.where(ok, above, cnt)

    thr, above = lax.fori_loop(0, 16, lo_body, (jnp.left_shift(hi16, 16), above1))
    need = (topk - above).astype(F32)

    acc_ref[...] = jnp.zeros_like(acc_ref)
    c2 = B_DH ** -0.5 * LOG2E
    tri = tri_ref[...]

    def att_step(tiles, carry):
        ms, ls, tie_seen = carry
        sels, s_alls = [], []
        for j in tiles:
            kv = key_ref[j]
            eq = kv == thr
            eqf = jnp.where(eq, 1.0, 0.0)
            before = jnp.dot(tri, eqf.astype(BF16), preferred_element_type=F32) + tie_seen
            rank = jnp.where(kv > thr, -1.0, jnp.where(eq, before, NEVER))
            sels.append(jnp.where(kc_ref[j] <= qc, rank, NEVER) < need)
            tie_seen = tie_seen + jnp.sum(eqf, axis=0, keepdims=True)
            s_alls.append(jnp.dot(k_ref[j], rhs_q, preferred_element_type=F32))
        new_ms, new_ls, alphas = [], [], []
        ps = [[] for _ in tiles]
        for h in range(B_HEADS):
            ss = [jnp.where(sel, s_all[:, h * QB:(h + 1) * QB], NEG_BIG) for sel, s_all in zip(sels, s_alls)]
            m_new = ms[h]
            for s in ss:
                m_new = jnp.maximum(m_new, jnp.max(s, axis=0, keepdims=True))
            alpha = jnp.exp2((ms[h] - m_new) * c2)
            l_new = alpha * ls[h]
            for t, s in enumerate(ss):
                p = jnp.exp2((s - m_new) * c2)
                l_new = l_new + jnp.sum(p, axis=0, keepdims=True)
                ps[t].append(p.astype(BF16))
            new_ls.append(l_new)
            new_ms.append(m_new)
            alphas.append(alpha)
        pv = None
        for t, j in enumerate(tiles):
            part = jnp.dot(vt_ref[j], jnp.concatenate(ps[t], axis=1), preferred_element_type=F32)
            pv = part if pv is None else pv + part
        for h in range(B_HEADS):
            acc_ref[h] = alphas[h] * acc_ref[h] + pv[:, h * QB:(h + 1) * QB]
        return tuple(new_ms), tuple(new_ls), tie_seen

    init_m = tuple(jnp.full((1, QB), NEG_BIG, F32) for _ in range(B_HEADS))
    init_l = tuple(jnp.zeros((1, QB), F32) for _ in range(B_HEADS))
    carry = (init_m, init_l, jnp.zeros((1, QB), F32))
    carry = lax.fori_loop(0, nk // 2, lambda p, c: att_step([2 * p, 2 * p + 1], c), carry)
    ms, ls, _ = lax.fori_loop(nk // 2 * 2, nk, lambda j, c: att_step([j], c), carry)
    for h in range(B_HEADS):
        seen = ms[h] > 0.5 * NEG_BIG
        inv = jnp.where(seen, 1.0 / jnp.where(seen, ls[h], 1.0), 0.0)
        ob_ref[:, h * B_DH:(h + 1) * B_DH] = (acc_ref[h] * inv).T.astype(ob_ref.dtype)


def dsa(q, qi, wi, k_all, v_all, ki_all, q_chunk, k_chunk, nkeys, topk=TOPK):
    B, Tq, _ = q.shape
    Tk = k_all.shape[1]
    assert Tq % LANES == 0 and QB % LANES == 0
    nq, nkt = -(-Tq // QB), -(-Tk // KT)
    padq = lambda a: jnp.pad(a, ((0, 0), (0, nq * QB - Tq), (0, 0)))
    padk = lambda a: jnp.pad(a, ((0, 0), (0, nkt * KT - Tk), (0, 0)))
    qt = jnp.swapaxes(padq(q), 1, 2)
    qit = jnp.swapaxes(padq(qi), 1, 2)
    wit = jnp.swapaxes(padq(wi), 1, 2)
    k4 = padk(k_all.astype(BF16)).reshape(B, nkt, KT, B_DH)
    vt4 = jnp.swapaxes(padk(v_all.astype(BF16)).reshape(B, nkt, KT, B_DH), 2, 3)
    ki4 = padk(ki_all.astype(BF16)).reshape(B, nkt, KT, IDX_DIM)
    kc = jnp.pad(k_chunk.astype(jnp.int32), (0, nkt * KT - Tk), constant_values=NO_CHUNK)
    kc = jnp.broadcast_to(kc.reshape(nkt, KT, 1), (nkt, KT, QB))
    qc = jnp.pad(q_chunk.astype(jnp.int32), (0, nq * QB - Tq), constant_values=NO_QUERY).reshape(nq, 1, QB)
    per = QB // LANES
    nkeys = jnp.pad(nkeys.astype(jnp.int32), (0, nq * per - nkeys.shape[0])).reshape(nq, per).max(axis=1)
    nkb = (nkeys + (KT - 1)) // KT
    tri = jnp.asarray(np.tril(np.ones((KT, KT), np.float32), -1), BF16)
    grid_spec = pltpu.PrefetchScalarGridSpec(
        num_scalar_prefetch=1,
        grid=(B, nq),
        in_specs=[pl.BlockSpec((None, B_QW, QB), lambda b, i, n: (b, 0, i)),
                  pl.BlockSpec((None, IDX_HEADS * IDX_DIM, QB), lambda b, i, n: (b, 0, i)),
                  pl.BlockSpec((None, IDX_HEADS, QB), lambda b, i, n: (b, 0, i)),
                  pl.BlockSpec((None, 1, QB), lambda b, i, n: (i, 0, 0)),
                  pl.BlockSpec((nkt, KT, QB), lambda b, i, n: (0, 0, 0)),
                  pl.BlockSpec((None, nkt, KT, B_DH), lambda b, i, n: (b, 0, 0, 0)),
                  pl.BlockSpec((None, nkt, B_DH, KT), lambda b, i, n: (b, 0, 0, 0)),
                  pl.BlockSpec((None, nkt, KT, IDX_DIM), lambda b, i, n: (b, 0, 0, 0)),
                  pl.BlockSpec((KT, KT), lambda b, i, n: (0, 0))],
        out_specs=pl.BlockSpec((None, QB, B_QW), lambda b, i, n: (b, i, 0)),
        scratch_shapes=[pltpu.VMEM((nkt, KT, QB), jnp.int32),
                        pltpu.VMEM((nkt, KT, QB), BF16),
                        pltpu.VMEM((B_HEADS, B_DH, QB), F32)],
    )
    ob = pl.pallas_call(
        functools.partial(_dsa_kernel, topk=topk),
        grid_spec=grid_spec,
        out_shape=jax.ShapeDtypeStruct((B, nq * QB, B_QW), BF16),
        compiler_params=_cparams(("parallel", "arbitrary")),
        name="dsa",
    )(nkb, qt, qit, wit, qc, kc, k4, vt4, ki4, tri)
    return ob[:, :Tq]


def _cast_kernel(x_ref, o_ref):
    o_ref[...] = x_ref[...].astype(o_ref.dtype)


def cast_bf16(w):
    L, R, C = w.shape
    assert R % CAST_ROWS == 0
    spec = pl.BlockSpec((None, CAST_ROWS, C), lambda l, i: (l, i, 0))
    return pl.pallas_call(
        _cast_kernel,
        grid=(L, R // CAST_ROWS),
        in_specs=[spec],
        out_specs=spec,
        out_shape=jax.ShapeDtypeStruct(w.shape, BF16),
        compiler_params=_cparams(("parallel", "parallel")),
        name="cast",
    )(w)


def _pool_mix(u, halo, w_ref, sc, pos):
    run = jnp.concatenate([halo, u], axis=0)
    outs = []
    for g, w in enumerate(POOL_WINDOWS):
        lo = g * POOL_GROUP
        if g:
            run = run[:, POOL_GROUP:]
        run = run + pltpu.roll(run, w // 2, 0)
        cnt = jnp.clip(pos + 1, 1, w).astype(F32)
        mean = run[HALO:, 0:POOL_GROUP] / cnt
        d = (mean - u[:, lo:lo + POOL_GROUP]).astype(BF16)
        outs.append(jnp.dot(d, w_ref[g], preferred_element_type=F32))
    return jnp.concatenate(outs, axis=1) * sc


def _pool_kernel(h_ref, u_ref, halo_ref, w_ref, sc_ref, y_ref, *, tm, pos0, zero_first_halo):
    i = pl.program_id(1)
    halo = halo_ref[...]
    if zero_first_halo:
        halo = jnp.where(i > 0, halo, 0.0)
    pos = pos0 + i * tm + lax.broadcasted_iota(jnp.int32, (tm, 1), 0)
    y_ref[...] = h_ref[...] + _pool_mix(u_ref[...], halo, w_ref, sc_ref[...], pos)


def pool(h, u, halo_src, w_grp, scale, tm, pos0, zero_first_halo):
    G, R, D = h.shape
    assert R % tm == 0 and tm % HALO == 0
    row = pl.BlockSpec((None, tm, D), lambda b, i: (b, i, 0))
    step = tm // HALO
    halo = pl.BlockSpec((None, HALO, D), lambda b, i: (b, jnp.maximum(i * step - 1, 0), 0))
    return pl.pallas_call(
        functools.partial(_pool_kernel, tm=tm, pos0=pos0, zero_first_halo=zero_first_halo),
        grid=(G, R // tm),
        in_specs=[row, row, halo,
                  pl.BlockSpec((len(POOL_WINDOWS), POOL_GROUP, POOL_GROUP), lambda b, i: (0, 0, 0)),
                  pl.BlockSpec((1, D), lambda b, i: (0, 0))],
        out_specs=row,
        out_shape=jax.ShapeDtypeStruct((G, R, D), F32),
        compiler_params=_cparams(("parallel", "parallel")),
        name="pool",
    )(h, u, halo_src, w_grp, scale.reshape(1, D))


def _hgrn_params(lb, o_gain):
    lb_pos = lb > 0
    lb_safe = jnp.where(lb_pos, lb, 0.5)
    rows = [jnp.log(lb_safe), jnp.log1p(-lb_safe), 1.0 - lb, lb_pos.astype(F32), o_gain.astype(F32)]
    rows += [jnp.zeros_like(lb)] * 3
    return jnp.stack(rows).astype(F32)


def kernel(x_prompt, x_sample, cache_k, cache_v, cache_ki, state_hgrn, state_pool, meta_tokens, norm_ffn1, ffn1_wg, ffn1_wu, ffn1_wd, norm_mix, norm_ffn2, ffn2_wg, ffn2_wu, ffn2_wd, ab_w_in, ab_w_out, hgrn_lb_logits, hgrn_out_norm, attn_q_norm, attn_k_norm, pool_w, pool_scale):
    Bp, Sp, D = x_prompt.shape
    Bs, Ts, _ = x_sample.shape
    P = cache_k.shape[2]
    Tp = FRONT_PAD + N_META + Sp
    assert Tp % LANES == 0 and Ts == CHUNK and P % LANES == 0
    tm_p = Tp // 8
    Rs = Bs * Ts

    hp = jnp.concatenate([jnp.zeros((Bp, FRONT_PAD, D), F32),
                          jnp.broadcast_to(meta_tokens.astype(F32)[None], (Bp, N_META, D)), x_prompt], axis=1)
    hs = x_sample.reshape(1, Rs, D)

    row_p = jnp.arange(Tp, dtype=jnp.int32)
    pos_p = row_p - FRONT_PAD
    chunk_p = jnp.where(pos_p < 0, -2, jnp.where(pos_p < N_META, -1, (pos_p - N_META) // CHUNK))
    kchunk_p = jnp.where(pos_p < 0, jnp.int32(NO_CHUNK), chunk_p)
    nkeys_p = (jnp.arange(Tp // LANES, dtype=jnp.int32) + 1) * LANES
    tabs_p = rope_tables(pos_p)

    pos_s = P + jnp.arange(Ts, dtype=jnp.int32)
    qchunk_s = jnp.concatenate([pos_s // CHUNK, jnp.full((LANES - Ts,), -2, jnp.int32)])
    kchunk_s = jnp.arange(P + Ts, dtype=jnp.int32) // CHUNK
    nkeys_s = jnp.full((1,), P + Ts, jnp.int32)
    tabs_s = [jnp.tile(t, (Bs, 1)) for t in rope_tables(pos_s)]

    lb_soft = jax.nn.softmax(hgrn_lb_logits.astype(F32), axis=0)
    lb_all = jnp.cumsum(lb_soft, axis=0) - lb_soft[0]

    cast = lambda w: w.astype(BF16)
    ffn1_w = [cast_bf16(w) for w in (ffn1_wg, ffn1_wu, ffn1_wd)]
    ffn2_w = [cast_bf16(w) for w in (ffn2_wg, ffn2_wu, ffn2_wd)]
    kp_l, vp_l, kip_l, sp_l, pp_l = [], [], [], [], []
    ks_l, vs_l, kis_l, ss_l, ps_l = [], [], [], [], []
    for l in range(DEPTH):
        j = l // 2
        even = l % 2 == 0
        udt = BF16 if even else F32
        if even:
            hp, up = ffn(hp, norm_ffn1[l], *ffn1_w, l, g2=norm_mix[l], u_dtype=udt, tm=tm_p)
        else:
            hp = ffn(hp, norm_ffn1[l], *ffn1_w, l, tm=tm_p)
        hs, us = ffn(hs, norm_ffn1[l], *ffn1_w, l, g2=norm_mix[l], u_dtype=udt, tm=Rs)
        pool_p = None
        if even:
            w_in = jnp.pad(cast(ab_w_in[j]), ((0, 0), (0, IN_COLS_PAD - ab_w_in.shape[2])))
            w_out = cast(ab_w_out[j])
            par = _hgrn_params(lb_all[j], hgrn_out_norm[j])
            za, q, kv, qi, slab = inproj(up, w_in, attn_q_norm[j], attn_k_norm[j], tabs_p, tm_p)
            s0 = jnp.zeros((Bp, A_HEADS, A_DK, A_DK), F32)
            oa, sft = hgrn(za, par, s0, LANES)
            k_new, v_new, ki_new = kv[..., :B_DH], kv[..., B_DH:], slab[..., :IDX_DIM]
            wi = slab[..., IDX_DIM:IDX_DIM + IDX_HEADS]
            ob = dsa(q, qi, wi, k_new, v_new, ki_new, chunk_p, kchunk_p, nkeys_p)
            mix_p = (oa, ob, w_out)
            kp_l.append(k_new[:, FRONT_PAD:]); vp_l.append(v_new[:, FRONT_PAD:]); kip_l.append(ki_new[:, FRONT_PAD:])
            sp_l.append(jnp.swapaxes(sft, 2, 3))
            za, q, kv, qi, slab = inproj(us, w_in, attn_q_norm[j], attn_k_norm[j], tabs_s, Rs)
            s0 = jnp.swapaxes(state_hgrn[j].astype(F32), 2, 3)
            oa, sft = hgrn(za.reshape(Bs, Ts, A_COLS), par, s0, Ts)
            kv, slab = kv.reshape(Bs, Ts, -1), slab.reshape(Bs, Ts, -1)
            k_new, v_new, ki_new = kv[..., :B_DH], kv[..., B_DH:], slab[..., :IDX_DIM]
            wi = slab[..., IDX_DIM:IDX_DIM + IDX_HEADS]
            padq = lambda a: jnp.pad(a.reshape(Bs, Ts, -1), ((0, 0), (0, LANES - Ts), (0, 0)))
            cat = lambda past, new: jnp.concatenate([past.astype(F32), new], axis=1)
            ob = dsa(padq(q), padq(qi), padq(wi), cat(cache_k[j], k_new), cat(cache_v[j], v_new),
                     cat(cache_ki[j], ki_new), qchunk_s, kchunk_s, nkeys_s)
            mix_s = (oa.reshape(1, Rs, A_VW), ob[:, :Ts].reshape(1, Rs, B_QW), w_out)
            ks_l.append(k_new); vs_l.append(v_new); kis_l.append(ki_new)
            ss_l.append(jnp.swapaxes(sft, 2, 3).astype(state_hgrn.dtype))
        else:
            pw = cast(pool_w[j])
            pool_p = (norm_mix[l], pw, pool_scale[j], -FRONT_PAD)
            us3 = us.reshape(Bs, Ts, D)
            hist = state_pool[j].astype(F32)
            halo = jnp.pad(hist, ((0, 0), (HALO - POOL_HIST, 0), (0, 0)))
            hs = pool(hs.reshape(Bs, Ts, D), us3, halo, pw, pool_scale[j], Ts, P, False).reshape(1, Rs, D)
            ps_l.append(jnp.concatenate([hist, us3], axis=1)[:, -POOL_HIST:])
            mix_p = mix_s = None
        hp = ffn(hp, norm_ffn2[l], *ffn2_w, l, tm=tm_p, mix=mix_p, pool=pool_p)
        if pool_p is not None:
            hp, hist_p = hp
            pp_l.append(hist_p[:, HALO - POOL_HIST:])
        hs = ffn(hs, norm_ffn2[l], *ffn2_w, l, tm=Rs, mix=mix_s)

    y_prompt = hp[:, FRONT_PAD + N_META:]
    return (y_prompt, hs.reshape(Bs, Ts, D),
            jnp.stack(kp_l), jnp.stack(vp_l), jnp.stack(kip_l), jnp.stack(sp_l), jnp.stack(pp_l),
            jnp.stack(ks_l), jnp.stack(vs_l), jnp.stack(kis_l), jnp.stack(ss_l), jnp.stack(ps_l))
```

```python
import functools

import numpy as np
import jax
import jax.numpy as jnp
from jax import lax
from jax.experimental import pallas as pl
from jax.experimental.pallas import tpu as pltpu

F32 = jnp.float32
BF16 = jnp.bfloat16

D_MODEL = 1024
D_FF = 2816
DEPTH = 4
CHUNK = 64
N_META = 16
A_HEADS = 4
A_DK = 128
A_VW = 512
B_HEADS = 4
B_DH = 128
B_QW = 512
IDX_HEADS = 4
IDX_DIM = 64
TOPK = 256
ROPE_THETA = 500000.0
ROT_FRAC = 4
POOL_WINDOWS = (2, 4, 8, 16)
POOL_GROUP = D_MODEL // len(POOL_WINDOWS)
POOL_HIST = max(POOL_WINDOWS) - 1
RMS_EPS = 1e-6
NEG_BIG = -1e30

LANES = 128
SUBLANES = 8
BF16_ROWS = 16
MXU_N = 256
FFN_CHUNKS = 2
CAST_ROWS = 256
KT = 256
QB = 256
NO_CHUNK = 2 ** 30
NO_QUERY = -2
NEVER = 3e38
LOG2E = 1.4426950408889634
HALO = 16
FRONT_PAD = 2 * CHUNK - N_META
IN_COLS_PAD = 3200
A_COLS = 2048
VMEM_LIMIT = 56 * 1024 * 1024
HI_MASK = -65536
MIN_NORMAL_BITS = 0x00800000


def _cparams(sem):
    return pltpu.CompilerParams(dimension_semantics=sem, vmem_limit_bytes=VMEM_LIMIT)


def _sigmoid(x):
    return 1.0 / (1.0 + jnp.exp(-x))


def _rms(x, g):
    return x * lax.rsqrt(jnp.mean(x * x, axis=-1, keepdims=True) + RMS_EPS) * g


def _ffn_kernel(*refs, row_tiles, ff_chunks, emit_u, mix, pool_pos0):
    refs = list(refs)
    x_ref = refs.pop(0)
    if mix:
        oa_ref, ob_ref, wo_ref = refs.pop(0), refs.pop(0), refs.pop(0)
    if pool_pos0 is not None:
        halo_ref, gm_ref, pw_ref, psc_ref = (refs.pop(0) for _ in range(4))
        hist_ref = refs.pop()
    g_ref, wg_ref, wu_ref, wd_ref = refs[:4]
    if emit_u:
        g2_ref, y_ref, u_ref = refs[4:]
    else:
        (y_ref,) = refs[4:]
    if pool_pos0 is not None:
        i = pl.program_id(1)
        tm = x_ref.shape[0]
        u_all = _rms(x_ref[...], gm_ref[...])
        halo0 = _rms(jnp.where(i > 0, halo_ref[...], 0.0), gm_ref[...])
        hist_ref[...] = u_all[tm - HALO:tm, :]
    for r0, r1 in row_tiles:
        x = x_ref[r0:r1, :]
        if pool_pos0 is not None:
            halo = halo0 if r0 == 0 else u_all[r0 - HALO:r0, :]
            pos = pool_pos0 + i * tm + r0 + lax.broadcasted_iota(jnp.int32, (r1 - r0, 1), 0)
            x = x + _pool_mix(u_all[r0:r1, :], halo, pw_ref, psc_ref[...], pos)
        if mix:
            x = x + jnp.dot(oa_ref[r0:r1, :], wo_ref[0:A_VW, :], preferred_element_type=F32)
            x = x + jnp.dot(ob_ref[r0:r1, :], wo_ref[A_VW:A_VW + B_QW, :], preferred_element_type=F32)
        xn = _rms(x, g_ref[...]).astype(BF16)
        acc = None
        for c0, c1 in ff_chunks:
            gate = jnp.dot(xn, wg_ref[:, c0:c1], preferred_element_type=F32)
            up = jnp.dot(xn, wu_ref[:, c0:c1], preferred_element_type=F32)
            act = (gate * _sigmoid(gate) * up).astype(BF16)
            part = jnp.dot(act, wd_ref[c0:c1, :], preferred_element_type=F32)
            acc = part if acc is None else acc + part
        y = x + 0.5 * acc
        y_ref[r0:r1, :] = y
        if emit_u:
            u_ref[r0:r1, :] = _rms(y, g2_ref[...]).astype(u_ref.dtype)


def _ffn_tiling(tm, F):
    half = (tm // (2 * BF16_ROWS)) * BF16_ROWS
    row_tiles = [(0, half), (half, tm)] if half else [(0, tm)]
    n_col = -(-F // MXU_N)
    per = -(-n_col // FFN_CHUNKS)
    edges = [min(c * per * MXU_N, F) for c in range(FFN_CHUNKS + 1)]
    return row_tiles, [(a, b) for a, b in zip(edges[:-1], edges[1:]) if b > a]


def ffn(x, g, wg, wu, wd, layer, g2=None, u_dtype=BF16, tm=528, mix=None, pool=None):
    G, R, D = x.shape
    F = wg.shape[2]
    tm = min(tm, R)
    assert R % tm == 0
    row_tiles, ff_chunks = _ffn_tiling(tm, F)
    emit_u = g2 is not None
    row = lambda c: pl.BlockSpec((None, tm, c), lambda b, i: (b, i, 0))
    vec = pl.BlockSpec((1, D), lambda b, i: (0, 0))
    resident = lambda shape: pl.BlockSpec(shape, lambda b, i: (0,) * len(shape), pipeline_mode=pl.Buffered(1))
    in_specs, args = [row(D)], [x]
    if mix is not None:
        oa, ob, w_out = mix
        in_specs += [row(A_VW), row(B_QW), resident(w_out.shape)]
        args += [oa, ob, w_out]
    if pool is not None:
        g_mix, w_grp, scale, pos0 = pool
        assert tm % HALO == 0
        step = tm // HALO
        in_specs += [pl.BlockSpec((None, HALO, D), lambda b, i: (b, jnp.maximum(i * step - 1, 0), 0)),
                     vec, resident(w_grp.shape), vec]
        args += [x, g_mix.reshape(1, D), w_grp, scale.reshape(1, D)]
    stacked = lambda r, c: pl.BlockSpec((None, r, c), lambda b, i: (layer, 0, 0), pipeline_mode=pl.Buffered(1))
    in_specs += [vec, stacked(D, F), stacked(D, F), stacked(F, D)]
    args += [g.reshape(1, D), wg, wu, wd]
    row = row(D)
    out_shape = [jax.ShapeDtypeStruct((G, R, D), F32)]
    out_specs = [row]
    if emit_u:
        in_specs.append(vec)
        args.append(g2.reshape(1, D))
        out_shape.append(jax.ShapeDtypeStruct((G, R, D), u_dtype))
        out_specs.append(row)
    if pool is not None:
        out_shape.append(jax.ShapeDtypeStruct((G, HALO, D), F32))
        out_specs.append(pl.BlockSpec((None, HALO, D), lambda b, i: (b, 0, 0)))
    out = pl.pallas_call(
        functools.partial(_ffn_kernel, row_tiles=row_tiles, ff_chunks=ff_chunks, emit_u=emit_u,
                          mix=mix is not None, pool_pos0=None if pool is None else pool[3]),
        grid=(G, R // tm),
        in_specs=in_specs,
        out_specs=out_specs,
        out_shape=out_shape,
        compiler_params=_cparams(("parallel", "arbitrary" if pool is not None else "parallel")),
        name="ffn",
    )(*args)
    return tuple(out) if len(out) > 1 else out[0]


def _rope(x, c, s_lo, s_hi, half):
    return x * c + pltpu.roll(x, half, 1) * s_lo + pltpu.roll(x, LANES - half, 1) * s_hi


def _inproj_kernel(u_ref, w_ref, qg_ref, kg_ref, cb_ref, sbl_ref, sbh_ref, ci_ref, sil_ref, sih_ref,
                   za_ref, q_ref, kv_ref, qi_ref, slab_ref):
    u = u_ref[...]
    hb = B_DH // ROT_FRAC // 2
    hi = IDX_DIM // ROT_FRAC // 2
    cb, sbl, sbh = cb_ref[...], sbl_ref[...], sbh_ref[...]
    ci, sil, sih = ci_ref[...], sil_ref[...], sih_ref[...]
    zq = jnp.dot(u, w_ref[:, A_COLS:A_COLS + B_QW], preferred_element_type=F32)
    for h in range(B_HEADS):
        xh = _rms(zq[:, h * B_DH:(h + 1) * B_DH], qg_ref[...])
        q_ref[:, h * B_DH:(h + 1) * B_DH] = _rope(xh, cb, sbl, sbh, hb).astype(q_ref.dtype)
    c0 = A_COLS + B_QW
    zkv = jnp.dot(u, w_ref[:, c0:c0 + 2 * B_DH], preferred_element_type=F32)
    kv_ref[:, 0:B_DH] = _rope(_rms(zkv[:, 0:B_DH], kg_ref[...]), cb, sbl, sbh, hb)
    kv_ref[:, B_DH:2 * B_DH] = zkv[:, B_DH:2 * B_DH]
    c1 = c0 + 2 * B_DH
    zi = jnp.dot(u, w_ref[:, c1:c1 + IDX_HEADS * IDX_DIM + LANES], preferred_element_type=F32)
    for t in range(IDX_HEADS * IDX_DIM // LANES):
        qi_ref[:, t * LANES:(t + 1) * LANES] = _rope(zi[:, t * LANES:(t + 1) * LANES], ci, sil, sih, hi).astype(qi_ref.dtype)
    tail = zi[:, IDX_HEADS * IDX_DIM:]
    lane = lax.broadcasted_iota(jnp.int32, tail.shape, 1)
    slab_ref[...] = jnp.where(lane < IDX_DIM, _rope(tail, ci, sil, sih, hi), tail)
    za_ref[...] = jnp.dot(u, w_ref[:, 0:A_COLS], preferred_element_type=F32)


def inproj(u, w_pad, q_gain, k_gain, tabs, tm):
    G, R, D = u.shape
    assert R % tm == 0
    row = lambda c: pl.BlockSpec((None, tm, c), lambda b, i: (b, i, 0))
    tab = pl.BlockSpec((tm, LANES), lambda b, i: (i, 0))
    vec = pl.BlockSpec((1, LANES), lambda b, i: (0, 0))
    return pl.pallas_call(
        _inproj_kernel,
        grid=(G, R // tm),
        in_specs=[row(D), pl.BlockSpec((D, IN_COLS_PAD), lambda b, i: (0, 0)), vec, vec] + [tab] * 6,
        out_specs=[row(A_COLS), row(B_QW), row(2 * B_DH), row(IDX_HEADS * IDX_DIM), row(LANES)],
        out_shape=[jax.ShapeDtypeStruct((G, R, A_COLS), F32),
                   jax.ShapeDtypeStruct((G, R, B_QW), BF16),
                   jax.ShapeDtypeStruct((G, R, 2 * B_DH), F32),
                   jax.ShapeDtypeStruct((G, R, IDX_HEADS * IDX_DIM), BF16),
                   jax.ShapeDtypeStruct((G, R, LANES), F32)],
        compiler_params=_cparams(("parallel", "parallel")),
        name="inproj",
    )(u, w_pad, q_gain.reshape(1, LANES), k_gain.reshape(1, LANES), *tabs)


def rope_tables(pos):
    pos = pos.astype(F32)
    out = []
    for d in (B_DH, IDX_DIM):
        half = d // ROT_FRAC // 2
        inv = ROPE_THETA ** (-jnp.arange(half, dtype=F32) / half)
        ang = pos[:, None] * inv[None, :]
        cos, sin = jnp.cos(ang), jnp.sin(ang)
        n = pos.shape[0]
        z = lambda w: jnp.zeros((n, w), F32)
        c = jnp.concatenate([cos, cos, jnp.ones((n, d - 2 * half), F32)], axis=1)
        s_lo = jnp.concatenate([z(half), sin, z(d - 2 * half)], axis=1)
        s_hi = jnp.concatenate([-sin, z(d - half)], axis=1)
        rep = LANES // d
        out += [jnp.tile(c, (1, rep)), jnp.tile(s_lo, (1, rep)), jnp.tile(s_hi, (1, rep))]
    return out


def _hgrn_consts(C):
    levels = []
    m = C // 2
    while m >= 1:
        levels.append(m)
        m //= 2
    t = np.arange(C)[:, None]
    r = np.arange(C)[None, :]
    mats = [(r <= t)]
    lvl = np.full((C, C), -1, np.int32)
    lvl[np.arange(C), np.arange(C)] = len(levels)
    for li, m in enumerate(levels):
        anchor = (t // (2 * m)) * (2 * m) + m - 1
        second = t > anchor
        mats.append(np.where(second, (r > anchor) & (r <= t), (r > t) & (r <= anchor)))
        same = (t // (2 * m)) == (r // (2 * m))
        split = same & ((t % (2 * m)) >= m) & ((r % (2 * m)) < m)
        lvl[split] = li
    dmat = np.concatenate([x.astype(np.float32) for x in mats], axis=0)
    return levels, jnp.asarray(dmat, BF16), jnp.asarray(lvl)


def _nt(a, b):
    return lax.dot_general(a, b, (((1,), (1,)), ((), ())), preferred_element_type=F32)


def _tn(a, b):
    return lax.dot_general(a, b, (((0,), (0,)), ((), ())), preferred_element_type=F32)


def _hgrn_kernel(zf_ref, zn_ref, zv_ref, d_ref, lvl_ref, par_ref, s0_ref, oa_ref, sf_ref,
                 st_ref, qa_ref, ka_ref, g_ref, *, C, levels):
    c = pl.program_id(1)
    dk = A_DK
    W = A_HEADS * dk
    log_lb, log_1m, one_m, lb_pos, o_gain = (par_ref[i:i + 1, :] for i in range(5))

    def stash_gates(z_ref, slot):
        aq = z_ref[:, 0:W]
        af = z_ref[:, W:2 * W]
        e = jnp.exp(-jnp.abs(af))
        ls = jnp.minimum(af, 0.0) - jnp.log1p(e)
        b2 = log_1m + ls
        lae = jnp.maximum(log_lb, b2) + jnp.log1p(jnp.exp(-jnp.abs(log_lb - b2)))
        log_f = jnp.where(lb_pos > 0.0, lae, ls)
        ka_ref[slot] = one_m * (jnp.where(af >= 0.0, e, 1.0) / (1.0 + e))
        qa_ref[slot] = aq * _sigmoid(aq)
        g_hi = log_f.astype(BF16)
        r1 = log_f - g_hi.astype(F32)
        g_mid = r1.astype(BF16)
        g_ref[slot, 0] = g_hi
        g_ref[slot, 1] = g_mid
        g_ref[slot, 2] = (r1 - g_mid.astype(F32)).astype(BF16)

    @pl.when(c == 0)
    def _():
        st_ref[...] = s0_ref[...]
        stash_gates(zf_ref, 0)

    cur = c % 2
    qa, ka = qa_ref[cur], ka_ref[cur]
    ai = zv_ref[:, 0:W]
    ag = zv_ref[:, W:2 * W]
    dm = d_ref[...]
    X = (jnp.dot(dm, g_ref[cur, 0], preferred_element_type=F32) + jnp.dot(dm, g_ref[cur, 1], preferred_element_type=F32)
         + jnp.dot(dm, g_ref[cur, 2], preferred_element_type=F32))
    lvl = lvl_ref[...]
    rows = lax.broadcasted_iota(jnp.int32, (C, 1), 0)
    nl = len(levels)
    b = X[0:C, :]
    b_last = b[C - 1:C, :]
    qe = (qa * jnp.exp(b)).astype(BF16)
    ke = (ka * jnp.exp(b_last - b)).astype(BF16)
    decay = jnp.exp(b_last)
    qb, kb, vb = qa.astype(BF16), ka.astype(BF16), ai.astype(BF16)
    zs = [(jnp.where((rows // m) % 2 == 1, qa, ka) * jnp.exp(X[(1 + li) * C:(2 + li) * C, :])).astype(BF16)
          for li, m in enumerate(levels)]
    gate = o_gain * _sigmoid(ag)
    for h in range(A_HEADS):
        sl = slice(h * dk, (h + 1) * dk)
        st = st_ref[h]
        o = _nt(qe[:, sl], st.astype(BF16))
        att = jnp.where(lvl == nl, _nt(qb[:, sl], kb[:, sl]), 0.0)
        for li in range(nl):
            att = att + jnp.where(lvl == li, _nt(zs[li][:, sl], zs[li][:, sl]), 0.0)
        o = o + jnp.dot(att.astype(BF16), vb[:, sl], preferred_element_type=F32)
        st_ref[h] = st * decay[:, sl] + _tn(vb[:, sl], ke[:, sl])
        on = o * lax.rsqrt(jnp.mean(o * o, axis=-1, keepdims=True) + RMS_EPS)
        oa_ref[:, sl] = (on * gate[:, sl]).astype(oa_ref.dtype)

    stash_gates(zn_ref, 1 - cur)

    @pl.when(c == pl.num_programs(1) - 1)
    def _():
        sf_ref[...] = st_ref[...]


def hgrn(za, par, s0t, C):
    B, T, _ = za.shape
    assert T % C == 0
    levels, dmat, lvl = _hgrn_consts(C)
    W = A_HEADS * A_DK
    st_spec = pl.BlockSpec((None, A_HEADS, A_DK, A_DK), lambda b, c: (b, 0, 0, 0))
    nc = T // C
    half = lambda index: pl.BlockSpec((None, C, 2 * W), index)
    return pl.pallas_call(
        functools.partial(_hgrn_kernel, C=C, levels=levels),
        grid=(B, nc),
        in_specs=[half(lambda b, c: (b, 0, 0)),
                  half(lambda b, c: (b, jnp.minimum(c + 1, nc - 1), 0)),
                  half(lambda b, c: (b, c, 1)),
                  pl.BlockSpec(dmat.shape, lambda b, c: (0, 0)),
                  pl.BlockSpec((C, C), lambda b, c: (0, 0)),
                  pl.BlockSpec((8, W), lambda b, c: (0, 0)),
                  st_spec],
        out_specs=[pl.BlockSpec((None, C, W), lambda b, c: (b, c, 0)), st_spec],
        out_shape=[jax.ShapeDtypeStruct((B, T, W), BF16),
                   jax.ShapeDtypeStruct((B, A_HEADS, A_DK, A_DK), F32)],
        scratch_shapes=[pltpu.VMEM((A_HEADS, A_DK, A_DK), F32),
                        pltpu.VMEM((2, C, W), F32), pltpu.VMEM((2, C, W), F32), pltpu.VMEM((2, 3, C, W), BF16)],
        compiler_params=_cparams(("parallel", "arbitrary")),
        name="hgrn",
    )(za, za, za, dmat, lvl, par, s0t)


def _tree_sum(parts):
    while len(parts) > 1:
        parts = [parts[r] + parts[r + 1] if r + 1 < len(parts) else parts[r] for r in range(0, len(parts), 2)]
    return parts[0]


def _fori_pairs(n, tile, init):
    acc = lax.fori_loop(0, n // 2, lambda p, a: a + (tile(2 * p) + tile(2 * p + 1)), init)
    return lax.fori_loop(n // 2 * 2, n, lambda j, a: a + tile(j), acc)


def _dsa_kernel(nkb_ref,qt_ref, qit_ref, wit_ref, qc_ref, kc_ref, k_ref, vt_ref, ki_ref, tri_ref,
                ob_ref, key_ref, hi_ref, acc_ref, *, topk):
    i = pl.program_id(1)
    nk = nkb_ref[i]
    qc = qc_ref[...]
    qit = qit_ref[...]
    rhs_i = jnp.concatenate([qit[h * IDX_DIM:(h + 1) * IDX_DIM, :] for h in range(IDX_HEADS)], axis=1)
    qt = qt_ref[...]
    rhs_q = jnp.concatenate([qt[h * B_DH:(h + 1) * B_DH, :] for h in range(B_HEADS)], axis=1)
    wsc = wit_ref[...] * (IDX_HEADS ** -0.5 * IDX_DIM ** -0.5)

    def score_tile(j):
        sc = jnp.dot(ki_ref[j], rhs_i, preferred_element_type=F32)
        s = jnp.maximum(sc[:, 0:QB], 0.0) * wsc[0:1, :]
        for h in range(1, IDX_HEADS):
            s = s + jnp.maximum(sc[:, h * QB:(h + 1) * QB], 0.0) * wsc[h:h + 1, :]
        s = jnp.where(kc_ref[j] <= qc, s, NEG_BIG)
        bits = lax.bitcast_convert_type(s, jnp.int32)
        bits = jnp.where((bits & 0x7FFFFFFF) < MIN_NORMAL_BITS, 0, bits)
        key_ref[j] = bits ^ ((bits >> 31) & 0x7FFFFFFF)
        hi_ref[j] = lax.bitcast_convert_type(bits & HI_MASK, F32).astype(BF16)

    @pl.loop(0, nk // 2)
    def _(p):
        score_tile(2 * p)
        score_tile(2 * p + 1)

    @pl.loop(nk // 2 * 2, nk)
    def _(j):
        score_tile(j)

    def count_hi(cand16):
        cand16 = jnp.where((cand16 >= 1) & (cand16 < MIN_NORMAL_BITS >> 16), MIN_NORMAL_BITS >> 16, cand16)
        pat = cand16 ^ ((cand16 >> 15) & 0x7FFF)
        cand = lax.bitcast_convert_type(jnp.left_shift(pat, 16), F32).astype(BF16)

        def tile(j):
            hit = jnp.where(hi_ref[j] >= cand, jnp.bfloat16(1), jnp.bfloat16(0))
            parts = [hit[r * BF16_ROWS:(r + 1) * BF16_ROWS, :] for r in range(KT // BF16_ROWS)]
            return _tree_sum(parts).astype(F32)
        acc = _fori_pairs(nk, tile, jnp.zeros((BF16_ROWS, QB), F32))
        return jnp.sum(acc, axis=0, keepdims=True).astype(jnp.int32)

    def count(pred):
        def tile(j):
            hit = jnp.where(pred(key_ref[j]), 1, 0)
            return _tree_sum([hit[r * SUBLANES:(r + 1) * SUBLANES, :] for r in range(KT // SUBLANES)])
        acc = _fori_pairs(nk, tile, jnp.zeros((SUBLANES, QB), jnp.int32))
        return jnp.sum(acc, axis=0, keepdims=True)

    c0 = count_hi(jnp.zeros((1, QB), jnp.int32))
    hi0 = jnp.where(c0 >= topk, 0, -(2 ** 15)).astype(jnp.int32)
    above0 = jnp.where(c0 >= topk, 0, c0)

    def hi_body(it, carry):
        thr, above = carry
        cand = thr | jnp.left_shift(jnp.int32(1), 14 - it)
        cnt = count_hi(cand)
        ok = cnt >= topk
        return jnp.where(ok, cand, thr), jnp.where(ok, above, cnt)

    hi16, above1 = lax.fori_loop(0, 15, hi_body, (hi0, above0))

    def lo_body(it, carry):
        thr, above = carry
        cand = thr | jnp.left_shift(jnp.int32(1), 15 - it)
        cnt = count(lambda kv: kv >= cand)
        ok = cnt >= topk
        return jnp.where(ok, cand, thr), jnp.where(ok, above, cnt)

    thr, above = lax.fori_loop(0, 16, lo_body, (jnp.left_shift(hi16, 16), above1))
    need = (topk - above).astype(F32)

    acc_ref[...] = jnp.zeros_like(acc_ref)
    c2 = B_DH ** -0.5 * LOG2E
    tri = tri_ref[...]

    def att_step(tiles, carry):
        ms, ls, tie_seen = carry
        sels, s_alls = [], []
        for j in tiles:
            kv = key_ref[j]
            eq = kv == thr
            eqf = jnp.where(eq, 1.0, 0.0)
            before = jnp.dot(tri, eqf.astype(BF16), preferred_element_type=F32) + tie_seen
            rank = jnp.where(kv > thr, -1.0, jnp.where(eq, before, NEVER))
            sels.append(jnp.where(kc_ref[j] <= qc, rank, NEVER) < need)
            tie_seen = tie_seen + jnp.sum(eqf, axis=0, keepdims=True)
            s_alls.append(jnp.dot(k_ref[j], rhs_q, preferred_element_type=F32))
        new_ms, new_ls, alphas = [], [], []
        ps = [[] for _ in tiles]
        for h in range(B_HEADS):
            ss = [jnp.where(sel, s_all[:, h * QB:(h + 1) * QB], NEG_BIG) for sel, s_all in zip(sels, s_alls)]
            m_new = ms[h]
            for s in ss:
                m_new = jnp.maximum(m_new, jnp.max(s, axis=0, keepdims=True))
            alpha = jnp.exp2((ms[h] - m_new) * c2)
            l_new = alpha * ls[h]
            for t, s in enumerate(ss):
                p = jnp.exp2((s - m_new) * c2)
                l_new = l_new + jnp.sum(p, axis=0, keepdims=True)
                ps[t].append(p.astype(BF16))
            new_ls.append(l_new)
            new_ms.append(m_new)
            alphas.append(alpha)
        pv = None
        for t, j in enumerate(tiles):
            part = jnp.dot(vt_ref[j], jnp.concatenate(ps[t], axis=1), preferred_element_type=F32)
            pv = part if pv is None else pv + part
        for h in range(B_HEADS):
            acc_ref[h] = alphas[h] * acc_ref[h] + pv[:, h * QB:(h + 1) * QB]
        return tuple(new_ms), tuple(new_ls), tie_seen

    init_m = tuple(jnp.full((1, QB), NEG_BIG, F32) for _ in range(B_HEADS))
    init_l = tuple(jnp.zeros((1, QB), F32) for _ in range(B_HEADS))
    carry = (init_m, init_l, jnp.zeros((1, QB), F32))
    carry = lax.fori_loop(0, nk // 2, lambda p, c: att_step([2 * p, 2 * p + 1], c), carry)
    ms, ls, _ = lax.fori_loop(nk // 2 * 2, nk, lambda j, c: att_step([j], c), carry)
    for h in range(B_HEADS):
        seen = ms[h] > 0.5 * NEG_BIG
        inv = jnp.where(seen, 1.0 / jnp.where(seen, ls[h], 1.0), 0.0)
        ob_ref[:, h * B_DH:(h + 1) * B_DH] = (acc_ref[h] * inv).T.astype(ob_ref.dtype)


def dsa(q, qi, wi, k_all, v_all, ki_all, q_chunk, k_chunk, nkeys, topk=TOPK):
    B, Tq, _ = q.shape
    Tk = k_all.shape[1]
    assert Tq % LANES == 0 and QB % LANES == 0
    nq, nkt = -(-Tq // QB), -(-Tk // KT)
    padq = lambda a: jnp.pad(a, ((0, 0), (0, nq * QB - Tq), (0, 0)))
    padk = lambda a: jnp.pad(a, ((0, 0), (0, nkt * KT - Tk), (0, 0)))
    qt = jnp.swapaxes(padq(q), 1, 2)
    qit = jnp.swapaxes(padq(qi), 1, 2)
    wit = jnp.swapaxes(padq(wi), 1, 2)
    k4 = padk(k_all.astype(BF16)).reshape(B, nkt, KT, B_DH)
    vt4 = jnp.swapaxes(padk(v_all.astype(BF16)).reshape(B, nkt, KT, B_DH), 2, 3)
    ki4 = padk(ki_all.astype(BF16)).reshape(B, nkt, KT, IDX_DIM)
    kc = jnp.pad(k_chunk.astype(jnp.int32), (0, nkt * KT - Tk), constant_values=NO_CHUNK)
    kc = jnp.broadcast_to(kc.reshape(nkt, KT, 1), (nkt, KT, QB))
    qc = jnp.pad(q_chunk.astype(jnp.int32), (0, nq * QB - Tq), constant_values=NO_QUERY).reshape(nq, 1, QB)
    per = QB // LANES
    nkeys = jnp.pad(nkeys.astype(jnp.int32), (0, nq * per - nkeys.shape[0])).reshape(nq, per).max(axis=1)
    nkb = (nkeys + (KT - 1)) // KT
    tri = jnp.asarray(np.tril(np.ones((KT, KT), np.float32), -1), BF16)
    grid_spec = pltpu.PrefetchScalarGridSpec(
        num_scalar_prefetch=1,
        grid=(B, nq),
        in_specs=[pl.BlockSpec((None, B_QW, QB), lambda b, i, n: (b, 0, i)),
                  pl.BlockSpec((None, IDX_HEADS * IDX_DIM, QB), lambda b, i, n: (b, 0, i)),
                  pl.BlockSpec((None, IDX_HEADS, QB), lambda b, i, n: (b, 0, i)),
                  pl.BlockSpec((None, 1, QB), lambda b, i, n: (i, 0, 0)),
                  pl.BlockSpec((nkt, KT, QB), lambda b, i, n: (0, 0, 0)),
                  pl.BlockSpec((None, nkt, KT, B_DH), lambda b, i, n: (b, 0, 0, 0)),
                  pl.BlockSpec((None, nkt, B_DH, KT), lambda b, i, n: (b, 0, 0, 0)),
                  pl.BlockSpec((None, nkt, KT, IDX_DIM), lambda b, i, n: (b, 0, 0, 0)),
                  pl.BlockSpec((KT, KT), lambda b, i, n: (0, 0))],
        out_specs=pl.BlockSpec((None, QB, B_QW), lambda b, i, n: (b, i, 0)),
        scratch_shapes=[pltpu.VMEM((nkt, KT, QB), jnp.int32),
                        pltpu.VMEM((nkt, KT, QB), BF16),
                        pltpu.VMEM((B_HEADS, B_DH, QB), F32)],
    )
    return pl.pallas_call(
        functools.partial(_dsa_kernel, topk=topk),
        grid_spec=grid_spec,
        out_shape=jax.ShapeDtypeStruct((B, Tq, B_QW), BF16),
        compiler_params=_cparams(("parallel", "arbitrary")),
        name="dsa",
    )(nkb, qt, qit, wit, qc, kc, k4, vt4, ki4, tri)


def _cast_kernel(x_ref, o_ref):
    C = x_ref.shape[1]
    o_ref[:, 0:C] = x_ref[...].astype(o_ref.dtype)
    if o_ref.shape[1] > C:
        o_ref[:, C:] = jnp.zeros((o_ref.shape[0], o_ref.shape[1] - C), o_ref.dtype)


def cast_bf16(w, cols=None):
    L, R, C = w.shape
    cols = C if cols is None else cols
    assert R % CAST_ROWS == 0
    spec = lambda c: pl.BlockSpec((None, CAST_ROWS, c), lambda l, i: (l, i, 0))
    return pl.pallas_call(
        _cast_kernel,
        grid=(L, R // CAST_ROWS),
        in_specs=[spec(C)],
        out_specs=spec(cols),
        out_shape=jax.ShapeDtypeStruct((L, R, cols), BF16),
        compiler_params=_cparams(("parallel", "parallel")),
        name="cast",
    )(w)


def _pool_mix(u, halo, w_ref, sc, pos):
    run = jnp.concatenate([halo, u], axis=0)
    outs = []
    for g, w in enumerate(POOL_WINDOWS):
        lo = g * POOL_GROUP
        if g:
            run = run[:, POOL_GROUP:]
        run = run + pltpu.roll(run, w // 2, 0)
        cnt = jnp.clip(pos + 1, 1, w).astype(F32)
        mean = run[HALO:, 0:POOL_GROUP] / cnt
        d = (mean - u[:, lo:lo + POOL_GROUP]).astype(BF16)
        outs.append(jnp.dot(d, w_ref[g], preferred_element_type=F32))
    return jnp.concatenate(outs, axis=1) * sc


def _pool_kernel(h_ref, u_ref, halo_ref, w_ref, sc_ref, y_ref, *, tm, pos0, zero_first_halo):
    i = pl.program_id(1)
    halo = halo_ref[...]
    if zero_first_halo:
        halo = jnp.where(i > 0, halo, 0.0)
    pos = pos0 + i * tm + lax.broadcasted_iota(jnp.int32, (tm, 1), 0)
    y_ref[...] = h_ref[...] + _pool_mix(u_ref[...], halo, w_ref, sc_ref[...], pos)


def pool(h, u, halo_src, w_grp, scale, tm, pos0, zero_first_halo):
    G, R, D = h.shape
    assert R % tm == 0 and tm % HALO == 0
    row = pl.BlockSpec((None, tm, D), lambda b, i: (b, i, 0))
    step = tm // HALO
    halo = pl.BlockSpec((None, HALO, D), lambda b, i: (b, jnp.maximum(i * step - 1, 0), 0))
    return pl.pallas_call(
        functools.partial(_pool_kernel, tm=tm, pos0=pos0, zero_first_halo=zero_first_halo),
        grid=(G, R // tm),
        in_specs=[row, row, halo,
                  pl.BlockSpec((len(POOL_WINDOWS), POOL_GROUP, POOL_GROUP), lambda b, i: (0, 0, 0)),
                  pl.BlockSpec((1, D), lambda b, i: (0, 0))],
        out_specs=row,
        out_shape=jax.ShapeDtypeStruct((G, R, D), F32),
        compiler_params=_cparams(("parallel", "parallel")),
        name="pool",
    )(h, u, halo_src, w_grp, scale.reshape(1, D))


def _hgrn_params(lb, o_gain):
    lb_pos = lb > 0
    lb_safe = jnp.where(lb_pos, lb, 0.5)
    rows = [jnp.log(lb_safe), jnp.log1p(-lb_safe), 1.0 - lb, lb_pos.astype(F32), o_gain.astype(F32)]
    rows += [jnp.zeros_like(lb)] * 3
    return jnp.stack(rows).astype(F32)


def kernel(x_prompt, x_sample, cache_k, cache_v, cache_ki, state_hgrn, state_pool, meta_tokens, norm_ffn1, ffn1_wg, ffn1_wu, ffn1_wd, norm_mix, norm_ffn2, ffn2_wg, ffn2_wu, ffn2_wd, ab_w_in, ab_w_out, hgrn_lb_logits, hgrn_out_norm, attn_q_norm, attn_k_norm, pool_w, pool_scale):
    Bp, Sp, D = x_prompt.shape
    Bs, Ts, _ = x_sample.shape
    P = cache_k.shape[2]
    Tp = FRONT_PAD + N_META + Sp
    assert Tp % LANES == 0 and Ts == CHUNK and P % LANES == 0
    tm_p = Tp // 8
    Rs = Bs * Ts

    hp = jnp.concatenate([jnp.zeros((Bp, FRONT_PAD, D), F32),
                          jnp.broadcast_to(meta_tokens.astype(F32)[None], (Bp, N_META, D)), x_prompt], axis=1)
    hs = x_sample.reshape(1, Rs, D)

    row_p = jnp.arange(Tp, dtype=jnp.int32)
    pos_p = row_p - FRONT_PAD
    chunk_p = jnp.where(pos_p < 0, -2, jnp.where(pos_p < N_META, -1, (pos_p - N_META) // CHUNK))
    kchunk_p = jnp.where(pos_p < 0, jnp.int32(NO_CHUNK), chunk_p)
    nkeys_p = (jnp.arange(Tp // LANES, dtype=jnp.int32) + 1) * LANES
    tabs_p = rope_tables(pos_p)

    pos_s = P + jnp.arange(Ts, dtype=jnp.int32)
    qchunk_s = jnp.concatenate([pos_s // CHUNK, jnp.full((QB - Ts,), NO_QUERY, jnp.int32)])
    kchunk_s = jnp.arange(P + Ts, dtype=jnp.int32) // CHUNK
    nkeys_s = jnp.full((1,), P + Ts, jnp.int32)
    tabs_s = [jnp.tile(t, (Bs, 1)) for t in rope_tables(pos_s)]

    lb_soft = jax.nn.softmax(hgrn_lb_logits.astype(F32), axis=0)
    lb_all = jnp.cumsum(lb_soft, axis=0) - lb_soft[0]

    cast = lambda w: w.astype(BF16)
    ffn1_w = [cast_bf16(w) for w in (ffn1_wg, ffn1_wu, ffn1_wd)]
    ffn2_w = [cast_bf16(w) for w in (ffn2_wg, ffn2_wu, ffn2_wd)]
    w_in_all = cast_bf16(ab_w_in, IN_COLS_PAD)
    kp_l, vp_l, kip_l, sp_l, pp_l = [], [], [], [], []
    ks_l, vs_l, kis_l, ss_l, ps_l = [], [], [], [], []
    for l in range(DEPTH):
        j = l // 2
        even = l % 2 == 0
        udt = BF16 if even else F32
        if even:
            hp, up = ffn(hp, norm_ffn1[l], *ffn1_w, l, g2=norm_mix[l], u_dtype=udt, tm=tm_p)
        else:
            hp = ffn(hp, norm_ffn1[l], *ffn1_w, l, tm=tm_p)
        hs, us = ffn(hs, norm_ffn1[l], *ffn1_w, l, g2=norm_mix[l], u_dtype=udt, tm=Rs)
        pool_p = None
        if even:
            w_in = w_in_all[j]
            w_out = cast(ab_w_out[j])
            par = _hgrn_params(lb_all[j], hgrn_out_norm[j])
            za, q, kv, qi, slab = inproj(up, w_in, attn_q_norm[j], attn_k_norm[j], tabs_p, tm_p)
            s0 = jnp.zeros((Bp, A_HEADS, A_DK, A_DK), F32)
            oa, sft = hgrn(za, par, s0, LANES)
            k_new, v_new, ki_new = kv[..., :B_DH], kv[..., B_DH:], slab[..., :IDX_DIM]
            wi = slab[..., IDX_DIM:IDX_DIM + IDX_HEADS]
            ob = dsa(q, qi, wi, k_new, v_new, ki_new, chunk_p, kchunk_p, nkeys_p)
            mix_p = (oa, ob, w_out)
            kp_l.append(k_new[:, FRONT_PAD:]); vp_l.append(v_new[:, FRONT_PAD:]); kip_l.append(ki_new[:, FRONT_PAD:])
            sp_l.append(jnp.swapaxes(sft, 2, 3))
            za, q, kv, qi, slab = inproj(us, w_in, attn_q_norm[j], attn_k_norm[j], tabs_s, Rs)
            s0 = jnp.swapaxes(state_hgrn[j].astype(F32), 2, 3)
            oa, sft = hgrn(za.reshape(Bs, Ts, A_COLS), par, s0, Ts)
            kv, slab = kv.reshape(Bs, Ts, -1), slab.reshape(Bs, Ts, -1)
            k_new, v_new, ki_new = kv[..., :B_DH], kv[..., B_DH:], slab[..., :IDX_DIM]
            wi = slab[..., IDX_DIM:IDX_DIM + IDX_HEADS]
            padq = lambda a: jnp.pad(a.reshape(Bs, Ts, -1), ((0, 0), (0, QB - Ts), (0, 0)))
            cat = lambda past, new: jnp.concatenate([past.astype(F32), new], axis=1)
            ob = dsa(padq(q), padq(qi), padq(wi), cat(cache_k[j], k_new), cat(cache_v[j], v_new),
                     cat(cache_ki[j], ki_new), qchunk_s, kchunk_s, nkeys_s)
            mix_s = (oa.reshape(1, Rs, A_VW), ob[:, :Ts].reshape(1, Rs, B_QW), w_out)
            ks_l.append(k_new); vs_l.append(v_new); kis_l.append(ki_new)
            ss_l.append(jnp.swapaxes(sft, 2, 3).astype(state_hgrn.dtype))
        else:
            pw = cast(pool_w[j])
            pool_p = (norm_mix[l], pw, pool_scale[j], -FRONT_PAD)
            us3 = us.reshape(Bs, Ts, D)
            hist = state_pool[j].astype(F32)
            halo = jnp.pad(hist, ((0, 0), (HALO - POOL_HIST, 0), (0, 0)))
            hs = pool(hs.reshape(Bs, Ts, D), us3, halo, pw, pool_scale[j], Ts, P, False).reshape(1, Rs, D)
            ps_l.append(jnp.concatenate([hist, us3], axis=1)[:, -POOL_HIST:])
            mix_p = mix_s = None
        hp = ffn(hp, norm_ffn2[l], *ffn2_w, l, tm=tm_p, mix=mix_p, pool=pool_p)
        if pool_p is not None:
            hp, hist_p = hp
            pp_l.append(hist_p[:, HALO - POOL_HIST:])
        hs = ffn(hs, norm_ffn2[l], *ffn2_w, l, tm=Rs, mix=mix_s)

    y_prompt = hp[:, FRONT_PAD + N_META:]
    return (y_prompt, hs.reshape(Bs, Ts, D),
            jnp.stack(kp_l), jnp.stack(vp_l), jnp.stack(kip_l), jnp.stack(sp_l), jnp.stack(pp_l),
            jnp.stack(ks_l), jnp.stack(vs_l), jnp.stack(kis_l), jnp.stack(ss_l), jnp.stack(ps_l))
```

```python
import functools

import numpy as np
import jax
import jax.numpy as jnp
from jax import lax
from jax.experimental import pallas as pl
from jax.experimental.pallas import tpu as pltpu

F32 = jnp.float32
BF16 = jnp.bfloat16

D_MODEL = 1024
D_FF = 2816
DEPTH = 4
CHUNK = 64
N_META = 16
A_HEADS = 4
A_DK = 128
A_VW = 512
B_HEADS = 4
B_DH = 128
B_QW = 512
IDX_HEADS = 4
IDX_DIM = 64
TOPK = 256
ROPE_THETA = 500000.0
ROT_FRAC = 4
POOL_WINDOWS = (2, 4, 8, 16)
POOL_GROUP = D_MODEL // len(POOL_WINDOWS)
POOL_HIST = max(POOL_WINDOWS) - 1
RMS_EPS = 1e-6
NEG_BIG = -1e30

LANES = 128
SUBLANES = 8
BF16_ROWS = 16
MXU_N = 256
FFN_CHUNKS = 2
FFN_ROW_TILES = 2
CAST_ROWS = 256
KT = 256
QB = 256
NO_CHUNK = 2 ** 30
NO_QUERY = -2
NEVER = 3e38
LOG2E = 1.4426950408889634
HALO = 16
FRONT_PAD = 2 * CHUNK - N_META
IN_COLS_PAD = 3200
A_COLS = 2048
VMEM_LIMIT = 56 * 1024 * 1024
HI_MASK = -65536
MIN_NORMAL_BITS = 0x00800000


def _cparams(sem):
    return pltpu.CompilerParams(dimension_semantics=sem, vmem_limit_bytes=VMEM_LIMIT)


def _sigmoid(x):
    return 1.0 / (1.0 + jnp.exp(-x))


def _rms(x, g):
    return x * lax.rsqrt(jnp.mean(x * x, axis=-1, keepdims=True) + RMS_EPS) * g


def _ffn_kernel(*refs, row_tiles, ff_chunks, emit_u, mix, pool_pos0):
    refs = list(refs)
    x_ref = refs.pop(0)
    if mix:
        oa_ref, ob_ref, wo_ref = refs.pop(0), refs.pop(0), refs.pop(0)
    if pool_pos0 is not None:
        halo_ref, gm_ref, pw_ref, psc_ref = (refs.pop(0) for _ in range(4))
        hist_ref = refs.pop()
    g_ref, wg_ref, wu_ref, wd_ref = refs[:4]
    if emit_u:
        g2_ref, y_ref, u_ref = refs[4:]
    else:
        (y_ref,) = refs[4:]
    if pool_pos0 is not None:
        i = pl.program_id(1)
        tm = x_ref.shape[0]
        u_all = _rms(x_ref[...], gm_ref[...])
        halo0 = _rms(jnp.where(i > 0, halo_ref[...], 0.0), gm_ref[...])
        hist_ref[...] = u_all[tm - HALO:tm, :]
    for r0, r1 in row_tiles:
        x = x_ref[r0:r1, :]
        if pool_pos0 is not None:
            halo = halo0 if r0 == 0 else u_all[r0 - HALO:r0, :]
            pos = pool_pos0 + i * tm + r0 + lax.broadcasted_iota(jnp.int32, (r1 - r0, 1), 0)
            x = x + _pool_mix(u_all[r0:r1, :], halo, pw_ref, psc_ref[...], pos)
        if mix:
            x = x + jnp.dot(oa_ref[r0:r1, :], wo_ref[0:A_VW, :], preferred_element_type=F32)
            x = x + jnp.dot(ob_ref[r0:r1, :], wo_ref[A_VW:A_VW + B_QW, :], preferred_element_type=F32)
        xn = _rms(x, g_ref[...]).astype(BF16)
        acc = None
        for c0, c1 in ff_chunks:
            gate = jnp.dot(xn, wg_ref[:, c0:c1], preferred_element_type=F32)
            up = jnp.dot(xn, wu_ref[:, c0:c1], preferred_element_type=F32)
            act = (gate * _sigmoid(gate) * up).astype(BF16)
            part = jnp.dot(act, wd_ref[c0:c1, :], preferred_element_type=F32)
            acc = part if acc is None else acc + part
        y = x + 0.5 * acc
        y_ref[r0:r1, :] = y
        if emit_u:
            u_ref[r0:r1, :] = _rms(y, g2_ref[...]).astype(u_ref.dtype)


def _ffn_tiling(tm, F):
    edges = sorted({min(tm, -(-(k * tm) // (FFN_ROW_TILES * BF16_ROWS)) * BF16_ROWS) for k in range(FFN_ROW_TILES + 1)})
    row_tiles = list(zip(edges[:-1], edges[1:]))
    n_col = -(-F // MXU_N)
    per = -(-n_col // FFN_CHUNKS)
    edges = [min(c * per * MXU_N, F) for c in range(FFN_CHUNKS + 1)]
    return row_tiles, [(a, b) for a, b in zip(edges[:-1], edges[1:]) if b > a]


def ffn(x, g, wg, wu, wd, layer, g2=None, u_dtype=BF16, tm=528, mix=None, pool=None):
    G, R, D = x.shape
    F = wg.shape[2]
    tm = min(tm, R)
    assert R % tm == 0
    row_tiles, ff_chunks = _ffn_tiling(tm, F)
    emit_u = g2 is not None
    row = lambda c: pl.BlockSpec((None, tm, c), lambda b, i: (b, i, 0))
    vec = pl.BlockSpec((1, D), lambda b, i: (0, 0))
    resident = lambda shape: pl.BlockSpec(shape, lambda b, i: (0,) * len(shape), pipeline_mode=pl.Buffered(1))
    in_specs, args = [row(D)], [x]
    if mix is not None:
        oa, ob, w_out = mix
        in_specs += [row(A_VW), row(B_QW), resident(w_out.shape)]
        args += [oa, ob, w_out]
    if pool is not None:
        g_mix, w_grp, scale, pos0 = pool
        assert tm % HALO == 0
        step = tm // HALO
        in_specs += [pl.BlockSpec((None, HALO, D), lambda b, i: (b, jnp.maximum(i * step - 1, 0), 0)),
                     vec, resident(w_grp.shape), vec]
        args += [x, g_mix.reshape(1, D), w_grp, scale.reshape(1, D)]
    stacked = lambda r, c: pl.BlockSpec((None, r, c), lambda b, i: (layer, 0, 0), pipeline_mode=pl.Buffered(1))
    in_specs += [vec, stacked(D, F), stacked(D, F), stacked(F, D)]
    args += [g.reshape(1, D), wg, wu, wd]
    row = row(D)
    out_shape = [jax.ShapeDtypeStruct((G, R, D), F32)]
    out_specs = [row]
    if emit_u:
        in_specs.append(vec)
        args.append(g2.reshape(1, D))
        out_shape.append(jax.ShapeDtypeStruct((G, R, D), u_dtype))
        out_specs.append(row)
    if pool is not None:
        out_shape.append(jax.ShapeDtypeStruct((G, HALO, D), F32))
        out_specs.append(pl.BlockSpec((None, HALO, D), lambda b, i: (b, 0, 0)))
    out = pl.pallas_call(
        functools.partial(_ffn_kernel, row_tiles=row_tiles, ff_chunks=ff_chunks, emit_u=emit_u,
                          mix=mix is not None, pool_pos0=None if pool is None else pool[3]),
        grid=(G, R // tm),
        in_specs=in_specs,
        out_specs=out_specs,
        out_shape=out_shape,
        compiler_params=_cparams(("parallel", "arbitrary" if pool is not None else "parallel")),
        name="ffn",
    )(*args)
    return tuple(out) if len(out) > 1 else out[0]


def _rope(x, c, s_lo, s_hi, half):
    return x * c + pltpu.roll(x, half, 1) * s_lo + pltpu.roll(x, LANES - half, 1) * s_hi


def _inproj_kernel(u_ref, w_ref, qg_ref, kg_ref, cb_ref, sbl_ref, sbh_ref, ci_ref, sil_ref, sih_ref,
                   za_ref, q_ref, kv_ref, qi_ref, slab_ref, kvb_ref):
    u = u_ref[...]
    hb = B_DH // ROT_FRAC // 2
    hi = IDX_DIM // ROT_FRAC // 2
    cb, sbl, sbh = cb_ref[...], sbl_ref[...], sbh_ref[...]
    ci, sil, sih = ci_ref[...], sil_ref[...], sih_ref[...]
    zq = jnp.dot(u, w_ref[:, A_COLS:A_COLS + B_QW], preferred_element_type=F32)
    for h in range(B_HEADS):
        xh = _rms(zq[:, h * B_DH:(h + 1) * B_DH], qg_ref[...])
        q_ref[:, h * B_DH:(h + 1) * B_DH] = _rope(xh, cb, sbl, sbh, hb).astype(q_ref.dtype)
    c0 = A_COLS + B_QW
    zkv = jnp.dot(u, w_ref[:, c0:c0 + 2 * B_DH], preferred_element_type=F32)
    k_new = _rope(_rms(zkv[:, 0:B_DH], kg_ref[...]), cb, sbl, sbh, hb)
    kv_ref[:, 0:B_DH] = k_new
    kv_ref[:, B_DH:2 * B_DH] = zkv[:, B_DH:2 * B_DH]
    kvb_ref[:, 0:B_DH] = k_new.astype(BF16)
    kvb_ref[:, B_DH:2 * B_DH] = zkv[:, B_DH:2 * B_DH].astype(BF16)
    c1 = c0 + 2 * B_DH
    zi = jnp.dot(u, w_ref[:, c1:c1 + IDX_HEADS * IDX_DIM + LANES], preferred_element_type=F32)
    for t in range(IDX_HEADS * IDX_DIM // LANES):
        qi_ref[:, t * LANES:(t + 1) * LANES] = _rope(zi[:, t * LANES:(t + 1) * LANES], ci, sil, sih, hi).astype(qi_ref.dtype)
    tail = zi[:, IDX_HEADS * IDX_DIM:]
    lane = lax.broadcasted_iota(jnp.int32, tail.shape, 1)
    slab = jnp.where(lane < IDX_DIM, _rope(tail, ci, sil, sih, hi), tail)
    slab_ref[...] = slab
    kvb_ref[:, 2 * B_DH:2 * B_DH + LANES] = slab.astype(BF16)
    za_ref[...] = jnp.dot(u, w_ref[:, 0:A_COLS], preferred_element_type=F32)


def inproj(u, w_pad, q_gain, k_gain, tabs, tm):
    G, R, D = u.shape
    assert R % tm == 0
    row = lambda c: pl.BlockSpec((None, tm, c), lambda b, i: (b, i, 0))
    tab = pl.BlockSpec((tm, LANES), lambda b, i: (i, 0))
    vec = pl.BlockSpec((1, LANES), lambda b, i: (0, 0))
    return pl.pallas_call(
        _inproj_kernel,
        grid=(G, R // tm),
        in_specs=[row(D), pl.BlockSpec((D, IN_COLS_PAD), lambda b, i: (0, 0)), vec, vec] + [tab] * 6,
        out_specs=[row(A_COLS), row(B_QW), row(2 * B_DH), row(IDX_HEADS * IDX_DIM), row(LANES),
                   row(2 * B_DH + LANES)],
        out_shape=[jax.ShapeDtypeStruct((G, R, A_COLS), F32),
                   jax.ShapeDtypeStruct((G, R, B_QW), BF16),
                   jax.ShapeDtypeStruct((G, R, 2 * B_DH), F32),
                   jax.ShapeDtypeStruct((G, R, IDX_HEADS * IDX_DIM), BF16),
                   jax.ShapeDtypeStruct((G, R, LANES), F32),
                   jax.ShapeDtypeStruct((G, R, 2 * B_DH + LANES), BF16)],
        compiler_params=_cparams(("parallel", "parallel")),
        name="inproj",
    )(u, w_pad, q_gain.reshape(1, LANES), k_gain.reshape(1, LANES), *tabs)


def rope_tables(pos):
    pos = pos.astype(F32)
    out = []
    for d in (B_DH, IDX_DIM):
        half = d // ROT_FRAC // 2
        inv = ROPE_THETA ** (-jnp.arange(half, dtype=F32) / half)
        ang = pos[:, None] * inv[None, :]
        cos, sin = jnp.cos(ang), jnp.sin(ang)
        n = pos.shape[0]
        z = lambda w: jnp.zeros((n, w), F32)
        c = jnp.concatenate([cos, cos, jnp.ones((n, d - 2 * half), F32)], axis=1)
        s_lo = jnp.concatenate([z(half), sin, z(d - 2 * half)], axis=1)
        s_hi = jnp.concatenate([-sin, z(d - half)], axis=1)
        rep = LANES // d
        out += [jnp.tile(c, (1, rep)), jnp.tile(s_lo, (1, rep)), jnp.tile(s_hi, (1, rep))]
    return out


def _hgrn_consts(C):
    levels = []
    m = C // 2
    while m >= 1:
        levels.append(m)
        m //= 2
    t = np.arange(C)[:, None]
    r = np.arange(C)[None, :]
    mats = [(r <= t)]
    lvl = np.full((C, C), -1, np.int32)
    lvl[np.arange(C), np.arange(C)] = len(levels)
    for li, m in enumerate(levels):
        anchor = (t // (2 * m)) * (2 * m) + m - 1
        second = t > anchor
        mats.append(np.where(second, (r > anchor) & (r <= t), (r > t) & (r <= anchor)))
        same = (t // (2 * m)) == (r // (2 * m))
        split = same & ((t % (2 * m)) >= m) & ((r % (2 * m)) < m)
        lvl[split] = li
    dmat = np.concatenate([x.astype(np.float32) for x in mats], axis=0)
    return levels, jnp.asarray(dmat, BF16), jnp.asarray(lvl)


def _nt(a, b):
    return lax.dot_general(a, b, (((1,), (1,)), ((), ())), preferred_element_type=F32)


def _tn(a, b):
    return lax.dot_general(a, b, (((0,), (0,)), ((), ())), preferred_element_type=F32)


def _hgrn_kernel(zf_ref, zn_ref, zv_ref, d_ref, lvl_ref, par_ref, s0_ref, oa_ref, sf_ref,
                 st_ref, qa_ref, ka_ref, g_ref, *, C, levels):
    c = pl.program_id(1)
    dk = A_DK
    W = A_HEADS * dk
    log_lb, log_1m, one_m, lb_pos, o_gain = (par_ref[i:i + 1, :] for i in range(5))

    def stash_gates(z_ref, slot):
        aq = z_ref[:, 0:W]
        af = z_ref[:, W:2 * W]
        e = jnp.exp(-jnp.abs(af))
        ls = jnp.minimum(af, 0.0) - jnp.log1p(e)
        b2 = log_1m + ls
        lae = jnp.maximum(log_lb, b2) + jnp.log1p(jnp.exp(-jnp.abs(log_lb - b2)))
        log_f = jnp.where(lb_pos > 0.0, lae, ls)
        ka_ref[slot] = one_m * (jnp.where(af >= 0.0, e, 1.0) / (1.0 + e))
        qa_ref[slot] = aq * _sigmoid(aq)
        g_hi = log_f.astype(BF16)
        r1 = log_f - g_hi.astype(F32)
        g_mid = r1.astype(BF16)
        g_ref[slot, 0] = g_hi
        g_ref[slot, 1] = g_mid
        g_ref[slot, 2] = (r1 - g_mid.astype(F32)).astype(BF16)

    @pl.when(c == 0)
    def _():
        st_ref[...] = s0_ref[...]
        stash_gates(zf_ref, 0)

    cur = c % 2
    qa, ka = qa_ref[cur], ka_ref[cur]
    ai = zv_ref[:, 0:W]
    ag = zv_ref[:, W:2 * W]
    dm = d_ref[...]
    X = (jnp.dot(dm, g_ref[cur, 0], preferred_element_type=F32) + jnp.dot(dm, g_ref[cur, 1], preferred_element_type=F32)
         + jnp.dot(dm, g_ref[cur, 2], preferred_element_type=F32))
    lvl = lvl_ref[...]
    rows = lax.broadcasted_iota(jnp.int32, (C, 1), 0)
    nl = len(levels)
    b = X[0:C, :]
    b_last = b[C - 1:C, :]
    qe = (qa * jnp.exp(b)).astype(BF16)
    ke = (ka * jnp.exp(b_last - b)).astype(BF16)
    decay = jnp.exp(b_last)
    qb, kb, vb = qa.astype(BF16), ka.astype(BF16), ai.astype(BF16)
    zs = [(jnp.where((rows // m) % 2 == 1, qa, ka) * jnp.exp(X[(1 + li) * C:(2 + li) * C, :])).astype(BF16)
          for li, m in enumerate(levels)]
    gate = o_gain * _sigmoid(ag)
    for h in range(A_HEADS):
        sl = slice(h * dk, (h + 1) * dk)
        st = st_ref[h]
        o = _nt(qe[:, sl], st.astype(BF16))
        att = jnp.where(lvl == nl, _nt(qb[:, sl], kb[:, sl]), 0.0)
        for li in range(nl):
            att = att + jnp.where(lvl == li, _nt(zs[li][:, sl], zs[li][:, sl]), 0.0)
        o = o + jnp.dot(att.astype(BF16), vb[:, sl], preferred_element_type=F32)
        st_ref[h] = st * decay[:, sl] + _tn(vb[:, sl], ke[:, sl])
        on = o * lax.rsqrt(jnp.mean(o * o, axis=-1, keepdims=True) + RMS_EPS)
        oa_ref[:, sl] = (on * gate[:, sl]).astype(oa_ref.dtype)

    stash_gates(zn_ref, 1 - cur)

    @pl.when(c == pl.num_programs(1) - 1)
    def _():
        sf_ref[...] = st_ref[...]


def hgrn(za, par, s0t, C):
    B, T, _ = za.shape
    assert T % C == 0
    levels, dmat, lvl = _hgrn_consts(C)
    W = A_HEADS * A_DK
    st_spec = pl.BlockSpec((None, A_HEADS, A_DK, A_DK), lambda b, c: (b, 0, 0, 0))
    nc = T // C
    half = lambda index: pl.BlockSpec((None, C, 2 * W), index)
    return pl.pallas_call(
        functools.partial(_hgrn_kernel, C=C, levels=levels),
        grid=(B, nc),
        in_specs=[half(lambda b, c: (b, 0, 0)),
                  half(lambda b, c: (b, jnp.minimum(c + 1, nc - 1), 0)),
                  half(lambda b, c: (b, c, 1)),
                  pl.BlockSpec(dmat.shape, lambda b, c: (0, 0)),
                  pl.BlockSpec((C, C), lambda b, c: (0, 0)),
                  pl.BlockSpec((8, W), lambda b, c: (0, 0)),
                  st_spec],
        out_specs=[pl.BlockSpec((None, C, W), lambda b, c: (b, c, 0)), st_spec],
        out_shape=[jax.ShapeDtypeStruct((B, T, W), BF16),
                   jax.ShapeDtypeStruct((B, A_HEADS, A_DK, A_DK), F32)],
        scratch_shapes=[pltpu.VMEM((A_HEADS, A_DK, A_DK), F32),
                        pltpu.VMEM((2, C, W), F32), pltpu.VMEM((2, C, W), F32), pltpu.VMEM((2, 3, C, W), BF16)],
        compiler_params=_cparams(("parallel", "arbitrary")),
        name="hgrn",
    )(za, za, za, dmat, lvl, par, s0t)


def _tree_sum(parts):
    while len(parts) > 1:
        parts = [parts[r] + parts[r + 1] if r + 1 < len(parts) else parts[r] for r in range(0, len(parts), 2)]
    return parts[0]


def _fori_pairs(n, tile, init):
    acc = lax.fori_loop(0, n // 2, lambda p, a: a + (tile(2 * p) + tile(2 * p + 1)), init)
    return lax.fori_loop(n // 2 * 2, n, lambda j, a: a + tile(j), acc)


def _dsa_kernel(nkb_ref,qt_ref, qit_ref, wit_ref, qc_ref, kc_ref, k_ref, vt_ref, ki_ref, tri_ref,
                ob_ref, key_ref, hi_ref, acc_ref, *, topk):
    i = pl.program_id(1)
    nk = nkb_ref[i]
    qc = qc_ref[...]
    qit = qit_ref[...]
    rhs_i = jnp.concatenate([qit[h * IDX_DIM:(h + 1) * IDX_DIM, :] for h in range(IDX_HEADS)], axis=1)
    qt = qt_ref[...]
    rhs_q = jnp.concatenate([qt[h * B_DH:(h + 1) * B_DH, :] for h in range(B_HEADS)], axis=1)
    wsc = wit_ref[...] * (IDX_HEADS ** -0.5 * IDX_DIM ** -0.5)

    def score_tile(j):
        sc = jnp.dot(ki_ref[j], rhs_i, preferred_element_type=F32)
        s = jnp.maximum(sc[:, 0:QB], 0.0) * wsc[0:1, :]
        for h in range(1, IDX_HEADS):
            s = s + jnp.maximum(sc[:, h * QB:(h + 1) * QB], 0.0) * wsc[h:h + 1, :]
        s = jnp.where(kc_ref[j] <= qc, s, NEG_BIG)
        bits = lax.bitcast_convert_type(s, jnp.int32)
        bits = jnp.where((bits & 0x7FFFFFFF) < MIN_NORMAL_BITS, 0, bits)
        key_ref[j] = bits ^ ((bits >> 31) & 0x7FFFFFFF)
        hi_ref[j] = lax.bitcast_convert_type(bits & HI_MASK, F32).astype(BF16)

    @pl.loop(0, nk // 2)
    def _(p):
        score_tile(2 * p)
        score_tile(2 * p + 1)

    @pl.loop(nk // 2 * 2, nk)
    def _(j):
        score_tile(j)

    def count_hi(cand16):
        cand16 = jnp.where((cand16 >= 1) & (cand16 < MIN_NORMAL_BITS >> 16), MIN_NORMAL_BITS >> 16, cand16)
        pat = cand16 ^ ((cand16 >> 15) & 0x7FFF)
        cand = lax.bitcast_convert_type(jnp.left_shift(pat, 16), F32).astype(BF16)

        def tile(j):
            hit = jnp.where(hi_ref[j] >= cand, jnp.bfloat16(1), jnp.bfloat16(0))
            parts = [hit[r * BF16_ROWS:(r + 1) * BF16_ROWS, :] for r in range(KT // BF16_ROWS)]
            return _tree_sum(parts).astype(F32)
        acc = _fori_pairs(nk, tile, jnp.zeros((BF16_ROWS, QB), F32))
        return jnp.sum(acc, axis=0, keepdims=True).astype(jnp.int32)

    def count(pred):
        def tile(j):
            hit = jnp.where(pred(key_ref[j]), 1, 0)
            return _tree_sum([hit[r * SUBLANES:(r + 1) * SUBLANES, :] for r in range(KT // SUBLANES)])
        acc = _fori_pairs(nk, tile, jnp.zeros((SUBLANES, QB), jnp.int32))
        return jnp.sum(acc, axis=0, keepdims=True)

    c0 = count_hi(jnp.zeros((1, QB), jnp.int32))
    hi0 = jnp.where(c0 >= topk, 0, -(2 ** 15)).astype(jnp.int32)
    above0 = jnp.where(c0 >= topk, 0, c0)

    def hi_body(it, carry):
        thr, above = carry
        cand = thr | jnp.left_shift(jnp.int32(1), 14 - it)
        cnt = count_hi(cand)
        ok = cnt >= topk
        return jnp.where(ok, cand, thr), jnp.where(ok, above, cnt)

    hi16, above1 = lax.fori_loop(0, 15, hi_body, (hi0, above0))

    def lo_body(it, carry):
        thr, above = carry
        cand = thr | jnp.left_shift(jnp.int32(1), 15 - it)
        cnt = count(lambda kv: kv >= cand)
        ok = cnt >= topk
        return jnp.where(ok, cand, thr), jnp.where(ok, above, cnt)

    thr, above = lax.fori_loop(0, 16, lo_body, (jnp.left_shift(hi16, 16), above1))
    need = (topk - above).astype(F32)

    acc_ref[...] = jnp.zeros_like(acc_ref)
    c2 = B_DH ** -0.5 * LOG2E
    tri = tri_ref[...]

    def att_step(tiles, carry):
        ms, ls, tie_seen = carry
        sels, s_alls = [], []
        for j in tiles:
            kv = key_ref[j]
            eq = kv == thr
            eqf = jnp.where(eq, 1.0, 0.0)
            before = jnp.dot(tri, eqf.astype(BF16), preferred_element_type=F32) + tie_seen
            rank = jnp.where(kv > thr, -1.0, jnp.where(eq, before, NEVER))
            sels.append(jnp.where(jnp.where(kc_ref[j] <= qc, rank, NEVER) < need, 0.0, NEG_BIG))
            tie_seen = tie_seen + jnp.sum(eqf, axis=0, keepdims=True)
            s_alls.append(jnp.dot(k_ref[j], rhs_q, preferred_element_type=F32))
        new_ms, new_ls, alphas = [], [], []
        ps = [[] for _ in tiles]
        for h in range(B_HEADS):
            ss = [s_all[:, h * QB:(h + 1) * QB] + sel for sel, s_all in zip(sels, s_alls)]
            m_new = ms[h]
            for s in ss:
                m_new = jnp.maximum(m_new, jnp.max(s, axis=0, keepdims=True))
            alpha = jnp.exp2((ms[h] - m_new) * c2)
            l_new = alpha * ls[h]
            for t, s in enumerate(ss):
                p = jnp.exp2((s - m_new) * c2)
                l_new = l_new + jnp.sum(p, axis=0, keepdims=True)
                ps[t].append(p.astype(BF16))
            new_ls.append(l_new)
            new_ms.append(m_new)
            alphas.append(alpha)
        pv = None
        for t, j in enumerate(tiles):
            part = jnp.dot(vt_ref[j], jnp.concatenate(ps[t], axis=1), preferred_element_type=F32)
            pv = part if pv is None else pv + part
        for h in range(B_HEADS):
            acc_ref[h] = alphas[h] * acc_ref[h] + pv[:, h * QB:(h + 1) * QB]
        return tuple(new_ms), tuple(new_ls), tie_seen

    init_m = tuple(jnp.full((1, QB), NEG_BIG, F32) for _ in range(B_HEADS))
    init_l = tuple(jnp.zeros((1, QB), F32) for _ in range(B_HEADS))
    carry = (init_m, init_l, jnp.zeros((1, QB), F32))
    carry = lax.fori_loop(0, nk // 2, lambda p, c: att_step([2 * p, 2 * p + 1], c), carry)
    ms, ls, _ = lax.fori_loop(nk // 2 * 2, nk, lambda j, c: att_step([j], c), carry)
    for h in range(B_HEADS):
        seen = ms[h] > 0.5 * NEG_BIG
        inv = jnp.where(seen, 1.0 / jnp.where(seen, ls[h], 1.0), 0.0)
        ob_ref[:, h * B_DH:(h + 1) * B_DH] = (acc_ref[h] * inv).T.astype(ob_ref.dtype)


def dsa(q, qi, wi, k_all, v_all, ki_all, q_chunk, k_chunk, nkeys, topk=TOPK):
    B, Tq, _ = q.shape
    Tk = k_all.shape[1]
    assert Tq % LANES == 0 and QB % LANES == 0
    nq, nkt = -(-Tq // QB), -(-Tk // KT)
    padq = lambda a: jnp.pad(a, ((0, 0), (0, nq * QB - Tq), (0, 0)))
    padk = lambda a: jnp.pad(a, ((0, 0), (0, nkt * KT - Tk), (0, 0)))
    qt = jnp.swapaxes(padq(q), 1, 2)
    qit = jnp.swapaxes(padq(qi), 1, 2)
    wit = jnp.swapaxes(padq(wi), 1, 2)
    k4 = padk(k_all.astype(BF16)).reshape(B, nkt, KT, B_DH)
    vt4 = jnp.swapaxes(padk(v_all.astype(BF16)).reshape(B, nkt, KT, B_DH), 2, 3)
    ki4 = padk(ki_all.astype(BF16)).reshape(B, nkt, KT, IDX_DIM)
    kc = jnp.pad(k_chunk.astype(jnp.int32), (0, nkt * KT - Tk), constant_values=NO_CHUNK)
    kc = jnp.broadcast_to(kc.reshape(nkt, KT, 1), (nkt, KT, QB))
    qc = jnp.pad(q_chunk.astype(jnp.int32), (0, nq * QB - Tq), constant_values=NO_QUERY).reshape(nq, 1, QB)
    per = QB // LANES
    nkeys = jnp.pad(nkeys.astype(jnp.int32), (0, nq * per - nkeys.shape[0])).reshape(nq, per).max(axis=1)
    nkb = (nkeys + (KT - 1)) // KT
    tri = jnp.asarray(np.tril(np.ones((KT, KT), np.float32), -1), BF16)
    grid_spec = pltpu.PrefetchScalarGridSpec(
        num_scalar_prefetch=1,
        grid=(B, nq),
        in_specs=[pl.BlockSpec((None, B_QW, QB), lambda b, i, n: (b, 0, i)),
                  pl.BlockSpec((None, IDX_HEADS * IDX_DIM, QB), lambda b, i, n: (b, 0, i)),
                  pl.BlockSpec((None, IDX_HEADS, QB), lambda b, i, n: (b, 0, i)),
                  pl.BlockSpec((None, 1, QB), lambda b, i, n: (i, 0, 0)),
                  pl.BlockSpec((nkt, KT, QB), lambda b, i, n: (0, 0, 0)),
                  pl.BlockSpec((None, nkt, KT, B_DH), lambda b, i, n: (b, 0, 0, 0)),
                  pl.BlockSpec((None, nkt, B_DH, KT), lambda b, i, n: (b, 0, 0, 0)),
                  pl.BlockSpec((None, nkt, KT, IDX_DIM), lambda b, i, n: (b, 0, 0, 0)),
                  pl.BlockSpec((KT, KT), lambda b, i, n: (0, 0))],
        out_specs=pl.BlockSpec((None, QB, B_QW), lambda b, i, n: (b, i, 0)),
        scratch_shapes=[pltpu.VMEM((nkt, KT, QB), jnp.int32),
                        pltpu.VMEM((nkt, KT, QB), BF16),
                        pltpu.VMEM((B_HEADS, B_DH, QB), F32)],
    )
    return pl.pallas_call(
        functools.partial(_dsa_kernel, topk=topk),
        grid_spec=grid_spec,
        out_shape=jax.ShapeDtypeStruct((B, Tq, B_QW), BF16),
        compiler_params=_cparams(("parallel", "arbitrary")),
        name="dsa",
    )(nkb, qt, qit, wit, qc, kc, k4, vt4, ki4, tri)


def _cast_kernel(x_ref, o_ref):
    C = x_ref.shape[1]
    o_ref[:, 0:C] = x_ref[...].astype(o_ref.dtype)
    if o_ref.shape[1] > C:
        o_ref[:, C:] = jnp.zeros((o_ref.shape[0], o_ref.shape[1] - C), o_ref.dtype)


def cast_bf16(w, cols=None):
    L, R, C = w.shape
    cols = C if cols is None else cols
    assert R % CAST_ROWS == 0
    spec = lambda c: pl.BlockSpec((None, CAST_ROWS, c), lambda l, i: (l, i, 0))
    return pl.pallas_call(
        _cast_kernel,
        grid=(L, R // CAST_ROWS),
        in_specs=[spec(C)],
        out_specs=spec(cols),
        out_shape=jax.ShapeDtypeStruct((L, R, cols), BF16),
        compiler_params=_cparams(("parallel", "parallel")),
        name="cast",
    )(w)


def _pool_mix(u, halo, w_ref, sc, pos):
    run = jnp.concatenate([halo, u], axis=0)
    outs = []
    for g, w in enumerate(POOL_WINDOWS):
        lo = g * POOL_GROUP
        if g:
            run = run[:, POOL_GROUP:]
        run = run + pltpu.roll(run, w // 2, 0)
        cnt = jnp.clip(pos + 1, 1, w).astype(F32)
        mean = run[HALO:, 0:POOL_GROUP] / cnt
        d = (mean - u[:, lo:lo + POOL_GROUP]).astype(BF16)
        outs.append(jnp.dot(d, w_ref[g], preferred_element_type=F32))
    return jnp.concatenate(outs, axis=1) * sc


def _pool_kernel(h_ref, u_ref, halo_ref, w_ref, sc_ref, y_ref, *, tm, pos0, zero_first_halo):
    i = pl.program_id(1)
    halo = halo_ref[...]
    if zero_first_halo:
        halo = jnp.where(i > 0, halo, 0.0)
    pos = pos0 + i * tm + lax.broadcasted_iota(jnp.int32, (tm, 1), 0)
    y_ref[...] = h_ref[...] + _pool_mix(u_ref[...], halo, w_ref, sc_ref[...], pos)


def pool(h, u, halo_src, w_grp, scale, tm, pos0, zero_first_halo):
    G, R, D = h.shape
    assert R % tm == 0 and tm % HALO == 0
    row = pl.BlockSpec((None, tm, D), lambda b, i: (b, i, 0))
    step = tm // HALO
    halo = pl.BlockSpec((None, HALO, D), lambda b, i: (b, jnp.maximum(i * step - 1, 0), 0))
    return pl.pallas_call(
        functools.partial(_pool_kernel, tm=tm, pos0=pos0, zero_first_halo=zero_first_halo),
        grid=(G, R // tm),
        in_specs=[row, row, halo,
                  pl.BlockSpec((len(POOL_WINDOWS), POOL_GROUP, POOL_GROUP), lambda b, i: (0, 0, 0)),
                  pl.BlockSpec((1, D), lambda b, i: (0, 0))],
        out_specs=row,
        out_shape=jax.ShapeDtypeStruct((G, R, D), F32),
        compiler_params=_cparams(("parallel", "parallel")),
        name="pool",
    )(h, u, halo_src, w_grp, scale.reshape(1, D))


def _hgrn_params(lb, o_gain):
    lb_pos = lb > 0
    lb_safe = jnp.where(lb_pos, lb, 0.5)
    rows = [jnp.log(lb_safe), jnp.log1p(-lb_safe), 1.0 - lb, lb_pos.astype(F32), o_gain.astype(F32)]
    rows += [jnp.zeros_like(lb)] * 3
    return jnp.stack(rows).astype(F32)


def kernel(x_prompt, x_sample, cache_k, cache_v, cache_ki, state_hgrn, state_pool, meta_tokens, norm_ffn1, ffn1_wg, ffn1_wu, ffn1_wd, norm_mix, norm_ffn2, ffn2_wg, ffn2_wu, ffn2_wd, ab_w_in, ab_w_out, hgrn_lb_logits, hgrn_out_norm, attn_q_norm, attn_k_norm, pool_w, pool_scale):
    Bp, Sp, D = x_prompt.shape
    Bs, Ts, _ = x_sample.shape
    P = cache_k.shape[2]
    Tp = FRONT_PAD + N_META + Sp
    assert Tp % LANES == 0 and Ts == CHUNK and P % LANES == 0
    tm_p = Tp // 8
    Rs = Bs * Ts

    hp = jnp.concatenate([jnp.zeros((Bp, FRONT_PAD, D), F32),
                          jnp.broadcast_to(meta_tokens.astype(F32)[None], (Bp, N_META, D)), x_prompt], axis=1)
    hs = x_sample.reshape(1, Rs, D)

    row_p = jnp.arange(Tp, dtype=jnp.int32)
    pos_p = row_p - FRONT_PAD
    chunk_p = jnp.where(pos_p < 0, -2, jnp.where(pos_p < N_META, -1, (pos_p - N_META) // CHUNK))
    kchunk_p = jnp.where(pos_p < 0, jnp.int32(NO_CHUNK), chunk_p)
    nkeys_p = (jnp.arange(Tp // LANES, dtype=jnp.int32) + 1) * LANES
    tabs_p = rope_tables(pos_p)

    pos_s = P + jnp.arange(Ts, dtype=jnp.int32)
    qchunk_s = jnp.concatenate([pos_s // CHUNK, jnp.full((QB - Ts,), NO_QUERY, jnp.int32)])
    kchunk_s = jnp.arange(P + Ts, dtype=jnp.int32) // CHUNK
    nkeys_s = jnp.full((1,), P + Ts, jnp.int32)
    tabs_s = [jnp.tile(t, (Bs, 1)) for t in rope_tables(pos_s)]

    lb_soft = jax.nn.softmax(hgrn_lb_logits.astype(F32), axis=0)
    lb_all = jnp.cumsum(lb_soft, axis=0) - lb_soft[0]

    cast = lambda w: w.astype(BF16)
    ffn1_w = [cast_bf16(w) for w in (ffn1_wg, ffn1_wu, ffn1_wd)]
    ffn2_w = [cast_bf16(w) for w in (ffn2_wg, ffn2_wu, ffn2_wd)]
    w_in_all = cast_bf16(ab_w_in, IN_COLS_PAD)
    kp_l, vp_l, kip_l, sp_l, pp_l = [], [], [], [], []
    ks_l, vs_l, kis_l, ss_l, ps_l = [], [], [], [], []
    for l in range(DEPTH):
        j = l // 2
        even = l % 2 == 0
        udt = BF16 if even else F32
        if even:
            hp, up = ffn(hp, norm_ffn1[l], *ffn1_w, l, g2=norm_mix[l], u_dtype=udt, tm=tm_p)
        else:
            hp = ffn(hp, norm_ffn1[l], *ffn1_w, l, tm=tm_p)
        hs, us = ffn(hs, norm_ffn1[l], *ffn1_w, l, g2=norm_mix[l], u_dtype=udt, tm=Rs)
        pool_p = None
        if even:
            w_in = w_in_all[j]
            w_out = cast(ab_w_out[j])
            par = _hgrn_params(lb_all[j], hgrn_out_norm[j])
            za, q, kv, qi, slab, kvb = inproj(up, w_in, attn_q_norm[j], attn_k_norm[j], tabs_p, tm_p)
            s0 = jnp.zeros((Bp, A_HEADS, A_DK, A_DK), F32)
            oa, sft = hgrn(za, par, s0, LANES)
            k_new, v_new, ki_new = kv[..., :B_DH], kv[..., B_DH:], slab[..., :IDX_DIM]
            wi = slab[..., IDX_DIM:IDX_DIM + IDX_HEADS]
            split_b = lambda a: (a[..., :B_DH], a[..., B_DH:2 * B_DH], a[..., 2 * B_DH:2 * B_DH + IDX_DIM])
            ob = dsa(q, qi, wi, *split_b(kvb), chunk_p, kchunk_p, nkeys_p)
            mix_p = (oa, ob, w_out)
            kp_l.append(k_new[:, FRONT_PAD:]); vp_l.append(v_new[:, FRONT_PAD:]); kip_l.append(ki_new[:, FRONT_PAD:])
            sp_l.append(jnp.swapaxes(sft, 2, 3))
            za, q, kv, qi, slab, kvb = inproj(us, w_in, attn_q_norm[j], attn_k_norm[j], tabs_s, Rs)
            s0 = jnp.swapaxes(state_hgrn[j].astype(F32), 2, 3)
            oa, sft = hgrn(za.reshape(Bs, Ts, A_COLS), par, s0, Ts)
            kv, slab = kv.reshape(Bs, Ts, -1), slab.reshape(Bs, Ts, -1)
            k_new, v_new, ki_new = kv[..., :B_DH], kv[..., B_DH:], slab[..., :IDX_DIM]
            wi = slab[..., IDX_DIM:IDX_DIM + IDX_HEADS]
            padq = lambda a: jnp.pad(a.reshape(Bs, Ts, -1), ((0, 0), (0, QB - Ts), (0, 0)))
            cat = lambda past, new: jnp.concatenate([past.astype(BF16), new], axis=1)
            kb, vb, kib = split_b(kvb.reshape(Bs, Ts, -1))
            ob = dsa(padq(q), padq(qi), padq(wi), cat(cache_k[j], kb), cat(cache_v[j], vb),
                     cat(cache_ki[j], kib), qchunk_s, kchunk_s, nkeys_s)
            mix_s = (oa.reshape(1, Rs, A_VW), ob[:, :Ts].reshape(1, Rs, B_QW), w_out)
            ks_l.append(k_new); vs_l.append(v_new); kis_l.append(ki_new)
            ss_l.append(jnp.swapaxes(sft, 2, 3).astype(state_hgrn.dtype))
        else:
            pw = cast(pool_w[j])
            pool_p = (norm_mix[l], pw, pool_scale[j], -FRONT_PAD)
            us3 = us.reshape(Bs, Ts, D)
            hist = state_pool[j].astype(F32)
            halo = jnp.pad(hist, ((0, 0), (HALO - POOL_HIST, 0), (0, 0)))
            hs = pool(hs.reshape(Bs, Ts, D), us3, halo, pw, pool_scale[j], Ts, P, False).reshape(1, Rs, D)
            ps_l.append(jnp.concatenate([hist, us3], axis=1)[:, -POOL_HIST:])
            mix_p = mix_s = None
        hp = ffn(hp, norm_ffn2[l], *ffn2_w, l, tm=tm_p, mix=mix_p, pool=pool_p)
        if pool_p is not None:
            hp, hist_p = hp
            pp_l.append(hist_p[:, HALO - POOL_HIST:])
        hs = ffn(hs, norm_ffn2[l], *ffn2_w, l, tm=Rs, mix=mix_s)

    y_prompt = hp[:, FRONT_PAD + N_META:]
    return (y_prompt, hs.reshape(Bs, Ts, D),
            jnp.stack(kp_l), jnp.stack(vp_l), jnp.stack(kip_l), jnp.stack(sp_l), jnp.stack(pp_l),
            jnp.stack(ks_l), jnp.stack(vs_l), jnp.stack(kis_l), jnp.stack(ss_l), jnp.stack(ps_l))
```

```python
import functools

import numpy as np
import jax
import jax.numpy as jnp
from jax import lax
from jax.experimental import pallas as pl
from jax.experimental.pallas import tpu as pltpu

F32 = jnp.float32
BF16 = jnp.bfloat16

D_MODEL = 1024
D_FF = 2816
DEPTH = 4
CHUNK = 64
N_META = 16
A_HEADS = 4
A_DK = 128
A_VW = 512
B_HEADS = 4
B_DH = 128
B_QW = 512
IDX_HEADS = 4
IDX_DIM = 64
TOPK = 256
ROPE_THETA = 500000.0
ROT_FRAC = 4
POOL_WINDOWS = (2, 4, 8, 16)
POOL_GROUP = D_MODEL // len(POOL_WINDOWS)
POOL_HIST = max(POOL_WINDOWS) - 1
RMS_EPS = 1e-6
NEG_BIG = -1e30

LANES = 128
SUBLANES = 8
BF16_ROWS = 16
MXU_N = 256
FFN_CHUNKS = 2
FFN_ROW_TILES = 2
CAST_ROWS = 256
KT = 256
QB = 256
NO_CHUNK = 2 ** 30
NO_QUERY = -2
V_ROWS = B_DH + BF16_ROWS
NEVER = 3e38
LOG2E = 1.4426950408889634
HALO = 16
FRONT_PAD = 2 * CHUNK - N_META
IN_COLS_PAD = 3200
A_COLS = 2048
VMEM_LIMIT = 56 * 1024 * 1024
HI_MASK = -65536
MIN_NORMAL_BITS = 0x00800000


def _cparams(sem):
    return pltpu.CompilerParams(dimension_semantics=sem, vmem_limit_bytes=VMEM_LIMIT)


def _sigmoid(x):
    return 1.0 / (1.0 + jnp.exp(-x))


def _rms(x, g):
    return x * lax.rsqrt(jnp.mean(x * x, axis=-1, keepdims=True) + RMS_EPS) * g


def _ffn_kernel(*refs, row_tiles, ff_chunks, emit_u, mix, pool_pos0):
    refs = list(refs)
    x_ref = refs.pop(0)
    if mix:
        oa_ref, ob_ref, wo_ref = refs.pop(0), refs.pop(0), refs.pop(0)
    if pool_pos0 is not None:
        halo_ref, gm_ref, pw_ref, psc_ref = (refs.pop(0) for _ in range(4))
        hist_ref = refs.pop()
    g_ref, wg_ref, wu_ref, wd_ref = refs[:4]
    if emit_u:
        g2_ref, y_ref, u_ref = refs[4:]
    else:
        (y_ref,) = refs[4:]
    if pool_pos0 is not None:
        i = pl.program_id(1)
        tm = x_ref.shape[0]
        u_all = _rms(x_ref[...], gm_ref[...])
        halo0 = _rms(jnp.where(i > 0, halo_ref[...], 0.0), gm_ref[...])
        hist_ref[...] = u_all[tm - HALO:tm, :]
    for r0, r1 in row_tiles:
        x = x_ref[r0:r1, :]
        if pool_pos0 is not None:
            halo = halo0 if r0 == 0 else u_all[r0 - HALO:r0, :]
            pos = pool_pos0 + i * tm + r0 + lax.broadcasted_iota(jnp.int32, (r1 - r0, 1), 0)
            x = x + _pool_mix(u_all[r0:r1, :], halo, pw_ref, psc_ref[...], pos)
        if mix:
            x = x + jnp.dot(oa_ref[r0:r1, :], wo_ref[0:A_VW, :], preferred_element_type=F32)
            x = x + jnp.dot(ob_ref[r0:r1, :], wo_ref[A_VW:A_VW + B_QW, :], preferred_element_type=F32)
        xn = _rms(x, g_ref[...]).astype(BF16)
        acc = None
        for c0, c1 in ff_chunks:
            gate = jnp.dot(xn, wg_ref[:, c0:c1], preferred_element_type=F32)
            up = jnp.dot(xn, wu_ref[:, c0:c1], preferred_element_type=F32)
            act = (gate * _sigmoid(gate) * up).astype(BF16)
            part = jnp.dot(act, wd_ref[c0:c1, :], preferred_element_type=F32)
            acc = part if acc is None else acc + part
        y = x + 0.5 * acc
        y_ref[r0:r1, :] = y
        if emit_u:
            u_ref[r0:r1, :] = _rms(y, g2_ref[...]).astype(u_ref.dtype)


def _ffn_tiling(tm, F):
    edges = sorted({min(tm, -(-(k * tm) // (FFN_ROW_TILES * BF16_ROWS)) * BF16_ROWS) for k in range(FFN_ROW_TILES + 1)})
    row_tiles = list(zip(edges[:-1], edges[1:]))
    n_col = -(-F // MXU_N)
    per = -(-n_col // FFN_CHUNKS)
    edges = [min(c * per * MXU_N, F) for c in range(FFN_CHUNKS + 1)]
    return row_tiles, [(a, b) for a, b in zip(edges[:-1], edges[1:]) if b > a]


def ffn(x, g, wg, wu, wd, layer, g2=None, u_dtype=BF16, tm=528, mix=None, pool=None):
    G, R, D = x.shape
    F = wg.shape[2]
    tm = min(tm, R)
    assert R % tm == 0
    row_tiles, ff_chunks = _ffn_tiling(tm, F)
    emit_u = g2 is not None
    row = lambda c: pl.BlockSpec((None, tm, c), lambda b, i: (b, i, 0))
    vec = pl.BlockSpec((1, D), lambda b, i: (0, 0))
    resident = lambda shape: pl.BlockSpec(shape, lambda b, i: (0,) * len(shape), pipeline_mode=pl.Buffered(1))
    in_specs, args = [row(D)], [x]
    if mix is not None:
        oa, ob, w_out = mix
        in_specs += [row(A_VW), row(B_QW), resident(w_out.shape)]
        args += [oa, ob, w_out]
    if pool is not None:
        g_mix, w_grp, scale, pos0 = pool
        assert tm % HALO == 0
        step = tm // HALO
        in_specs += [pl.BlockSpec((None, HALO, D), lambda b, i: (b, jnp.maximum(i * step - 1, 0), 0)),
                     vec, resident(w_grp.shape), vec]
        args += [x, g_mix.reshape(1, D), w_grp, scale.reshape(1, D)]
    stacked = lambda r, c: pl.BlockSpec((None, r, c), lambda b, i: (layer, 0, 0), pipeline_mode=pl.Buffered(1))
    in_specs += [vec, stacked(D, F), stacked(D, F), stacked(F, D)]
    args += [g.reshape(1, D), wg, wu, wd]
    row = row(D)
    out_shape = [jax.ShapeDtypeStruct((G, R, D), F32)]
    out_specs = [row]
    if emit_u:
        in_specs.append(vec)
        args.append(g2.reshape(1, D))
        out_shape.append(jax.ShapeDtypeStruct((G, R, D), u_dtype))
        out_specs.append(row)
    if pool is not None:
        out_shape.append(jax.ShapeDtypeStruct((G, HALO, D), F32))
        out_specs.append(pl.BlockSpec((None, HALO, D), lambda b, i: (b, 0, 0)))
    out = pl.pallas_call(
        functools.partial(_ffn_kernel, row_tiles=row_tiles, ff_chunks=ff_chunks, emit_u=emit_u,
                          mix=mix is not None, pool_pos0=None if pool is None else pool[3]),
        grid=(G, R // tm),
        in_specs=in_specs,
        out_specs=out_specs,
        out_shape=out_shape,
        compiler_params=_cparams(("parallel", "arbitrary" if pool is not None else "parallel")),
        name="ffn",
    )(*args)
    return tuple(out) if len(out) > 1 else out[0]


def _rope(x, c, s_lo, s_hi, half):
    return x * c + pltpu.roll(x, half, 1) * s_lo + pltpu.roll(x, LANES - half, 1) * s_hi


def _inproj_kernel(u_ref, w_ref, qg_ref, kg_ref, cb_ref, sbl_ref, sbh_ref, ci_ref, sil_ref, sih_ref,
                   za_ref, q_ref, kv_ref, qi_ref, slab_ref, kvb_ref):
    u = u_ref[...]
    hb = B_DH // ROT_FRAC // 2
    hi = IDX_DIM // ROT_FRAC // 2
    cb, sbl, sbh = cb_ref[...], sbl_ref[...], sbh_ref[...]
    ci, sil, sih = ci_ref[...], sil_ref[...], sih_ref[...]
    zq = jnp.dot(u, w_ref[:, A_COLS:A_COLS + B_QW], preferred_element_type=F32)
    for h in range(B_HEADS):
        xh = _rms(zq[:, h * B_DH:(h + 1) * B_DH], qg_ref[...])
        q_ref[:, h * B_DH:(h + 1) * B_DH] = _rope(xh, cb, sbl, sbh, hb).astype(q_ref.dtype)
    c0 = A_COLS + B_QW
    zkv = jnp.dot(u, w_ref[:, c0:c0 + 2 * B_DH], preferred_element_type=F32)
    k_new = _rope(_rms(zkv[:, 0:B_DH], kg_ref[...]), cb, sbl, sbh, hb)
    kv_ref[:, 0:B_DH] = k_new
    kv_ref[:, B_DH:2 * B_DH] = zkv[:, B_DH:2 * B_DH]
    kvb_ref[:, 0:B_DH] = k_new.astype(BF16)
    kvb_ref[:, B_DH:2 * B_DH] = zkv[:, B_DH:2 * B_DH].astype(BF16)
    c1 = c0 + 2 * B_DH
    zi = jnp.dot(u, w_ref[:, c1:c1 + IDX_HEADS * IDX_DIM + LANES], preferred_element_type=F32)
    for t in range(IDX_HEADS * IDX_DIM // LANES):
        qi_ref[:, t * LANES:(t + 1) * LANES] = _rope(zi[:, t * LANES:(t + 1) * LANES], ci, sil, sih, hi).astype(qi_ref.dtype)
    tail = zi[:, IDX_HEADS * IDX_DIM:]
    lane = lax.broadcasted_iota(jnp.int32, tail.shape, 1)
    slab = jnp.where(lane < IDX_DIM, _rope(tail, ci, sil, sih, hi), tail)
    slab_ref[...] = slab
    kvb_ref[:, 2 * B_DH:2 * B_DH + LANES] = slab.astype(BF16)
    za_ref[...] = jnp.dot(u, w_ref[:, 0:A_COLS], preferred_element_type=F32)


def inproj(u, w_pad, q_gain, k_gain, tabs, tm):
    G, R, D = u.shape
    assert R % tm == 0
    row = lambda c: pl.BlockSpec((None, tm, c), lambda b, i: (b, i, 0))
    tab = pl.BlockSpec((tm, LANES), lambda b, i: (i, 0))
    vec = pl.BlockSpec((1, LANES), lambda b, i: (0, 0))
    return pl.pallas_call(
        _inproj_kernel,
        grid=(G, R // tm),
        in_specs=[row(D), pl.BlockSpec((D, IN_COLS_PAD), lambda b, i: (0, 0)), vec, vec] + [tab] * 6,
        out_specs=[row(A_COLS), row(B_QW), row(2 * B_DH), row(IDX_HEADS * IDX_DIM), row(LANES),
                   row(2 * B_DH + LANES)],
        out_shape=[jax.ShapeDtypeStruct((G, R, A_COLS), F32),
                   jax.ShapeDtypeStruct((G, R, B_QW), BF16),
                   jax.ShapeDtypeStruct((G, R, 2 * B_DH), F32),
                   jax.ShapeDtypeStruct((G, R, IDX_HEADS * IDX_DIM), BF16),
                   jax.ShapeDtypeStruct((G, R, LANES), F32),
                   jax.ShapeDtypeStruct((G, R, 2 * B_DH + LANES), BF16)],
        compiler_params=_cparams(("parallel", "parallel")),
        name="inproj",
    )(u, w_pad, q_gain.reshape(1, LANES), k_gain.reshape(1, LANES), *tabs)


def rope_tables(pos):
    pos = pos.astype(F32)
    out = []
    for d in (B_DH, IDX_DIM):
        half = d // ROT_FRAC // 2
        inv = ROPE_THETA ** (-jnp.arange(half, dtype=F32) / half)
        ang = pos[:, None] * inv[None, :]
        cos, sin = jnp.cos(ang), jnp.sin(ang)
        n = pos.shape[0]
        z = lambda w: jnp.zeros((n, w), F32)
        c = jnp.concatenate([cos, cos, jnp.ones((n, d - 2 * half), F32)], axis=1)
        s_lo = jnp.concatenate([z(half), sin, z(d - 2 * half)], axis=1)
        s_hi = jnp.concatenate([-sin, z(d - half)], axis=1)
        rep = LANES // d
        out += [jnp.tile(c, (1, rep)), jnp.tile(s_lo, (1, rep)), jnp.tile(s_hi, (1, rep))]
    return out


def _hgrn_consts(C):
    levels = []
    m = C // 2
    while m >= 1:
        levels.append(m)
        m //= 2
    t = np.arange(C)[:, None]
    r = np.arange(C)[None, :]
    mats = [(r <= t)]
    lvl = np.full((C, C), -1, np.int32)
    lvl[np.arange(C), np.arange(C)] = len(levels)
    for li, m in enumerate(levels):
        anchor = (t // (2 * m)) * (2 * m) + m - 1
        second = t > anchor
        mats.append(np.where(second, (r > anchor) & (r <= t), (r > t) & (r <= anchor)))
        same = (t // (2 * m)) == (r // (2 * m))
        split = same & ((t % (2 * m)) >= m) & ((r % (2 * m)) < m)
        lvl[split] = li
    dmat = np.concatenate([x.astype(np.float32) for x in mats], axis=0)
    return levels, jnp.asarray(dmat, BF16), jnp.asarray(lvl)


def _nt(a, b):
    return lax.dot_general(a, b, (((1,), (1,)), ((), ())), preferred_element_type=F32)


def _tn(a, b):
    return lax.dot_general(a, b, (((0,), (0,)), ((), ())), preferred_element_type=F32)


def _hgrn_kernel(zf_ref, zn_ref, zv_ref, d_ref, lvl_ref, par_ref, s0_ref, oa_ref, sf_ref,
                 st_ref, qa_ref, ka_ref, g_ref, *, C, levels):
    c = pl.program_id(1)
    dk = A_DK
    W = A_HEADS * dk
    log_lb, log_1m, one_m, lb_pos, o_gain = (par_ref[i:i + 1, :] for i in range(5))

    def stash_gates(z_ref, slot):
        aq = z_ref[:, 0:W]
        af = z_ref[:, W:2 * W]
        e = jnp.exp(-jnp.abs(af))
        ls = jnp.minimum(af, 0.0) - jnp.log1p(e)
        b2 = log_1m + ls
        lae = jnp.maximum(log_lb, b2) + jnp.log1p(jnp.exp(-jnp.abs(log_lb - b2)))
        log_f = jnp.where(lb_pos > 0.0, lae, ls)
        ka_ref[slot] = one_m * (jnp.where(af >= 0.0, e, 1.0) / (1.0 + e))
        qa_ref[slot] = aq * _sigmoid(aq)
        g_hi = log_f.astype(BF16)
        r1 = log_f - g_hi.astype(F32)
        g_mid = r1.astype(BF16)
        g_ref[slot, 0] = g_hi
        g_ref[slot, 1] = g_mid
        g_ref[slot, 2] = (r1 - g_mid.astype(F32)).astype(BF16)

    @pl.when(c == 0)
    def _():
        st_ref[...] = s0_ref[...]
        stash_gates(zf_ref, 0)

    cur = c % 2
    qa, ka = qa_ref[cur], ka_ref[cur]
    ai = zv_ref[:, 0:W]
    ag = zv_ref[:, W:2 * W]
    dm = d_ref[...]
    X = (jnp.dot(dm, g_ref[cur, 0], preferred_element_type=F32) + jnp.dot(dm, g_ref[cur, 1], preferred_element_type=F32)
         + jnp.dot(dm, g_ref[cur, 2], preferred_element_type=F32))
    lvl = lvl_ref[...]
    rows = lax.broadcasted_iota(jnp.int32, (C, 1), 0)
    nl = len(levels)
    b = X[0:C, :]
    b_last = b[C - 1:C, :]
    qe = (qa * jnp.exp(b)).astype(BF16)
    ke = (ka * jnp.exp(b_last - b)).astype(BF16)
    decay = jnp.exp(b_last)
    qb, kb, vb = qa.astype(BF16), ka.astype(BF16), ai.astype(BF16)
    zs = [(jnp.where((rows // m) % 2 == 1, qa, ka) * jnp.exp(X[(1 + li) * C:(2 + li) * C, :])).astype(BF16)
          for li, m in enumerate(levels)]
    gate = o_gain * _sigmoid(ag)
    for h in range(A_HEADS):
        sl = slice(h * dk, (h + 1) * dk)
        st = st_ref[h]
        o = _nt(qe[:, sl], st.astype(BF16))
        att = jnp.where(lvl == nl, _nt(qb[:, sl], kb[:, sl]), 0.0)
        for li in range(nl):
            att = att + jnp.where(lvl == li, _nt(zs[li][:, sl], zs[li][:, sl]), 0.0)
        o = o + jnp.dot(att.astype(BF16), vb[:, sl], preferred_element_type=F32)
        st_ref[h] = st * decay[:, sl] + _tn(vb[:, sl], ke[:, sl])
        on = o * lax.rsqrt(jnp.mean(o * o, axis=-1, keepdims=True) + RMS_EPS)
        oa_ref[:, sl] = (on * gate[:, sl]).astype(oa_ref.dtype)

    stash_gates(zn_ref, 1 - cur)

    @pl.when(c == pl.num_programs(1) - 1)
    def _():
        sf_ref[...] = st_ref[...]


def hgrn(za, par, s0t, C):
    B, T, _ = za.shape
    assert T % C == 0
    levels, dmat, lvl = _hgrn_consts(C)
    W = A_HEADS * A_DK
    st_spec = pl.BlockSpec((None, A_HEADS, A_DK, A_DK), lambda b, c: (b, 0, 0, 0))
    nc = T // C
    half = lambda index: pl.BlockSpec((None, C, 2 * W), index)
    return pl.pallas_call(
        functools.partial(_hgrn_kernel, C=C, levels=levels),
        grid=(B, nc),
        in_specs=[half(lambda b, c: (b, 0, 0)),
                  half(lambda b, c: (b, jnp.minimum(c + 1, nc - 1), 0)),
                  half(lambda b, c: (b, c, 1)),
                  pl.BlockSpec(dmat.shape, lambda b, c: (0, 0)),
                  pl.BlockSpec((C, C), lambda b, c: (0, 0)),
                  pl.BlockSpec((8, W), lambda b, c: (0, 0)),
                  st_spec],
        out_specs=[pl.BlockSpec((None, C, W), lambda b, c: (b, c, 0)), st_spec],
        out_shape=[jax.ShapeDtypeStruct((B, T, W), BF16),
                   jax.ShapeDtypeStruct((B, A_HEADS, A_DK, A_DK), F32)],
        scratch_shapes=[pltpu.VMEM((A_HEADS, A_DK, A_DK), F32),
                        pltpu.VMEM((2, C, W), F32), pltpu.VMEM((2, C, W), F32), pltpu.VMEM((2, 3, C, W), BF16)],
        compiler_params=_cparams(("parallel", "arbitrary")),
        name="hgrn",
    )(za, za, za, dmat, lvl, par, s0t)


def _tree_sum(parts):
    while len(parts) > 1:
        parts = [parts[r] + parts[r + 1] if r + 1 < len(parts) else parts[r] for r in range(0, len(parts), 2)]
    return parts[0]


def _fori_pairs(n, tile, init):
    acc = lax.fori_loop(0, n // 2, lambda p, a: a + (tile(2 * p) + tile(2 * p + 1)), init)
    return lax.fori_loop(n // 2 * 2, n, lambda j, a: a + tile(j), acc)


def _dsa_kernel(nkb_ref,qt_ref, qit_ref, wit_ref, qc_ref, kc_ref, k_ref, vt_ref, ki_ref, tri_ref,
                ob_ref, key_ref, hi_ref, acc_ref, *, topk):
    i = pl.program_id(1)
    nk = nkb_ref[i]
    qc = qc_ref[...]
    qit = qit_ref[...]
    rhs_i = jnp.concatenate([qit[h * IDX_DIM:(h + 1) * IDX_DIM, :] for h in range(IDX_HEADS)], axis=1)
    qt = qt_ref[...]
    rhs_q = jnp.concatenate([qt[h * B_DH:(h + 1) * B_DH, :] for h in range(B_HEADS)], axis=1)
    wsc = wit_ref[...] * (IDX_HEADS ** -0.5 * IDX_DIM ** -0.5)

    def score_tile(j):
        sc = jnp.dot(ki_ref[j], rhs_i, preferred_element_type=F32)
        s = jnp.maximum(sc[:, 0:QB], 0.0) * wsc[0:1, :]
        for h in range(1, IDX_HEADS):
            s = s + jnp.maximum(sc[:, h * QB:(h + 1) * QB], 0.0) * wsc[h:h + 1, :]
        s = jnp.where(kc_ref[j] <= qc, s, NEG_BIG)
        bits = lax.bitcast_convert_type(s, jnp.int32)
        bits = jnp.where((bits & 0x7FFFFFFF) < MIN_NORMAL_BITS, 0, bits)
        key_ref[j] = bits ^ ((bits >> 31) & 0x7FFFFFFF)
        hi_ref[j] = lax.bitcast_convert_type(bits & HI_MASK, F32).astype(BF16)

    @pl.loop(0, nk // 2)
    def _(p):
        score_tile(2 * p)
        score_tile(2 * p + 1)

    @pl.loop(nk // 2 * 2, nk)
    def _(j):
        score_tile(j)

    def count_hi(cand16):
        cand16 = jnp.where((cand16 >= 1) & (cand16 < MIN_NORMAL_BITS >> 16), MIN_NORMAL_BITS >> 16, cand16)
        pat = cand16 ^ ((cand16 >> 15) & 0x7FFF)
        cand = lax.bitcast_convert_type(jnp.left_shift(pat, 16), F32).astype(BF16)

        def tile(j):
            hit = jnp.where(hi_ref[j] >= cand, jnp.bfloat16(1), jnp.bfloat16(0))
            parts = [hit[r * BF16_ROWS:(r + 1) * BF16_ROWS, :] for r in range(KT // BF16_ROWS)]
            return _tree_sum(parts).astype(F32)
        acc = _fori_pairs(nk, tile, jnp.zeros((BF16_ROWS, QB), F32))
        return jnp.sum(acc, axis=0, keepdims=True).astype(jnp.int32)

    def count(pred):
        def tile(j):
            hit = jnp.where(pred(key_ref[j]), 1, 0)
            return _tree_sum([hit[r * SUBLANES:(r + 1) * SUBLANES, :] for r in range(KT // SUBLANES)])
        acc = _fori_pairs(nk, tile, jnp.zeros((SUBLANES, QB), jnp.int32))
        return jnp.sum(acc, axis=0, keepdims=True)

    c0 = count_hi(jnp.zeros((1, QB), jnp.int32))
    hi0 = jnp.where(c0 >= topk, 0, -(2 ** 15)).astype(jnp.int32)
    above0 = jnp.where(c0 >= topk, 0, c0)

    def hi_body(it, carry):
        thr, above = carry
        cand = thr | jnp.left_shift(jnp.int32(1), 14 - it)
        cnt = count_hi(cand)
        ok = cnt >= topk
        return jnp.where(ok, cand, thr), jnp.where(ok, above, cnt)

    hi16, above1 = lax.fori_loop(0, 15, hi_body, (hi0, above0))

    def lo_body(it, carry):
        thr, above = carry
        cand = thr | jnp.left_shift(jnp.int32(1), 15 - it)
        cnt = count(lambda kv: kv >= cand)
        ok = cnt >= topk
        return jnp.where(ok, cand, thr), jnp.where(ok, above, cnt)

    thr, above = lax.fori_loop(0, 16, lo_body, (jnp.left_shift(hi16, 16), above1))
    need = (topk - above).astype(F32)

    acc_ref[...] = jnp.zeros_like(acc_ref)
    c2 = B_DH ** -0.5 * LOG2E
    tri = tri_ref[...]

    def att_step(tiles, carry):
        ms, tie_seen = carry
        sels, s_alls = [], []
        for j in tiles:
            kv = key_ref[j]
            eq = kv == thr
            eqf = jnp.where(eq, 1.0, 0.0)
            before = jnp.dot(tri, eqf.astype(BF16), preferred_element_type=F32) + tie_seen
            rank = jnp.where(kv > thr, -1.0, jnp.where(eq, before, NEVER))
            sels.append(jnp.where(jnp.where(kc_ref[j] <= qc, rank, NEVER) < need, 0.0, NEG_BIG))
            tie_seen = tie_seen + jnp.sum(eqf, axis=0, keepdims=True)
            s_alls.append(jnp.dot(k_ref[j], rhs_q, preferred_element_type=F32))
        new_ms, alphas = [], []
        ps = [[] for _ in tiles]
        for h in range(B_HEADS):
            ss = [s_all[:, h * QB:(h + 1) * QB] + sel for sel, s_all in zip(sels, s_alls)]
            m_new = ms[h]
            for s in ss:
                m_new = jnp.maximum(m_new, jnp.max(s, axis=0, keepdims=True))
            alphas.append(jnp.exp2((ms[h] - m_new) * c2))
            new_ms.append(m_new)
            for t, s in enumerate(ss):
                ps[t].append(jnp.exp2((s - m_new) * c2).astype(BF16))
        pv = None
        for t, j in enumerate(tiles):
            part = jnp.dot(vt_ref[j], jnp.concatenate(ps[t], axis=1), preferred_element_type=F32)
            pv = part if pv is None else pv + part
        for h in range(B_HEADS):
            acc_ref[h] = alphas[h] * acc_ref[h] + pv[:, h * QB:(h + 1) * QB]
        return tuple(new_ms), tie_seen

    init_m = tuple(jnp.full((1, QB), NEG_BIG, F32) for _ in range(B_HEADS))
    carry = (init_m, jnp.zeros((1, QB), F32))
    carry = lax.fori_loop(0, nk // 2, lambda p, c: att_step([2 * p, 2 * p + 1], c), carry)
    ms, _ = lax.fori_loop(nk // 2 * 2, nk, lambda j, c: att_step([j], c), carry)
    for h in range(B_HEADS):
        seen = ms[h] > 0.5 * NEG_BIG
        inv = jnp.where(seen, 1.0 / jnp.where(seen, acc_ref[h, B_DH:B_DH + 1, :], 1.0), 0.0)
        ob_ref[:, h * B_DH:(h + 1) * B_DH] = (acc_ref[h, 0:B_DH, :] * inv).T.astype(ob_ref.dtype)


def dsa(q, qi, wi, k_all, v_all, ki_all, q_chunk, k_chunk, nkeys, topk=TOPK):
    B, Tq, _ = q.shape
    Tk = k_all.shape[1]
    assert Tq % LANES == 0 and QB % LANES == 0
    nq, nkt = -(-Tq // QB), -(-Tk // KT)
    padq = lambda a: jnp.pad(a, ((0, 0), (0, nq * QB - Tq), (0, 0)))
    padk = lambda a: jnp.pad(a, ((0, 0), (0, nkt * KT - Tk), (0, 0)))
    qt = jnp.swapaxes(padq(q), 1, 2)
    qit = jnp.swapaxes(padq(qi), 1, 2)
    wit = jnp.swapaxes(padq(wi), 1, 2)
    k4 = padk(k_all.astype(BF16)).reshape(B, nkt, KT, B_DH)
    vt4 = jnp.swapaxes(padk(v_all.astype(BF16)).reshape(B, nkt, KT, B_DH), 2, 3)
    ones_rows = jnp.zeros((B, nkt, V_ROWS - B_DH, KT), BF16).at[:, :, 0, :].set(1.0)
    vt4 = jnp.concatenate([vt4, ones_rows], axis=2)
    ki4 = padk(ki_all.astype(BF16)).reshape(B, nkt, KT, IDX_DIM)
    kc = jnp.pad(k_chunk.astype(jnp.int32), (0, nkt * KT - Tk), constant_values=NO_CHUNK)
    kc = jnp.broadcast_to(kc.reshape(nkt, KT, 1), (nkt, KT, QB))
    qc = jnp.pad(q_chunk.astype(jnp.int32), (0, nq * QB - Tq), constant_values=NO_QUERY).reshape(nq, 1, QB)
    per = QB // LANES
    nkeys = jnp.pad(nkeys.astype(jnp.int32), (0, nq * per - nkeys.shape[0])).reshape(nq, per).max(axis=1)
    nkb = (nkeys + (KT - 1)) // KT
    tri = jnp.asarray(np.tril(np.ones((KT, KT), np.float32), -1), BF16)
    grid_spec = pltpu.PrefetchScalarGridSpec(
        num_scalar_prefetch=1,
        grid=(B, nq),
        in_specs=[pl.BlockSpec((None, B_QW, QB), lambda b, i, n: (b, 0, i)),
                  pl.BlockSpec((None, IDX_HEADS * IDX_DIM, QB), lambda b, i, n: (b, 0, i)),
                  pl.BlockSpec((None, IDX_HEADS, QB), lambda b, i, n: (b, 0, i)),
                  pl.BlockSpec((None, 1, QB), lambda b, i, n: (i, 0, 0)),
                  pl.BlockSpec((nkt, KT, QB), lambda b, i, n: (0, 0, 0)),
                  pl.BlockSpec((None, nkt, KT, B_DH), lambda b, i, n: (b, 0, 0, 0)),
                  pl.BlockSpec((None, nkt, V_ROWS, KT), lambda b, i, n: (b, 0, 0, 0)),
                  pl.BlockSpec((None, nkt, KT, IDX_DIM), lambda b, i, n: (b, 0, 0, 0)),
                  pl.BlockSpec((KT, KT), lambda b, i, n: (0, 0))],
        out_specs=pl.BlockSpec((None, QB, B_QW), lambda b, i, n: (b, i, 0)),
        scratch_shapes=[pltpu.VMEM((nkt, KT, QB), jnp.int32),
                        pltpu.VMEM((nkt, KT, QB), BF16),
                        pltpu.VMEM((B_HEADS, V_ROWS, QB), F32)],
    )
    return pl.pallas_call(
        functools.partial(_dsa_kernel, topk=topk),
        grid_spec=grid_spec,
        out_shape=jax.ShapeDtypeStruct((B, Tq, B_QW), BF16),
        compiler_params=_cparams(("parallel", "arbitrary")),
        name="dsa",
    )(nkb, qt, qit, wit, qc, kc, k4, vt4, ki4, tri)


def _cast_kernel(x_ref, o_ref):
    C = x_ref.shape[1]
    o_ref[:, 0:C] = x_ref[...].astype(o_ref.dtype)
    if o_ref.shape[1] > C:
        o_ref[:, C:] = jnp.zeros((o_ref.shape[0], o_ref.shape[1] - C), o_ref.dtype)


def cast_bf16(w, cols=None):
    L, R, C = w.shape
    cols = C if cols is None else cols
    assert R % CAST_ROWS == 0
    spec = lambda c: pl.BlockSpec((None, CAST_ROWS, c), lambda l, i: (l, i, 0))
    return pl.pallas_call(
        _cast_kernel,
        grid=(L, R // CAST_ROWS),
        in_specs=[spec(C)],
        out_specs=spec(cols),
        out_shape=jax.ShapeDtypeStruct((L, R, cols), BF16),
        compiler_params=_cparams(("parallel", "parallel")),
        name="cast",
    )(w)


def _pool_mix(u, halo, w_ref, sc, pos):
    run = jnp.concatenate([halo, u], axis=0)
    outs = []
    for g, w in enumerate(POOL_WINDOWS):
        lo = g * POOL_GROUP
        if g:
            run = run[:, POOL_GROUP:]
        run = run + pltpu.roll(run, w // 2, 0)
        cnt = jnp.clip(pos + 1, 1, w).astype(F32)
        mean = run[HALO:, 0:POOL_GROUP] / cnt
        d = (mean - u[:, lo:lo + POOL_GROUP]).astype(BF16)
        outs.append(jnp.dot(d, w_ref[g], preferred_element_type=F32))
    return jnp.concatenate(outs, axis=1) * sc


def _pool_kernel(h_ref, u_ref, halo_ref, w_ref, sc_ref, y_ref, *, tm, pos0, zero_first_halo):
    i = pl.program_id(1)
    halo = halo_ref[...]
    if zero_first_halo:
        halo = jnp.where(i > 0, halo, 0.0)
    pos = pos0 + i * tm + lax.broadcasted_iota(jnp.int32, (tm, 1), 0)
    y_ref[...] = h_ref[...] + _pool_mix(u_ref[...], halo, w_ref, sc_ref[...], pos)


def pool(h, u, halo_src, w_grp, scale, tm, pos0, zero_first_halo):
    G, R, D = h.shape
    assert R % tm == 0 and tm % HALO == 0
    row = pl.BlockSpec((None, tm, D), lambda b, i: (b, i, 0))
    step = tm // HALO
    halo = pl.BlockSpec((None, HALO, D), lambda b, i: (b, jnp.maximum(i * step - 1, 0), 0))
    return pl.pallas_call(
        functools.partial(_pool_kernel, tm=tm, pos0=pos0, zero_first_halo=zero_first_halo),
        grid=(G, R // tm),
        in_specs=[row, row, halo,
                  pl.BlockSpec((len(POOL_WINDOWS), POOL_GROUP, POOL_GROUP), lambda b, i: (0, 0, 0)),
                  pl.BlockSpec((1, D), lambda b, i: (0, 0))],
        out_specs=row,
        out_shape=jax.ShapeDtypeStruct((G, R, D), F32),
        compiler_params=_cparams(("parallel", "parallel")),
        name="pool",
    )(h, u, halo_src, w_grp, scale.reshape(1, D))


def _hgrn_params(lb, o_gain):
    lb_pos = lb > 0
    lb_safe = jnp.where(lb_pos, lb, 0.5)
    rows = [jnp.log(lb_safe), jnp.log1p(-lb_safe), 1.0 - lb, lb_pos.astype(F32), o_gain.astype(F32)]
    rows += [jnp.zeros_like(lb)] * 3
    return jnp.stack(rows).astype(F32)


def kernel(x_prompt, x_sample, cache_k, cache_v, cache_ki, state_hgrn, state_pool, meta_tokens, norm_ffn1, ffn1_wg, ffn1_wu, ffn1_wd, norm_mix, norm_ffn2, ffn2_wg, ffn2_wu, ffn2_wd, ab_w_in, ab_w_out, hgrn_lb_logits, hgrn_out_norm, attn_q_norm, attn_k_norm, pool_w, pool_scale):
    Bp, Sp, D = x_prompt.shape
    Bs, Ts, _ = x_sample.shape
    P = cache_k.shape[2]
    Tp = FRONT_PAD + N_META + Sp
    assert Tp % LANES == 0 and Ts == CHUNK and P % LANES == 0
    tm_p = Tp // 8
    Rs = Bs * Ts

    hp = jnp.concatenate([jnp.zeros((Bp, FRONT_PAD, D), F32),
                          jnp.broadcast_to(meta_tokens.astype(F32)[None], (Bp, N_META, D)), x_prompt], axis=1)
    hs = x_sample.reshape(1, Rs, D)

    row_p = jnp.arange(Tp, dtype=jnp.int32)
    pos_p = row_p - FRONT_PAD
    chunk_p = jnp.where(pos_p < 0, -2, jnp.where(pos_p < N_META, -1, (pos_p - N_META) // CHUNK))
    kchunk_p = jnp.where(pos_p < 0, jnp.int32(NO_CHUNK), chunk_p)
    nkeys_p = (jnp.arange(Tp // LANES, dtype=jnp.int32) + 1) * LANES
    tabs_p = rope_tables(pos_p)

    pos_s = P + jnp.arange(Ts, dtype=jnp.int32)
    qchunk_s = jnp.concatenate([pos_s // CHUNK, jnp.full((QB - Ts,), NO_QUERY, jnp.int32)])
    kchunk_s = jnp.arange(P + Ts, dtype=jnp.int32) // CHUNK
    nkeys_s = jnp.full((1,), P + Ts, jnp.int32)
    tabs_s = [jnp.tile(t, (Bs, 1)) for t in rope_tables(pos_s)]

    lb_soft = jax.nn.softmax(hgrn_lb_logits.astype(F32), axis=0)
    lb_all = jnp.cumsum(lb_soft, axis=0) - lb_soft[0]

    cast = lambda w: w.astype(BF16)
    ffn1_w = [cast_bf16(w) for w in (ffn1_wg, ffn1_wu, ffn1_wd)]
    ffn2_w = [cast_bf16(w) for w in (ffn2_wg, ffn2_wu, ffn2_wd)]
    w_in_all = cast_bf16(ab_w_in, IN_COLS_PAD)
    kp_l, vp_l, kip_l, sp_l, pp_l = [], [], [], [], []
    ks_l, vs_l, kis_l, ss_l, ps_l = [], [], [], [], []
    for l in range(DEPTH):
        j = l // 2
        even = l % 2 == 0
        udt = BF16 if even else F32
        if even:
            hp, up = ffn(hp, norm_ffn1[l], *ffn1_w, l, g2=norm_mix[l], u_dtype=udt, tm=tm_p)
        else:
            hp = ffn(hp, norm_ffn1[l], *ffn1_w, l, tm=tm_p)
        hs, us = ffn(hs, norm_ffn1[l], *ffn1_w, l, g2=norm_mix[l], u_dtype=udt, tm=Rs)
        pool_p = None
        if even:
            w_in = w_in_all[j]
            w_out = cast(ab_w_out[j])
            par = _hgrn_params(lb_all[j], hgrn_out_norm[j])
            za, q, kv, qi, slab, kvb = inproj(up, w_in, attn_q_norm[j], attn_k_norm[j], tabs_p, tm_p)
            s0 = jnp.zeros((Bp, A_HEADS, A_DK, A_DK), F32)
            oa, sft = hgrn(za, par, s0, LANES)
            k_new, v_new, ki_new = kv[..., :B_DH], kv[..., B_DH:], slab[..., :IDX_DIM]
            wi = slab[..., IDX_DIM:IDX_DIM + IDX_HEADS]
            split_b = lambda a: (a[..., :B_DH], a[..., B_DH:2 * B_DH], a[..., 2 * B_DH:2 * B_DH + IDX_DIM])
            ob = dsa(q, qi, wi, *split_b(kvb), chunk_p, kchunk_p, nkeys_p)
            mix_p = (oa, ob, w_out)
            kp_l.append(k_new[:, FRONT_PAD:]); vp_l.append(v_new[:, FRONT_PAD:]); kip_l.append(ki_new[:, FRONT_PAD:])
            sp_l.append(jnp.swapaxes(sft, 2, 3))
            za, q, kv, qi, slab, kvb = inproj(us, w_in, attn_q_norm[j], attn_k_norm[j], tabs_s, Rs)
            s0 = jnp.swapaxes(state_hgrn[j].astype(F32), 2, 3)
            oa, sft = hgrn(za.reshape(Bs, Ts, A_COLS), par, s0, Ts)
            kv, slab = kv.reshape(Bs, Ts, -1), slab.reshape(Bs, Ts, -1)
            k_new, v_new, ki_new = kv[..., :B_DH], kv[..., B_DH:], slab[..., :IDX_DIM]
            wi = slab[..., IDX_DIM:IDX_DIM + IDX_HEADS]
            padq = lambda a: jnp.pad(a.reshape(Bs, Ts, -1), ((0, 0), (0, QB - Ts), (0, 0)))
            cat = lambda past, new: jnp.concatenate([past.astype(BF16), new], axis=1)
            kb, vb, kib = split_b(kvb.reshape(Bs, Ts, -1))
            ob = dsa(padq(q), padq(qi), padq(wi), cat(cache_k[j], kb), cat(cache_v[j], vb),
                     cat(cache_ki[j], kib), qchunk_s, kchunk_s, nkeys_s)
            mix_s = (oa.reshape(1, Rs, A_VW), ob[:, :Ts].reshape(1, Rs, B_QW), w_out)
            ks_l.append(k_new); vs_l.append(v_new); kis_l.append(ki_new)
            ss_l.append(jnp.swapaxes(sft, 2, 3).astype(state_hgrn.dtype))
        else:
            pw = cast(pool_w[j])
            pool_p = (norm_mix[l], pw, pool_scale[j], -FRONT_PAD)
            us3 = us.reshape(Bs, Ts, D)
            hist = state_pool[j].astype(F32)
            halo = jnp.pad(hist, ((0, 0), (HALO - POOL_HIST, 0), (0, 0)))
            hs = pool(hs.reshape(Bs, Ts, D), us3, halo, pw, pool_scale[j], Ts, P, False).reshape(1, Rs, D)
            ps_l.append(jnp.concatenate([hist, us3], axis=1)[:, -POOL_HIST:])
            mix_p = mix_s = None
        hp = ffn(hp, norm_ffn2[l], *ffn2_w, l, tm=tm_p, mix=mix_p, pool=pool_p)
        if pool_p is not None:
            hp, hist_p = hp
            pp_l.append(hist_p[:, HALO - POOL_HIST:])
        hs = ffn(hs, norm_ffn2[l], *ffn2_w, l, tm=Rs, mix=mix_s)

    y_prompt = hp[:, FRONT_PAD + N_META:]
    return (y_prompt, hs.reshape(Bs, Ts, D),
            jnp.stack(kp_l), jnp.stack(vp_l), jnp.stack(kip_l), jnp.stack(sp_l), jnp.stack(pp_l),
            jnp.stack(ks_l), jnp.stack(vs_l), jnp.stack(kis_l), jnp.stack(ss_l), jnp.stack(ps_l))
```

```python
import functools

import numpy as np
import jax
import jax.numpy as jnp
from jax import lax
from jax.experimental import pallas as pl
from jax.experimental.pallas import tpu as pltpu

F32 = jnp.float32
BF16 = jnp.bfloat16

D_MODEL = 1024
D_FF = 2816
DEPTH = 4
CHUNK = 64
N_META = 16
A_HEADS = 4
A_DK = 128
A_VW = 512
B_HEADS = 4
B_DH = 128
B_QW = 512
IDX_HEADS = 4
IDX_DIM = 64
TOPK = 256
ROPE_THETA = 500000.0
ROT_FRAC = 4
POOL_WINDOWS = (2, 4, 8, 16)
POOL_GROUP = D_MODEL // len(POOL_WINDOWS)
POOL_HIST = max(POOL_WINDOWS) - 1
RMS_EPS = 1e-6
NEG_BIG = -1e30

LANES = 128
SUBLANES = 8
BF16_ROWS = 16
MXU_N = 256
FFN_CHUNKS = 2
FFN_ROW_TILES = 2
CAST_ROWS = 256
KT = 256
QB = 256
NO_CHUNK = 2 ** 30
NO_QUERY = -2
V_ROWS = B_DH + BF16_ROWS
NEVER = 3e38
LOG2E = 1.4426950408889634
HALO = 16
FRONT_PAD = 2 * CHUNK - N_META
IN_COLS_PAD = 3200
A_COLS = 2048
VMEM_LIMIT = 56 * 1024 * 1024
HI_MASK = -65536
MIN_NORMAL_BITS = 0x00800000


def _cparams(sem):
    return pltpu.CompilerParams(dimension_semantics=sem, vmem_limit_bytes=VMEM_LIMIT)


def _sigmoid(x):
    return 1.0 / (1.0 + jnp.exp(-x))


def _rms(x, g):
    return x * lax.rsqrt(jnp.mean(x * x, axis=-1, keepdims=True) + RMS_EPS) * g


def _ffn_kernel(*refs, row_tiles, ff_chunks, emit_u, mix, pool_pos0):
    refs = list(refs)
    x_ref = refs.pop(0)
    if mix:
        oa_ref, ob_ref, wo_ref = refs.pop(0), refs.pop(0), refs.pop(0)
    if pool_pos0 is not None:
        halo_ref, gm_ref, pw_ref, psc_ref = (refs.pop(0) for _ in range(4))
        hist_ref = refs.pop()
    g_ref, wg_ref, wu_ref, wd_ref = refs[:4]
    if emit_u:
        g2_ref, y_ref, u_ref = refs[4:]
    else:
        (y_ref,) = refs[4:]
    if pool_pos0 is not None:
        i = pl.program_id(1)
        tm = x_ref.shape[0]
        u_all = _rms(x_ref[...], gm_ref[...])
        halo0 = _rms(jnp.where(i > 0, halo_ref[...], 0.0), gm_ref[...])
        hist_ref[...] = u_all[tm - HALO:tm, :]
    for r0, r1 in row_tiles:
        x = x_ref[r0:r1, :]
        if pool_pos0 is not None:
            halo = halo0 if r0 == 0 else u_all[r0 - HALO:r0, :]
            pos = pool_pos0 + i * tm + r0 + lax.broadcasted_iota(jnp.int32, (r1 - r0, 1), 0)
            x = x + _pool_mix(u_all[r0:r1, :], halo, pw_ref, psc_ref[...], pos)
        if mix:
            x = x + jnp.dot(oa_ref[r0:r1, :], wo_ref[0:A_VW, :], preferred_element_type=F32)
            x = x + jnp.dot(ob_ref[r0:r1, :], wo_ref[A_VW:A_VW + B_QW, :], preferred_element_type=F32)
        xn = _rms(x, g_ref[...]).astype(BF16)
        acc = None
        for c0, c1 in ff_chunks:
            gate = jnp.dot(xn, wg_ref[:, c0:c1], preferred_element_type=F32)
            up = jnp.dot(xn, wu_ref[:, c0:c1], preferred_element_type=F32)
            act = (gate * _sigmoid(gate) * up).astype(BF16)
            part = jnp.dot(act, wd_ref[c0:c1, :], preferred_element_type=F32)
            acc = part if acc is None else acc + part
        y = x + 0.5 * acc
        y_ref[r0:r1, :] = y
        if emit_u:
            u_ref[r0:r1, :] = _rms(y, g2_ref[...]).astype(u_ref.dtype)


def _ffn_tiling(tm, F):
    edges = sorted({min(tm, -(-(k * tm) // (FFN_ROW_TILES * BF16_ROWS)) * BF16_ROWS) for k in range(FFN_ROW_TILES + 1)})
    row_tiles = list(zip(edges[:-1], edges[1:]))
    n_col = -(-F // MXU_N)
    per = -(-n_col // FFN_CHUNKS)
    edges = [min(c * per * MXU_N, F) for c in range(FFN_CHUNKS + 1)]
    return row_tiles, [(a, b) for a, b in zip(edges[:-1], edges[1:]) if b > a]


def ffn(x, g, wg, wu, wd, layer, g2=None, u_dtype=BF16, tm=528, mix=None, pool=None):
    G, R, D = x.shape
    F = wg.shape[2]
    tm = min(tm, R)
    assert R % tm == 0
    row_tiles, ff_chunks = _ffn_tiling(tm, F)
    emit_u = g2 is not None
    row = lambda c: pl.BlockSpec((None, tm, c), lambda b, i: (b, i, 0))
    vec = pl.BlockSpec((1, D), lambda b, i: (0, 0))
    resident = lambda shape: pl.BlockSpec(shape, lambda b, i: (0,) * len(shape), pipeline_mode=pl.Buffered(1))
    in_specs, args = [row(D)], [x]
    if mix is not None:
        oa, ob, w_out = mix
        in_specs += [row(A_VW), row(B_QW), resident(w_out.shape)]
        args += [oa, ob, w_out]
    if pool is not None:
        g_mix, w_grp, scale, pos0 = pool
        assert tm % HALO == 0
        step = tm // HALO
        in_specs += [pl.BlockSpec((None, HALO, D), lambda b, i: (b, jnp.maximum(i * step - 1, 0), 0)),
                     vec, resident(w_grp.shape), vec]
        args += [x, g_mix.reshape(1, D), w_grp, scale.reshape(1, D)]
    stacked = lambda r, c: pl.BlockSpec((None, r, c), lambda b, i: (layer, 0, 0), pipeline_mode=pl.Buffered(1))
    in_specs += [vec, stacked(D, F), stacked(D, F), stacked(F, D)]
    args += [g.reshape(1, D), wg, wu, wd]
    row = row(D)
    out_shape = [jax.ShapeDtypeStruct((G, R, D), F32)]
    out_specs = [row]
    if emit_u:
        in_specs.append(vec)
        args.append(g2.reshape(1, D))
        out_shape.append(jax.ShapeDtypeStruct((G, R, D), u_dtype))
        out_specs.append(row)
    if pool is not None:
        out_shape.append(jax.ShapeDtypeStruct((G, HALO, D), F32))
        out_specs.append(pl.BlockSpec((None, HALO, D), lambda b, i: (b, 0, 0)))
    out = pl.pallas_call(
        functools.partial(_ffn_kernel, row_tiles=row_tiles, ff_chunks=ff_chunks, emit_u=emit_u,
                          mix=mix is not None, pool_pos0=None if pool is None else pool[3]),
        grid=(G, R // tm),
        in_specs=in_specs,
        out_specs=out_specs,
        out_shape=out_shape,
        compiler_params=_cparams(("parallel", "arbitrary" if pool is not None else "parallel")),
        name="ffn",
    )(*args)
    return tuple(out) if len(out) > 1 else out[0]


def _rope(x, c, s_lo, s_hi, half):
    return x * c + pltpu.roll(x, half, 1) * s_lo + pltpu.roll(x, LANES - half, 1) * s_hi


def _inproj_kernel(u_ref, w_ref, qg_ref, kg_ref, cb_ref, sbl_ref, sbh_ref, ci_ref, sil_ref, sih_ref,
                   za_ref, q_ref, kv_ref, qi_ref, slab_ref, kvb_ref):
    u = u_ref[...]
    hb = B_DH // ROT_FRAC // 2
    hi = IDX_DIM // ROT_FRAC // 2
    cb, sbl, sbh = cb_ref[...], sbl_ref[...], sbh_ref[...]
    ci, sil, sih = ci_ref[...], sil_ref[...], sih_ref[...]
    zq = jnp.dot(u, w_ref[:, A_COLS:A_COLS + B_QW], preferred_element_type=F32)
    for h in range(B_HEADS):
        xh = _rms(zq[:, h * B_DH:(h + 1) * B_DH], qg_ref[...])
        q_ref[:, h * B_DH:(h + 1) * B_DH] = _rope(xh, cb, sbl, sbh, hb).astype(q_ref.dtype)
    c0 = A_COLS + B_QW
    zkv = jnp.dot(u, w_ref[:, c0:c0 + 2 * B_DH], preferred_element_type=F32)
    k_new = _rope(_rms(zkv[:, 0:B_DH], kg_ref[...]), cb, sbl, sbh, hb)
    kv_ref[:, 0:B_DH] = k_new
    kv_ref[:, B_DH:2 * B_DH] = zkv[:, B_DH:2 * B_DH]
    kvb_ref[:, 0:B_DH] = k_new.astype(BF16)
    kvb_ref[:, B_DH:2 * B_DH] = zkv[:, B_DH:2 * B_DH].astype(BF16)
    c1 = c0 + 2 * B_DH
    zi = jnp.dot(u, w_ref[:, c1:c1 + IDX_HEADS * IDX_DIM + LANES], preferred_element_type=F32)
    for t in range(IDX_HEADS * IDX_DIM // LANES):
        qi_ref[:, t * LANES:(t + 1) * LANES] = _rope(zi[:, t * LANES:(t + 1) * LANES], ci, sil, sih, hi).astype(qi_ref.dtype)
    tail = zi[:, IDX_HEADS * IDX_DIM:]
    lane = lax.broadcasted_iota(jnp.int32, tail.shape, 1)
    slab = jnp.where(lane < IDX_DIM, _rope(tail, ci, sil, sih, hi), tail)
    slab_ref[...] = slab
    kvb_ref[:, 2 * B_DH:2 * B_DH + LANES] = slab.astype(BF16)
    za_ref[...] = jnp.dot(u, w_ref[:, 0:A_COLS], preferred_element_type=F32)


def inproj(u, w_pad, q_gain, k_gain, tabs, tm):
    G, R, D = u.shape
    assert R % tm == 0
    row = lambda c: pl.BlockSpec((None, tm, c), lambda b, i: (b, i, 0))
    tab = pl.BlockSpec((tm, LANES), lambda b, i: (i, 0))
    vec = pl.BlockSpec((1, LANES), lambda b, i: (0, 0))
    return pl.pallas_call(
        _inproj_kernel,
        grid=(G, R // tm),
        in_specs=[row(D), pl.BlockSpec((D, IN_COLS_PAD), lambda b, i: (0, 0)), vec, vec] + [tab] * 6,
        out_specs=[row(A_COLS), row(B_QW), row(2 * B_DH), row(IDX_HEADS * IDX_DIM), row(LANES),
                   row(2 * B_DH + LANES)],
        out_shape=[jax.ShapeDtypeStruct((G, R, A_COLS), F32),
                   jax.ShapeDtypeStruct((G, R, B_QW), BF16),
                   jax.ShapeDtypeStruct((G, R, 2 * B_DH), F32),
                   jax.ShapeDtypeStruct((G, R, IDX_HEADS * IDX_DIM), BF16),
                   jax.ShapeDtypeStruct((G, R, LANES), F32),
                   jax.ShapeDtypeStruct((G, R, 2 * B_DH + LANES), BF16)],
        compiler_params=_cparams(("parallel", "parallel")),
        name="inproj",
    )(u, w_pad, q_gain.reshape(1, LANES), k_gain.reshape(1, LANES), *tabs)


def rope_tables(pos):
    pos = pos.astype(F32)
    out = []
    for d in (B_DH, IDX_DIM):
        half = d // ROT_FRAC // 2
        inv = ROPE_THETA ** (-jnp.arange(half, dtype=F32) / half)
        ang = pos[:, None] * inv[None, :]
        cos, sin = jnp.cos(ang), jnp.sin(ang)
        n = pos.shape[0]
        z = lambda w: jnp.zeros((n, w), F32)
        c = jnp.concatenate([cos, cos, jnp.ones((n, d - 2 * half), F32)], axis=1)
        s_lo = jnp.concatenate([z(half), sin, z(d - 2 * half)], axis=1)
        s_hi = jnp.concatenate([-sin, z(d - half)], axis=1)
        rep = LANES // d
        out += [jnp.tile(c, (1, rep)), jnp.tile(s_lo, (1, rep)), jnp.tile(s_hi, (1, rep))]
    return out


def _hgrn_consts(C):
    levels = []
    m = C // 2
    while m >= 1:
        levels.append(m)
        m //= 2
    t = np.arange(C)[:, None]
    r = np.arange(C)[None, :]
    mats = [(r <= t)]
    lvl = np.full((C, C), -1, np.int32)
    lvl[np.arange(C), np.arange(C)] = len(levels)
    for li, m in enumerate(levels):
        anchor = (t // (2 * m)) * (2 * m) + m - 1
        second = t > anchor
        mats.append(np.where(second, (r > anchor) & (r <= t), (r > t) & (r <= anchor)))
        same = (t // (2 * m)) == (r // (2 * m))
        split = same & ((t % (2 * m)) >= m) & ((r % (2 * m)) < m)
        lvl[split] = li
    dmat = np.concatenate([x.astype(np.float32) for x in mats], axis=0)
    return levels, jnp.asarray(dmat, BF16), jnp.asarray(lvl)


def _nt(a, b):
    return lax.dot_general(a, b, (((1,), (1,)), ((), ())), preferred_element_type=F32)


def _tn(a, b):
    return lax.dot_general(a, b, (((0,), (0,)), ((), ())), preferred_element_type=F32)


def _hgrn_kernel(zf_ref, zn_ref, zv_ref, d_ref, lvl_ref, par_ref, s0_ref, oa_ref, sf_ref,
                 st_ref, qa_ref, ka_ref, g_ref, *, C, levels):
    c = pl.program_id(1)
    dk = A_DK
    W = A_HEADS * dk
    log_lb, log_1m, one_m, lb_pos, o_gain = (par_ref[i:i + 1, :] for i in range(5))

    def stash_gates(z_ref, slot):
        aq = z_ref[:, 0:W]
        af = z_ref[:, W:2 * W]
        e = jnp.exp(-jnp.abs(af))
        ls = jnp.minimum(af, 0.0) - jnp.log1p(e)
        b2 = log_1m + ls
        lae = jnp.maximum(log_lb, b2) + jnp.log1p(jnp.exp(-jnp.abs(log_lb - b2)))
        log_f = jnp.where(lb_pos > 0.0, lae, ls)
        ka_ref[slot] = one_m * (jnp.where(af >= 0.0, e, 1.0) / (1.0 + e))
        qa_ref[slot] = aq * _sigmoid(aq)
        g_hi = log_f.astype(BF16)
        r1 = log_f - g_hi.astype(F32)
        g_mid = r1.astype(BF16)
        g_ref[slot, 0] = g_hi
        g_ref[slot, 1] = g_mid
        g_ref[slot, 2] = (r1 - g_mid.astype(F32)).astype(BF16)

    @pl.when(c == 0)
    def _():
        st_ref[...] = s0_ref[...]
        stash_gates(zf_ref, 0)

    cur = c % 2
    qa, ka = qa_ref[cur], ka_ref[cur]
    ai = zv_ref[:, 0:W]
    ag = zv_ref[:, W:2 * W]
    dm = d_ref[...]
    X = (jnp.dot(dm, g_ref[cur, 0], preferred_element_type=F32) + jnp.dot(dm, g_ref[cur, 1], preferred_element_type=F32)
         + jnp.dot(dm, g_ref[cur, 2], preferred_element_type=F32))
    lvl = lvl_ref[...]
    rows = lax.broadcasted_iota(jnp.int32, (C, 1), 0)
    nl = len(levels)
    b = X[0:C, :]
    b_last = b[C - 1:C, :]
    qe = (qa * jnp.exp(b)).astype(BF16)
    ke = (ka * jnp.exp(b_last - b)).astype(BF16)
    decay = jnp.exp(b_last)
    qb, kb, vb = qa.astype(BF16), ka.astype(BF16), ai.astype(BF16)
    zs = [(jnp.where((rows // m) % 2 == 1, qa, ka) * jnp.exp(X[(1 + li) * C:(2 + li) * C, :])).astype(BF16)
          for li, m in enumerate(levels)]
    gate = o_gain * _sigmoid(ag)
    for h in range(A_HEADS):
        sl = slice(h * dk, (h + 1) * dk)
        st = st_ref[h]
        o = _nt(qe[:, sl], st.astype(BF16))
        att = jnp.where(lvl == nl, _nt(qb[:, sl], kb[:, sl]), 0.0)
        for li in range(nl):
            att = att + jnp.where(lvl == li, _nt(zs[li][:, sl], zs[li][:, sl]), 0.0)
        o = o + jnp.dot(att.astype(BF16), vb[:, sl], preferred_element_type=F32)
        st_ref[h] = st * decay[:, sl] + _tn(vb[:, sl], ke[:, sl])
        on = o * lax.rsqrt(jnp.mean(o * o, axis=-1, keepdims=True) + RMS_EPS)
        oa_ref[:, sl] = (on * gate[:, sl]).astype(oa_ref.dtype)

    stash_gates(zn_ref, 1 - cur)

    @pl.when(c == pl.num_programs(1) - 1)
    def _():
        sf_ref[...] = st_ref[...]


def hgrn(za, par, s0t, C):
    B, T, _ = za.shape
    assert T % C == 0
    levels, dmat, lvl = _hgrn_consts(C)
    W = A_HEADS * A_DK
    st_spec = pl.BlockSpec((None, A_HEADS, A_DK, A_DK), lambda b, c: (b, 0, 0, 0))
    nc = T // C
    half = lambda index: pl.BlockSpec((None, C, 2 * W), index)
    return pl.pallas_call(
        functools.partial(_hgrn_kernel, C=C, levels=levels),
        grid=(B, nc),
        in_specs=[half(lambda b, c: (b, 0, 0)),
                  half(lambda b, c: (b, jnp.minimum(c + 1, nc - 1), 0)),
                  half(lambda b, c: (b, c, 1)),
                  pl.BlockSpec(dmat.shape, lambda b, c: (0, 0)),
                  pl.BlockSpec((C, C), lambda b, c: (0, 0)),
                  pl.BlockSpec((8, W), lambda b, c: (0, 0)),
                  st_spec],
        out_specs=[pl.BlockSpec((None, C, W), lambda b, c: (b, c, 0)), st_spec],
        out_shape=[jax.ShapeDtypeStruct((B, T, W), BF16),
                   jax.ShapeDtypeStruct((B, A_HEADS, A_DK, A_DK), F32)],
        scratch_shapes=[pltpu.VMEM((A_HEADS, A_DK, A_DK), F32),
                        pltpu.VMEM((2, C, W), F32), pltpu.VMEM((2, C, W), F32), pltpu.VMEM((2, 3, C, W), BF16)],
        compiler_params=_cparams(("parallel", "arbitrary")),
        name="hgrn",
    )(za, za, za, dmat, lvl, par, s0t)


def _tree_sum(parts):
    while len(parts) > 1:
        parts = [parts[r] + parts[r + 1] if r + 1 < len(parts) else parts[r] for r in range(0, len(parts), 2)]
    return parts[0]


def _fori_pairs(n, tile, init):
    acc = lax.fori_loop(0, n // 2, lambda p, a: a + (tile(2 * p) + tile(2 * p + 1)), init)
    return lax.fori_loop(n // 2 * 2, n, lambda j, a: a + tile(j), acc)


def _dsa_kernel(nkb_ref,qt_ref, qit_ref, wit_ref, qc_ref, kc_ref, k_ref, vt_ref, ki_ref, tri_ref,
                ob_ref, key_ref, hi_ref, acc_ref, *, topk):
    i = pl.program_id(1)
    nk = nkb_ref[i]
    qc = qc_ref[...]
    qit = qit_ref[...]
    rhs_i = jnp.concatenate([qit[h * IDX_DIM:(h + 1) * IDX_DIM, :] for h in range(IDX_HEADS)], axis=1)
    qt = qt_ref[...]
    rhs_q = jnp.concatenate([qt[h * B_DH:(h + 1) * B_DH, :] for h in range(B_HEADS)], axis=1)
    wsc = wit_ref[...] * (IDX_HEADS ** -0.5 * IDX_DIM ** -0.5)

    def score_tile(j):
        sc = jnp.dot(ki_ref[j], rhs_i, preferred_element_type=F32)
        s = jnp.maximum(sc[:, 0:QB], 0.0) * wsc[0:1, :]
        for h in range(1, IDX_HEADS):
            s = s + jnp.maximum(sc[:, h * QB:(h + 1) * QB], 0.0) * wsc[h:h + 1, :]
        s = jnp.where(kc_ref[j] <= qc, s, NEG_BIG)
        bits = lax.bitcast_convert_type(s, jnp.int32)
        bits = jnp.where((bits & 0x7FFFFFFF) < MIN_NORMAL_BITS, 0, bits)
        key_ref[j] = bits ^ ((bits >> 31) & 0x7FFFFFFF)
        hi_ref[j] = lax.bitcast_convert_type(bits & HI_MASK, F32).astype(BF16)

    @pl.loop(0, nk // 2)
    def _(p):
        score_tile(2 * p)
        score_tile(2 * p + 1)

    @pl.loop(nk // 2 * 2, nk)
    def _(j):
        score_tile(j)

    def count_hi(cand16):
        cand16 = jnp.where((cand16 >= 1) & (cand16 < MIN_NORMAL_BITS >> 16), MIN_NORMAL_BITS >> 16, cand16)
        pat = cand16 ^ ((cand16 >> 15) & 0x7FFF)
        cand = lax.bitcast_convert_type(jnp.left_shift(pat, 16), F32).astype(BF16)

        def tile(j):
            hit = jnp.where(hi_ref[j] >= cand, jnp.bfloat16(1), jnp.bfloat16(0))
            parts = [hit[r * BF16_ROWS:(r + 1) * BF16_ROWS, :] for r in range(KT // BF16_ROWS)]
            return _tree_sum(parts).astype(F32)
        acc = _fori_pairs(nk, tile, jnp.zeros((BF16_ROWS, QB), F32))
        return jnp.sum(acc, axis=0, keepdims=True).astype(jnp.int32)

    def count(pred):
        def tile(j):
            hit = jnp.where(pred(key_ref[j]), 1, 0)
            return _tree_sum([hit[r * SUBLANES:(r + 1) * SUBLANES, :] for r in range(KT // SUBLANES)])
        acc = _fori_pairs(nk, tile, jnp.zeros((SUBLANES, QB), jnp.int32))
        return jnp.sum(acc, axis=0, keepdims=True)

    c0 = count_hi(jnp.zeros((1, QB), jnp.int32))
    hi0 = jnp.where(c0 >= topk, 0, -(2 ** 15)).astype(jnp.int32)
    above0 = jnp.where(c0 >= topk, 0, c0)

    def hi_body(it, carry):
        thr, above = carry
        cand = thr | jnp.left_shift(jnp.int32(1), 14 - it)
        cnt = count_hi(cand)
        ok = cnt >= topk
        return jnp.where(ok, cand, thr), jnp.where(ok, above, cnt)

    hi16, above1 = lax.fori_loop(0, 15, hi_body, (hi0, above0))

    def lo_body(it, carry):
        thr, above = carry
        cand = thr | jnp.left_shift(jnp.int32(1), 15 - it)
        cnt = count(lambda kv: kv >= cand)
        ok = cnt >= topk
        return jnp.where(ok, cand, thr), jnp.where(ok, above, cnt)

    thr, above = lax.fori_loop(0, 16, lo_body, (jnp.left_shift(hi16, 16), above1))
    need = (topk - above).astype(F32)

    acc_ref[...] = jnp.zeros_like(acc_ref)
    c2 = B_DH ** -0.5 * LOG2E
    tri = tri_ref[...]

    def att_step(tiles, carry):
        ms, tie_seen = carry
        sels, s_alls = [], []
        for j in tiles:
            kv = key_ref[j]
            eq = kv == thr
            eqf = jnp.where(eq, 1.0, 0.0)
            before = jnp.dot(tri, eqf.astype(BF16), preferred_element_type=F32) + tie_seen
            rank = jnp.where(kv > thr, -1.0, jnp.where(eq, before, NEVER))
            sels.append(jnp.where(jnp.where(kc_ref[j] <= qc, rank, NEVER) < need, 0.0, NEG_BIG))
            tie_seen = tie_seen + jnp.sum(eqf, axis=0, keepdims=True)
            s_alls.append(jnp.dot(k_ref[j], rhs_q, preferred_element_type=F32))
        new_ms, alphas = [], []
        ps = [[] for _ in tiles]
        for h in range(B_HEADS):
            ss = [s_all[:, h * QB:(h + 1) * QB] + sel for sel, s_all in zip(sels, s_alls)]
            m_new = ms[h]
            for s in ss:
                m_new = jnp.maximum(m_new, jnp.max(s, axis=0, keepdims=True))
            alphas.append(jnp.exp2((ms[h] - m_new) * c2))
            new_ms.append(m_new)
            for t, s in enumerate(ss):
                ps[t].append(jnp.exp2((s - m_new) * c2).astype(BF16))
        pv = None
        for t, j in enumerate(tiles):
            part = jnp.dot(vt_ref[j], jnp.concatenate(ps[t], axis=1), preferred_element_type=F32)
            pv = part if pv is None else pv + part
        for h in range(B_HEADS):
            acc_ref[h] = alphas[h] * acc_ref[h] + pv[:, h * QB:(h + 1) * QB]
        return tuple(new_ms), tie_seen

    init_m = tuple(jnp.full((1, QB), NEG_BIG, F32) for _ in range(B_HEADS))
    carry = (init_m, jnp.zeros((1, QB), F32))
    carry = lax.fori_loop(0, nk // 2, lambda p, c: att_step([2 * p, 2 * p + 1], c), carry)
    ms, _ = lax.fori_loop(nk // 2 * 2, nk, lambda j, c: att_step([j], c), carry)
    for h in range(B_HEADS):
        seen = ms[h] > 0.5 * NEG_BIG
        inv = jnp.where(seen, 1.0 / jnp.where(seen, acc_ref[h, B_DH:B_DH + 1, :], 1.0), 0.0)
        ob_ref[:, h * B_DH:(h + 1) * B_DH] = (acc_ref[h, 0:B_DH, :] * inv).T.astype(ob_ref.dtype)


def dsa(q, qi, wi, k_all, v_all, ki_all, q_chunk, k_chunk, nkeys, topk=TOPK):
    B, Tq, _ = q.shape
    Tk = k_all.shape[1]
    assert Tq % LANES == 0 and QB % LANES == 0
    nq, nkt = -(-Tq // QB), -(-Tk // KT)
    padq = lambda a: jnp.pad(a, ((0, 0), (0, nq * QB - Tq), (0, 0)))
    padk = lambda a: jnp.pad(a, ((0, 0), (0, nkt * KT - Tk), (0, 0)))
    qt = jnp.swapaxes(padq(q), 1, 2)
    qit = jnp.swapaxes(padq(qi), 1, 2)
    wit = jnp.swapaxes(padq(wi), 1, 2)
    k4 = padk(k_all.astype(BF16)).reshape(B, nkt, KT, B_DH)
    vt4 = jnp.swapaxes(padk(v_all.astype(BF16)).reshape(B, nkt, KT, B_DH), 2, 3)
    ones_rows = jnp.zeros((B, nkt, V_ROWS - B_DH, KT), BF16).at[:, :, 0, :].set(1.0)
    vt4 = jnp.concatenate([vt4, ones_rows], axis=2)
    ki4 = padk(ki_all.astype(BF16)).reshape(B, nkt, KT, IDX_DIM)
    kc = jnp.pad(k_chunk.astype(jnp.int32), (0, nkt * KT - Tk), constant_values=NO_CHUNK)
    kc = jnp.broadcast_to(kc.reshape(nkt, KT, 1), (nkt, KT, QB))
    qc = jnp.pad(q_chunk.astype(jnp.int32), (0, nq * QB - Tq), constant_values=NO_QUERY).reshape(nq, 1, QB)
    per = QB // LANES
    nkeys = jnp.pad(nkeys.astype(jnp.int32), (0, nq * per - nkeys.shape[0])).reshape(nq, per).max(axis=1)
    nkb = (nkeys + (KT - 1)) // KT
    tri = jnp.asarray(np.tril(np.ones((KT, KT), np.float32), -1), BF16)
    grid_spec = pltpu.PrefetchScalarGridSpec(
        num_scalar_prefetch=1,
        grid=(B, nq),
        in_specs=[pl.BlockSpec((None, B_QW, QB), lambda b, i, n: (b, 0, i)),
                  pl.BlockSpec((None, IDX_HEADS * IDX_DIM, QB), lambda b, i, n: (b, 0, i)),
                  pl.BlockSpec((None, IDX_HEADS, QB), lambda b, i, n: (b, 0, i)),
                  pl.BlockSpec((None, 1, QB), lambda b, i, n: (i, 0, 0)),
                  pl.BlockSpec((nkt, KT, QB), lambda b, i, n: (0, 0, 0)),
                  pl.BlockSpec((None, nkt, KT, B_DH), lambda b, i, n: (b, 0, 0, 0)),
                  pl.BlockSpec((None, nkt, V_ROWS, KT), lambda b, i, n: (b, 0, 0, 0)),
                  pl.BlockSpec((None, nkt, KT, IDX_DIM), lambda b, i, n: (b, 0, 0, 0)),
                  pl.BlockSpec((KT, KT), lambda b, i, n: (0, 0))],
        out_specs=pl.BlockSpec((None, QB, B_QW), lambda b, i, n: (b, i, 0)),
        scratch_shapes=[pltpu.VMEM((nkt, KT, QB), jnp.int32),
                        pltpu.VMEM((nkt, KT, QB), BF16),
                        pltpu.VMEM((B_HEADS, V_ROWS, QB), F32)],
    )
    return pl.pallas_call(
        functools.partial(_dsa_kernel, topk=topk),
        grid_spec=grid_spec,
        out_shape=jax.ShapeDtypeStruct((B, Tq, B_QW), BF16),
        compiler_params=_cparams(("parallel", "arbitrary")),
        name="dsa",
    )(nkb, qt, qit, wit, qc, kc, k4, vt4, ki4, tri)


def _cast_kernel(x_ref, o_ref):
    C = x_ref.shape[1]
    o_ref[:, 0:C] = x_ref[...].astype(o_ref.dtype)
    if o_ref.shape[1] > C:
        o_ref[:, C:] = jnp.zeros((o_ref.shape[0], o_ref.shape[1] - C), o_ref.dtype)


def cast_bf16(w, cols=None):
    L, R, C = w.shape
    cols = C if cols is None else cols
    assert R % CAST_ROWS == 0
    spec = lambda c: pl.BlockSpec((None, CAST_ROWS, c), lambda l, i: (l, i, 0))
    return pl.pallas_call(
        _cast_kernel,
        grid=(L, R // CAST_ROWS),
        in_specs=[spec(C)],
        out_specs=spec(cols),
        out_shape=jax.ShapeDtypeStruct((L, R, cols), BF16),
        compiler_params=_cparams(("parallel", "parallel")),
        name="cast",
    )(w)


def _pool_mix(u, halo, w_ref, sc, pos):
    run = jnp.concatenate([halo, u], axis=0)
    outs = []
    for g, w in enumerate(POOL_WINDOWS):
        lo = g * POOL_GROUP
        if g:
            run = run[:, POOL_GROUP:]
        run = run + pltpu.roll(run, w // 2, 0)
        cnt = jnp.clip(pos + 1, 1, w).astype(F32)
        mean = run[HALO:, 0:POOL_GROUP] / cnt
        d = (mean - u[:, lo:lo + POOL_GROUP]).astype(BF16)
        outs.append(jnp.dot(d, w_ref[g], preferred_element_type=F32))
    return jnp.concatenate(outs, axis=1) * sc


def _pool_kernel(h_ref, u_ref, halo_ref, w_ref, sc_ref, y_ref, *, tm, pos0, zero_first_halo):
    i = pl.program_id(1)
    halo = halo_ref[...]
    if zero_first_halo:
        halo = jnp.where(i > 0, halo, 0.0)
    pos = pos0 + i * tm + lax.broadcasted_iota(jnp.int32, (tm, 1), 0)
    y_ref[...] = h_ref[...] + _pool_mix(u_ref[...], halo, w_ref, sc_ref[...], pos)


def pool(h, u, halo_src, w_grp, scale, tm, pos0, zero_first_halo):
    G, R, D = h.shape
    assert R % tm == 0 and tm % HALO == 0
    row = pl.BlockSpec((None, tm, D), lambda b, i: (b, i, 0))
    step = tm // HALO
    halo = pl.BlockSpec((None, HALO, D), lambda b, i: (b, jnp.maximum(i * step - 1, 0), 0))
    return pl.pallas_call(
        functools.partial(_pool_kernel, tm=tm, pos0=pos0, zero_first_halo=zero_first_halo),
        grid=(G, R // tm),
        in_specs=[row, row, halo,
                  pl.BlockSpec((len(POOL_WINDOWS), POOL_GROUP, POOL_GROUP), lambda b, i: (0, 0, 0)),
                  pl.BlockSpec((1, D), lambda b, i: (0, 0))],
        out_specs=row,
        out_shape=jax.ShapeDtypeStruct((G, R, D), F32),
        compiler_params=_cparams(("parallel", "parallel")),
        name="pool",
    )(h, u, halo_src, w_grp, scale.reshape(1, D))


def _hgrn_params(lb, o_gain):
    lb_pos = lb > 0
    lb_safe = jnp.where(lb_pos, lb, 0.5)
    rows = [jnp.log(lb_safe), jnp.log1p(-lb_safe), 1.0 - lb, lb_pos.astype(F32), o_gain.astype(F32)]
    rows += [jnp.zeros_like(lb)] * 3
    return jnp.stack(rows).astype(F32)


def kernel(x_prompt, x_sample, cache_k, cache_v, cache_ki, state_hgrn, state_pool, meta_tokens, norm_ffn1, ffn1_wg, ffn1_wu, ffn1_wd, norm_mix, norm_ffn2, ffn2_wg, ffn2_wu, ffn2_wd, ab_w_in, ab_w_out, hgrn_lb_logits, hgrn_out_norm, attn_q_norm, attn_k_norm, pool_w, pool_scale):
    Bp, Sp, D = x_prompt.shape
    Bs, Ts, _ = x_sample.shape
    P = cache_k.shape[2]
    Tp = FRONT_PAD + N_META + Sp
    assert Tp % LANES == 0 and Ts == CHUNK and P % LANES == 0
    tm_p = Tp // 8
    tm_f = Tp // 4
    Rs = Bs * Ts

    hp = jnp.concatenate([jnp.zeros((Bp, FRONT_PAD, D), F32),
                          jnp.broadcast_to(meta_tokens.astype(F32)[None], (Bp, N_META, D)), x_prompt], axis=1)
    hs = x_sample.reshape(1, Rs, D)

    row_p = jnp.arange(Tp, dtype=jnp.int32)
    pos_p = row_p - FRONT_PAD
    chunk_p = jnp.where(pos_p < 0, -2, jnp.where(pos_p < N_META, -1, (pos_p - N_META) // CHUNK))
    kchunk_p = jnp.where(pos_p < 0, jnp.int32(NO_CHUNK), chunk_p)
    nkeys_p = (jnp.arange(Tp // LANES, dtype=jnp.int32) + 1) * LANES
    tabs_p = rope_tables(pos_p)

    pos_s = P + jnp.arange(Ts, dtype=jnp.int32)
    qchunk_s = jnp.concatenate([pos_s // CHUNK, jnp.full((QB - Ts,), NO_QUERY, jnp.int32)])
    kchunk_s = jnp.arange(P + Ts, dtype=jnp.int32) // CHUNK
    nkeys_s = jnp.full((1,), P + Ts, jnp.int32)
    tabs_s = [jnp.tile(t, (Bs, 1)) for t in rope_tables(pos_s)]

    lb_soft = jax.nn.softmax(hgrn_lb_logits.astype(F32), axis=0)
    lb_all = jnp.cumsum(lb_soft, axis=0) - lb_soft[0]

    cast = lambda w: w.astype(BF16)
    ffn1_w = [cast_bf16(w) for w in (ffn1_wg, ffn1_wu, ffn1_wd)]
    ffn2_w = [cast_bf16(w) for w in (ffn2_wg, ffn2_wu, ffn2_wd)]
    w_in_all = cast_bf16(ab_w_in, IN_COLS_PAD)
    kp_l, vp_l, kip_l, sp_l, pp_l = [], [], [], [], []
    ks_l, vs_l, kis_l, ss_l, ps_l = [], [], [], [], []
    for l in range(DEPTH):
        j = l // 2
        even = l % 2 == 0
        udt = BF16 if even else F32
        if even:
            hp, up = ffn(hp, norm_ffn1[l], *ffn1_w, l, g2=norm_mix[l], u_dtype=udt, tm=tm_f)
        else:
            hp = ffn(hp, norm_ffn1[l], *ffn1_w, l, tm=tm_f)
        hs, us = ffn(hs, norm_ffn1[l], *ffn1_w, l, g2=norm_mix[l], u_dtype=udt, tm=Rs)
        pool_p = None
        if even:
            w_in = w_in_all[j]
            w_out = cast(ab_w_out[j])
            par = _hgrn_params(lb_all[j], hgrn_out_norm[j])
            za, q, kv, qi, slab, kvb = inproj(up, w_in, attn_q_norm[j], attn_k_norm[j], tabs_p, tm_p)
            s0 = jnp.zeros((Bp, A_HEADS, A_DK, A_DK), F32)
            oa, sft = hgrn(za, par, s0, LANES)
            k_new, v_new, ki_new = kv[..., :B_DH], kv[..., B_DH:], slab[..., :IDX_DIM]
            wi = slab[..., IDX_DIM:IDX_DIM + IDX_HEADS]
            split_b = lambda a: (a[..., :B_DH], a[..., B_DH:2 * B_DH], a[..., 2 * B_DH:2 * B_DH + IDX_DIM])
            ob = dsa(q, qi, wi, *split_b(kvb), chunk_p, kchunk_p, nkeys_p)
            mix_p = (oa, ob, w_out)
            kp_l.append(k_new[:, FRONT_PAD:]); vp_l.append(v_new[:, FRONT_PAD:]); kip_l.append(ki_new[:, FRONT_PAD:])
            sp_l.append(jnp.swapaxes(sft, 2, 3))
            za, q, kv, qi, slab, kvb = inproj(us, w_in, attn_q_norm[j], attn_k_norm[j], tabs_s, Rs)
            s0 = jnp.swapaxes(state_hgrn[j].astype(F32), 2, 3)
            oa, sft = hgrn(za.reshape(Bs, Ts, A_COLS), par, s0, Ts)
            kv, slab = kv.reshape(Bs, Ts, -1), slab.reshape(Bs, Ts, -1)
            k_new, v_new, ki_new = kv[..., :B_DH], kv[..., B_DH:], slab[..., :IDX_DIM]
            wi = slab[..., IDX_DIM:IDX_DIM + IDX_HEADS]
            padq = lambda a: jnp.pad(a.reshape(Bs, Ts, -1), ((0, 0), (0, QB - Ts), (0, 0)))
            cat = lambda past, new: jnp.concatenate([past.astype(BF16), new], axis=1)
            kb, vb, kib = split_b(kvb.reshape(Bs, Ts, -1))
            ob = dsa(padq(q), padq(qi), padq(wi), cat(cache_k[j], kb), cat(cache_v[j], vb),
                     cat(cache_ki[j], kib), qchunk_s, kchunk_s, nkeys_s)
            mix_s = (oa.reshape(1, Rs, A_VW), ob[:, :Ts].reshape(1, Rs, B_QW), w_out)
            ks_l.append(k_new); vs_l.append(v_new); kis_l.append(ki_new)
            ss_l.append(jnp.swapaxes(sft, 2, 3).astype(state_hgrn.dtype))
        else:
            pw = cast(pool_w[j])
            pool_p = (norm_mix[l], pw, pool_scale[j], -FRONT_PAD)
            us3 = us.reshape(Bs, Ts, D)
            hist = state_pool[j].astype(F32)
            halo = jnp.pad(hist, ((0, 0), (HALO - POOL_HIST, 0), (0, 0)))
            hs = pool(hs.reshape(Bs, Ts, D), us3, halo, pw, pool_scale[j], Ts, P, False).reshape(1, Rs, D)
            ps_l.append(jnp.concatenate([hist, us3], axis=1)[:, -POOL_HIST:])
            mix_p = mix_s = None
        hp = ffn(hp, norm_ffn2[l], *ffn2_w, l, tm=tm_f, mix=mix_p, pool=pool_p)
        if pool_p is not None:
            hp, hist_p = hp
            pp_l.append(hist_p[:, HALO - POOL_HIST:])
        hs = ffn(hs, norm_ffn2[l], *ffn2_w, l, tm=Rs, mix=mix_s)

    y_prompt = hp[:, FRONT_PAD + N_META:]
    return (y_prompt, hs.reshape(Bs, Ts, D),
            jnp.stack(kp_l), jnp.stack(vp_l), jnp.stack(kip_l), jnp.stack(sp_l), jnp.stack(pp_l),
            jnp.stack(ks_l), jnp.stack(vs_l), jnp.stack(kis_l), jnp.stack(ss_l), jnp.stack(ps_l))
```

```python
import functools

import numpy as np
import jax
import jax.numpy as jnp
from jax import lax
from jax.experimental import pallas as pl
from jax.experimental.pallas import tpu as pltpu

F32 = jnp.float32
BF16 = jnp.bfloat16

D_MODEL = 1024
D_FF = 2816
DEPTH = 4
CHUNK = 64
N_META = 16
A_HEADS = 4
A_DK = 128
A_VW = 512
B_HEADS = 4
B_DH = 128
B_QW = 512
IDX_HEADS = 4
IDX_DIM = 64
TOPK = 256
ROPE_THETA = 500000.0
ROT_FRAC = 4
POOL_WINDOWS = (2, 4, 8, 16)
POOL_GROUP = D_MODEL // len(POOL_WINDOWS)
POOL_HIST = max(POOL_WINDOWS) - 1
RMS_EPS = 1e-6
NEG_BIG = -1e30

LANES = 128
SUBLANES = 8
BF16_ROWS = 16
MXU_N = 256
FFN_CHUNKS = 2
FFN_ROW_TILES = 2
CAST_ROWS = 256
KT = 256
QB = 256
NO_CHUNK = 2 ** 30
NO_QUERY = -2
V_ROWS = B_DH + BF16_ROWS
SCORE_UNROLL = 4
NEVER = 3e38
LOG2E = 1.4426950408889634
HALO = 16
FRONT_PAD = 2 * CHUNK - N_META
IN_COLS_PAD = 3200
A_COLS = 2048
VMEM_LIMIT = 56 * 1024 * 1024
HI_MASK = -65536
MIN_NORMAL_BITS = 0x00800000


def _cparams(sem):
    return pltpu.CompilerParams(dimension_semantics=sem, vmem_limit_bytes=VMEM_LIMIT)


def _sigmoid(x):
    return 1.0 / (1.0 + jnp.exp(-x))


def _rms(x, g):
    return x * lax.rsqrt(jnp.mean(x * x, axis=-1, keepdims=True) + RMS_EPS) * g


def _ffn_kernel(*refs, row_tiles, ff_chunks, emit_u, mix, pool_pos0):
    refs = list(refs)
    x_ref = refs.pop(0)
    if mix:
        oa_ref, ob_ref, wo_ref = refs.pop(0), refs.pop(0), refs.pop(0)
    if pool_pos0 is not None:
        halo_ref, gm_ref, pw_ref, psc_ref = (refs.pop(0) for _ in range(4))
        hist_ref = refs.pop()
    g_ref, wg_ref, wu_ref, wd_ref = refs[:4]
    if emit_u:
        g2_ref, y_ref, u_ref = refs[4:]
    else:
        (y_ref,) = refs[4:]
    if pool_pos0 is not None:
        i = pl.program_id(1)
        tm = x_ref.shape[0]
        u_all = _rms(x_ref[...], gm_ref[...])
        halo0 = _rms(jnp.where(i > 0, halo_ref[...], 0.0), gm_ref[...])
        hist_ref[...] = u_all[tm - HALO:tm, :]
    for r0, r1 in row_tiles:
        x = x_ref[r0:r1, :]
        if pool_pos0 is not None:
            halo = halo0 if r0 == 0 else u_all[r0 - HALO:r0, :]
            pos = pool_pos0 + i * tm + r0 + lax.broadcasted_iota(jnp.int32, (r1 - r0, 1), 0)
            x = x + _pool_mix(u_all[r0:r1, :], halo, pw_ref, psc_ref[...], pos)
        if mix:
            x = x + jnp.dot(oa_ref[r0:r1, :], wo_ref[0:A_VW, :], preferred_element_type=F32)
            x = x + jnp.dot(ob_ref[r0:r1, :], wo_ref[A_VW:A_VW + B_QW, :], preferred_element_type=F32)
        xn = _rms(x, g_ref[...]).astype(BF16)
        acc = None
        for c0, c1 in ff_chunks:
            gate = jnp.dot(xn, wg_ref[:, c0:c1], preferred_element_type=F32)
            up = jnp.dot(xn, wu_ref[:, c0:c1], preferred_element_type=F32)
            act = (gate * _sigmoid(gate) * up).astype(BF16)
            part = jnp.dot(act, wd_ref[c0:c1, :], preferred_element_type=F32)
            acc = part if acc is None else acc + part
        y = x + 0.5 * acc
        y_ref[r0:r1, :] = y
        if emit_u:
            u_ref[r0:r1, :] = _rms(y, g2_ref[...]).astype(u_ref.dtype)


def _ffn_tiling(tm, F):
    edges = sorted({min(tm, -(-(k * tm) // (FFN_ROW_TILES * BF16_ROWS)) * BF16_ROWS) for k in range(FFN_ROW_TILES + 1)})
    row_tiles = list(zip(edges[:-1], edges[1:]))
    n_col = -(-F // MXU_N)
    per = -(-n_col // FFN_CHUNKS)
    edges = [min(c * per * MXU_N, F) for c in range(FFN_CHUNKS + 1)]
    return row_tiles, [(a, b) for a, b in zip(edges[:-1], edges[1:]) if b > a]


def ffn(x, g, wg, wu, wd, layer, g2=None, u_dtype=BF16, tm=528, mix=None, pool=None):
    G, R, D = x.shape
    F = wg.shape[2]
    tm = min(tm, R)
    assert R % tm == 0
    row_tiles, ff_chunks = _ffn_tiling(tm, F)
    emit_u = g2 is not None
    row = lambda c: pl.BlockSpec((None, tm, c), lambda b, i: (b, i, 0))
    vec = pl.BlockSpec((1, D), lambda b, i: (0, 0))
    resident = lambda shape: pl.BlockSpec(shape, lambda b, i: (0,) * len(shape), pipeline_mode=pl.Buffered(1))
    in_specs, args = [row(D)], [x]
    if mix is not None:
        oa, ob, w_out = mix
        in_specs += [row(A_VW), row(B_QW), resident(w_out.shape)]
        args += [oa, ob, w_out]
    if pool is not None:
        g_mix, w_grp, scale, pos0 = pool
        assert tm % HALO == 0
        step = tm // HALO
        in_specs += [pl.BlockSpec((None, HALO, D), lambda b, i: (b, jnp.maximum(i * step - 1, 0), 0)),
                     vec, resident(w_grp.shape), vec]
        args += [x, g_mix.reshape(1, D), w_grp, scale.reshape(1, D)]
    stacked = lambda r, c: pl.BlockSpec((None, r, c), lambda b, i: (layer, 0, 0), pipeline_mode=pl.Buffered(1))
    in_specs += [vec, stacked(D, F), stacked(D, F), stacked(F, D)]
    args += [g.reshape(1, D), wg, wu, wd]
    row = row(D)
    out_shape = [jax.ShapeDtypeStruct((G, R, D), F32)]
    out_specs = [row]
    if emit_u:
        in_specs.append(vec)
        args.append(g2.reshape(1, D))
        out_shape.append(jax.ShapeDtypeStruct((G, R, D), u_dtype))
        out_specs.append(row)
    if pool is not None:
        out_shape.append(jax.ShapeDtypeStruct((G, HALO, D), F32))
        out_specs.append(pl.BlockSpec((None, HALO, D), lambda b, i: (b, 0, 0)))
    out = pl.pallas_call(
        functools.partial(_ffn_kernel, row_tiles=row_tiles, ff_chunks=ff_chunks, emit_u=emit_u,
                          mix=mix is not None, pool_pos0=None if pool is None else pool[3]),
        grid=(G, R // tm),
        in_specs=in_specs,
        out_specs=out_specs,
        out_shape=out_shape,
        compiler_params=_cparams(("parallel", "arbitrary" if pool is not None else "parallel")),
        name="ffn",
    )(*args)
    return tuple(out) if len(out) > 1 else out[0]


def _rope(x, c, s_lo, s_hi, half):
    return x * c + pltpu.roll(x, half, 1) * s_lo + pltpu.roll(x, LANES - half, 1) * s_hi


def _inproj_kernel(u_ref, w_ref, qg_ref, kg_ref, cb_ref, sbl_ref, sbh_ref, ci_ref, sil_ref, sih_ref,
                   za_ref, q_ref, kv_ref, qi_ref, slab_ref, kvb_ref):
    u = u_ref[...]
    hb = B_DH // ROT_FRAC // 2
    hi = IDX_DIM // ROT_FRAC // 2
    cb, sbl, sbh = cb_ref[...], sbl_ref[...], sbh_ref[...]
    ci, sil, sih = ci_ref[...], sil_ref[...], sih_ref[...]
    zq = jnp.dot(u, w_ref[:, A_COLS:A_COLS + B_QW], preferred_element_type=F32)
    for h in range(B_HEADS):
        xh = _rms(zq[:, h * B_DH:(h + 1) * B_DH], qg_ref[...])
        q_ref[:, h * B_DH:(h + 1) * B_DH] = _rope(xh, cb, sbl, sbh, hb).astype(q_ref.dtype)
    c0 = A_COLS + B_QW
    zkv = jnp.dot(u, w_ref[:, c0:c0 + 2 * B_DH], preferred_element_type=F32)
    k_new = _rope(_rms(zkv[:, 0:B_DH], kg_ref[...]), cb, sbl, sbh, hb)
    kv_ref[:, 0:B_DH] = k_new
    kv_ref[:, B_DH:2 * B_DH] = zkv[:, B_DH:2 * B_DH]
    kvb_ref[:, 0:B_DH] = k_new.astype(BF16)
    kvb_ref[:, B_DH:2 * B_DH] = zkv[:, B_DH:2 * B_DH].astype(BF16)
    c1 = c0 + 2 * B_DH
    zi = jnp.dot(u, w_ref[:, c1:c1 + IDX_HEADS * IDX_DIM + LANES], preferred_element_type=F32)
    for t in range(IDX_HEADS * IDX_DIM // LANES):
        qi_ref[:, t * LANES:(t + 1) * LANES] = _rope(zi[:, t * LANES:(t + 1) * LANES], ci, sil, sih, hi).astype(qi_ref.dtype)
    tail = zi[:, IDX_HEADS * IDX_DIM:]
    lane = lax.broadcasted_iota(jnp.int32, tail.shape, 1)
    slab = jnp.where(lane < IDX_DIM, _rope(tail, ci, sil, sih, hi), tail)
    slab_ref[...] = slab
    kvb_ref[:, 2 * B_DH:2 * B_DH + LANES] = slab.astype(BF16)
    za_ref[...] = jnp.dot(u, w_ref[:, 0:A_COLS], preferred_element_type=F32)


def inproj(u, w_pad, q_gain, k_gain, tabs, tm):
    G, R, D = u.shape
    assert R % tm == 0
    row = lambda c: pl.BlockSpec((None, tm, c), lambda b, i: (b, i, 0))
    tab = pl.BlockSpec((tm, LANES), lambda b, i: (i, 0))
    vec = pl.BlockSpec((1, LANES), lambda b, i: (0, 0))
    return pl.pallas_call(
        _inproj_kernel,
        grid=(G, R // tm),
        in_specs=[row(D), pl.BlockSpec((D, IN_COLS_PAD), lambda b, i: (0, 0)), vec, vec] + [tab] * 6,
        out_specs=[row(A_COLS), row(B_QW), row(2 * B_DH), row(IDX_HEADS * IDX_DIM), row(LANES),
                   row(2 * B_DH + LANES)],
        out_shape=[jax.ShapeDtypeStruct((G, R, A_COLS), F32),
                   jax.ShapeDtypeStruct((G, R, B_QW), BF16),
                   jax.ShapeDtypeStruct((G, R, 2 * B_DH), F32),
                   jax.ShapeDtypeStruct((G, R, IDX_HEADS * IDX_DIM), BF16),
                   jax.ShapeDtypeStruct((G, R, LANES), F32),
                   jax.ShapeDtypeStruct((G, R, 2 * B_DH + LANES), BF16)],
        compiler_params=_cparams(("parallel", "parallel")),
        name="inproj",
    )(u, w_pad, q_gain.reshape(1, LANES), k_gain.reshape(1, LANES), *tabs)


def rope_tables(pos):
    pos = pos.astype(F32)
    out = []
    for d in (B_DH, IDX_DIM):
        half = d // ROT_FRAC // 2
        inv = ROPE_THETA ** (-jnp.arange(half, dtype=F32) / half)
        ang = pos[:, None] * inv[None, :]
        cos, sin = jnp.cos(ang), jnp.sin(ang)
        n = pos.shape[0]
        z = lambda w: jnp.zeros((n, w), F32)
        c = jnp.concatenate([cos, cos, jnp.ones((n, d - 2 * half), F32)], axis=1)
        s_lo = jnp.concatenate([z(half), sin, z(d - 2 * half)], axis=1)
        s_hi = jnp.concatenate([-sin, z(d - half)], axis=1)
        rep = LANES // d
        out += [jnp.tile(c, (1, rep)), jnp.tile(s_lo, (1, rep)), jnp.tile(s_hi, (1, rep))]
    return out


def _hgrn_consts(C):
    levels = []
    m = C // 2
    while m >= 1:
        levels.append(m)
        m //= 2
    t = np.arange(C)[:, None]
    r = np.arange(C)[None, :]
    mats = [(r <= t)]
    lvl = np.full((C, C), -1, np.int32)
    lvl[np.arange(C), np.arange(C)] = len(levels)
    for li, m in enumerate(levels):
        anchor = (t // (2 * m)) * (2 * m) + m - 1
        second = t > anchor
        mats.append(np.where(second, (r > anchor) & (r <= t), (r > t) & (r <= anchor)))
        same = (t // (2 * m)) == (r // (2 * m))
        split = same & ((t % (2 * m)) >= m) & ((r % (2 * m)) < m)
        lvl[split] = li
    dmat = np.concatenate([x.astype(np.float32) for x in mats], axis=0)
    return levels, jnp.asarray(dmat, BF16), jnp.asarray(lvl)


def _nt(a, b):
    return lax.dot_general(a, b, (((1,), (1,)), ((), ())), preferred_element_type=F32)


def _tn(a, b):
    return lax.dot_general(a, b, (((0,), (0,)), ((), ())), preferred_element_type=F32)


def _hgrn_kernel(zf_ref, zn_ref, zv_ref, d_ref, lvl_ref, par_ref, s0_ref, oa_ref, sf_ref,
                 st_ref, qa_ref, ka_ref, g_ref, *, C, levels):
    c = pl.program_id(1)
    dk = A_DK
    W = A_HEADS * dk
    log_lb, log_1m, one_m, lb_pos, o_gain = (par_ref[i:i + 1, :] for i in range(5))

    def stash_gates(z_ref, slot):
        aq = z_ref[:, 0:W]
        af = z_ref[:, W:2 * W]
        e = jnp.exp(-jnp.abs(af))
        ls = jnp.minimum(af, 0.0) - jnp.log1p(e)
        b2 = log_1m + ls
        lae = jnp.maximum(log_lb, b2) + jnp.log1p(jnp.exp(-jnp.abs(log_lb - b2)))
        log_f = jnp.where(lb_pos > 0.0, lae, ls)
        ka_ref[slot] = one_m * (jnp.where(af >= 0.0, e, 1.0) / (1.0 + e))
        qa_ref[slot] = aq * _sigmoid(aq)
        g_hi = log_f.astype(BF16)
        r1 = log_f - g_hi.astype(F32)
        g_mid = r1.astype(BF16)
        g_ref[slot, 0] = g_hi
        g_ref[slot, 1] = g_mid
        g_ref[slot, 2] = (r1 - g_mid.astype(F32)).astype(BF16)

    @pl.when(c == 0)
    def _():
        st_ref[...] = s0_ref[...]
        stash_gates(zf_ref, 0)

    cur = c % 2
    qa, ka = qa_ref[cur], ka_ref[cur]
    ai = zv_ref[:, 0:W]
    ag = zv_ref[:, W:2 * W]
    dm = d_ref[...]
    X = (jnp.dot(dm, g_ref[cur, 0], preferred_element_type=F32) + jnp.dot(dm, g_ref[cur, 1], preferred_element_type=F32)
         + jnp.dot(dm, g_ref[cur, 2], preferred_element_type=F32))
    lvl = lvl_ref[...]
    rows = lax.broadcasted_iota(jnp.int32, (C, 1), 0)
    nl = len(levels)
    b = X[0:C, :]
    b_last = b[C - 1:C, :]
    qe = (qa * jnp.exp(b)).astype(BF16)
    ke = (ka * jnp.exp(b_last - b)).astype(BF16)
    decay = jnp.exp(b_last)
    qb, kb, vb = qa.astype(BF16), ka.astype(BF16), ai.astype(BF16)
    zs = [(jnp.where((rows // m) % 2 == 1, qa, ka) * jnp.exp(X[(1 + li) * C:(2 + li) * C, :])).astype(BF16)
          for li, m in enumerate(levels)]
    gate = o_gain * _sigmoid(ag)
    for h in range(A_HEADS):
        sl = slice(h * dk, (h + 1) * dk)
        st = st_ref[h]
        o = _nt(qe[:, sl], st.astype(BF16))
        att = jnp.where(lvl == nl, _nt(qb[:, sl], kb[:, sl]), 0.0)
        for li in range(nl):
            att = att + jnp.where(lvl == li, _nt(zs[li][:, sl], zs[li][:, sl]), 0.0)
        o = o + jnp.dot(att.astype(BF16), vb[:, sl], preferred_element_type=F32)
        st_ref[h] = st * decay[:, sl] + _tn(vb[:, sl], ke[:, sl])
        on = o * lax.rsqrt(jnp.mean(o * o, axis=-1, keepdims=True) + RMS_EPS)
        oa_ref[:, sl] = (on * gate[:, sl]).astype(oa_ref.dtype)

    stash_gates(zn_ref, 1 - cur)

    @pl.when(c == pl.num_programs(1) - 1)
    def _():
        sf_ref[...] = st_ref[...]


def hgrn(za, par, s0t, C):
    B, T, _ = za.shape
    assert T % C == 0
    levels, dmat, lvl = _hgrn_consts(C)
    W = A_HEADS * A_DK
    st_spec = pl.BlockSpec((None, A_HEADS, A_DK, A_DK), lambda b, c: (b, 0, 0, 0))
    nc = T // C
    half = lambda index: pl.BlockSpec((None, C, 2 * W), index)
    return pl.pallas_call(
        functools.partial(_hgrn_kernel, C=C, levels=levels),
        grid=(B, nc),
        in_specs=[half(lambda b, c: (b, 0, 0)),
                  half(lambda b, c: (b, jnp.minimum(c + 1, nc - 1), 0)),
                  half(lambda b, c: (b, c, 1)),
                  pl.BlockSpec(dmat.shape, lambda b, c: (0, 0)),
                  pl.BlockSpec((C, C), lambda b, c: (0, 0)),
                  pl.BlockSpec((8, W), lambda b, c: (0, 0)),
                  st_spec],
        out_specs=[pl.BlockSpec((None, C, W), lambda b, c: (b, c, 0)), st_spec],
        out_shape=[jax.ShapeDtypeStruct((B, T, W), BF16),
                   jax.ShapeDtypeStruct((B, A_HEADS, A_DK, A_DK), F32)],
        scratch_shapes=[pltpu.VMEM((A_HEADS, A_DK, A_DK), F32),
                        pltpu.VMEM((2, C, W), F32), pltpu.VMEM((2, C, W), F32), pltpu.VMEM((2, 3, C, W), BF16)],
        compiler_params=_cparams(("parallel", "arbitrary")),
        name="hgrn",
    )(za, za, za, dmat, lvl, par, s0t)


def _tree_sum(parts):
    while len(parts) > 1:
        parts = [parts[r] + parts[r + 1] if r + 1 < len(parts) else parts[r] for r in range(0, len(parts), 2)]
    return parts[0]


def _fori_pairs(n, tile, init):
    acc = lax.fori_loop(0, n // 2, lambda p, a: a + (tile(2 * p) + tile(2 * p + 1)), init)
    return lax.fori_loop(n // 2 * 2, n, lambda j, a: a + tile(j), acc)


def _dsa_kernel(nkb_ref,qt_ref, qit_ref, wit_ref, qc_ref, kc_ref, k_ref, vt_ref, ki_ref, tri_ref,
                ob_ref, key_ref, hi_ref, acc_ref, *, topk):
    i = pl.program_id(1)
    nk = nkb_ref[i]
    qc = qc_ref[...]
    qit = qit_ref[...]
    rhs_i = jnp.concatenate([qit[h * IDX_DIM:(h + 1) * IDX_DIM, :] for h in range(IDX_HEADS)], axis=1)
    qt = qt_ref[...]
    rhs_q = jnp.concatenate([qt[h * B_DH:(h + 1) * B_DH, :] for h in range(B_HEADS)], axis=1)
    wsc = wit_ref[...] * (IDX_HEADS ** -0.5 * IDX_DIM ** -0.5)

    def score_tile(j):
        sc = jnp.dot(ki_ref[j], rhs_i, preferred_element_type=F32)
        s = jnp.maximum(sc[:, 0:QB], 0.0) * wsc[0:1, :]
        for h in range(1, IDX_HEADS):
            s = s + jnp.maximum(sc[:, h * QB:(h + 1) * QB], 0.0) * wsc[h:h + 1, :]
        s = jnp.where(kc_ref[j] <= qc, s, NEG_BIG)
        bits = lax.bitcast_convert_type(s, jnp.int32)
        bits = jnp.where((bits & 0x7FFFFFFF) < MIN_NORMAL_BITS, 0, bits)
        key_ref[j] = bits ^ ((bits >> 31) & 0x7FFFFFFF)
        hi_ref[j] = lax.bitcast_convert_type(bits & HI_MASK, F32).astype(BF16)

    @pl.loop(0, nk // SCORE_UNROLL)
    def _(p):
        for r in range(SCORE_UNROLL):
            score_tile(SCORE_UNROLL * p + r)

    done = nk // SCORE_UNROLL * SCORE_UNROLL

    @pl.loop(0, (nk - done) // 2)
    def _(p):
        score_tile(done + 2 * p)
        score_tile(done + 2 * p + 1)

    @pl.loop(done + (nk - done) // 2 * 2, nk)
    def _(j):
        score_tile(j)

    def count_hi(cand16):
        cand16 = jnp.where((cand16 >= 1) & (cand16 < MIN_NORMAL_BITS >> 16), MIN_NORMAL_BITS >> 16, cand16)
        pat = cand16 ^ ((cand16 >> 15) & 0x7FFF)
        cand = lax.bitcast_convert_type(jnp.left_shift(pat, 16), F32).astype(BF16)

        def tile(j):
            hit = jnp.where(hi_ref[j] >= cand, jnp.bfloat16(1), jnp.bfloat16(0))
            parts = [hit[r * BF16_ROWS:(r + 1) * BF16_ROWS, :] for r in range(KT // BF16_ROWS)]
            return _tree_sum(parts).astype(F32)
        acc = _fori_pairs(nk, tile, jnp.zeros((BF16_ROWS, QB), F32))
        return jnp.sum(acc, axis=0, keepdims=True).astype(jnp.int32)

    def count(pred):
        def tile(j):
            hit = jnp.where(pred(key_ref[j]), 1, 0)
            return _tree_sum([hit[r * SUBLANES:(r + 1) * SUBLANES, :] for r in range(KT // SUBLANES)])
        acc = _fori_pairs(nk, tile, jnp.zeros((SUBLANES, QB), jnp.int32))
        return jnp.sum(acc, axis=0, keepdims=True)

    c0 = count_hi(jnp.zeros((1, QB), jnp.int32))
    hi0 = jnp.where(c0 >= topk, 0, -(2 ** 15)).astype(jnp.int32)
    above0 = jnp.where(c0 >= topk, 0, c0)

    def hi_body(it, carry):
        thr, above = carry
        cand = thr | jnp.left_shift(jnp.int32(1), 14 - it)
        cnt = count_hi(cand)
        ok = cnt >= topk
        return jnp.where(ok, cand, thr), jnp.where(ok, above, cnt)

    hi16, above1 = lax.fori_loop(0, 15, hi_body, (hi0, above0))

    def lo_body(it, carry):
        thr, above = carry
        cand = thr | jnp.left_shift(jnp.int32(1), 15 - it)
        cnt = count(lambda kv: kv >= cand)
        ok = cnt >= topk
        return jnp.where(ok, cand, thr), jnp.where(ok, above, cnt)

    thr, above = lax.fori_loop(0, 16, lo_body, (jnp.left_shift(hi16, 16), above1))
    need = (topk - above).astype(F32)

    acc_ref[...] = jnp.zeros_like(acc_ref)
    c2 = B_DH ** -0.5 * LOG2E
    tri = tri_ref[...]

    def att_step(tiles, carry):
        ms, tie_seen = carry
        sels, s_alls = [], []
        for j in tiles:
            kv = key_ref[j]
            eq = kv == thr
            eqf = jnp.where(eq, 1.0, 0.0)
            before = jnp.dot(tri, eqf.astype(BF16), preferred_element_type=F32) + tie_seen
            rank = jnp.where(kv > thr, -1.0, jnp.where(eq, before, NEVER))
            sels.append(jnp.where(jnp.where(kc_ref[j] <= qc, rank, NEVER) < need, 0.0, NEG_BIG))
            tie_seen = tie_seen + jnp.sum(eqf, axis=0, keepdims=True)
            s_alls.append(jnp.dot(k_ref[j], rhs_q, preferred_element_type=F32))
        new_ms, alphas = [], []
        ps = [[] for _ in tiles]
        for h in range(B_HEADS):
            ss = [s_all[:, h * QB:(h + 1) * QB] + sel for sel, s_all in zip(sels, s_alls)]
            m_new = ms[h]
            for s in ss:
                m_new = jnp.maximum(m_new, jnp.max(s, axis=0, keepdims=True))
            alphas.append(jnp.exp2((ms[h] - m_new) * c2))
            new_ms.append(m_new)
            for t, s in enumerate(ss):
                ps[t].append(jnp.exp2((s - m_new) * c2).astype(BF16))
        pv = None
        for t, j in enumerate(tiles):
            part = jnp.dot(vt_ref[j], jnp.concatenate(ps[t], axis=1), preferred_element_type=F32)
            pv = part if pv is None else pv + part
        for h in range(B_HEADS):
            acc_ref[h] = alphas[h] * acc_ref[h] + pv[:, h * QB:(h + 1) * QB]
        return tuple(new_ms), tie_seen

    init_m = tuple(jnp.full((1, QB), NEG_BIG, F32) for _ in range(B_HEADS))
    carry = (init_m, jnp.zeros((1, QB), F32))
    carry = lax.fori_loop(0, nk // 2, lambda p, c: att_step([2 * p, 2 * p + 1], c), carry)
    ms, _ = lax.fori_loop(nk // 2 * 2, nk, lambda j, c: att_step([j], c), carry)
    for h in range(B_HEADS):
        seen = ms[h] > 0.5 * NEG_BIG
        inv = jnp.where(seen, 1.0 / jnp.where(seen, acc_ref[h, B_DH:B_DH + 1, :], 1.0), 0.0)
        ob_ref[:, h * B_DH:(h + 1) * B_DH] = (acc_ref[h, 0:B_DH, :] * inv).T.astype(ob_ref.dtype)


def dsa(q, qi, wi, k_all, v_all, ki_all, q_chunk, k_chunk, nkeys, topk=TOPK):
    B, Tq, _ = q.shape
    Tk = k_all.shape[1]
    assert Tq % LANES == 0 and QB % LANES == 0
    nq, nkt = -(-Tq // QB), -(-Tk // KT)
    padq = lambda a: jnp.pad(a, ((0, 0), (0, nq * QB - Tq), (0, 0)))
    padk = lambda a: jnp.pad(a, ((0, 0), (0, nkt * KT - Tk), (0, 0)))
    qt = jnp.swapaxes(padq(q), 1, 2)
    qit = jnp.swapaxes(padq(qi), 1, 2)
    wit = jnp.swapaxes(padq(wi), 1, 2)
    k4 = padk(k_all.astype(BF16)).reshape(B, nkt, KT, B_DH)
    vt4 = jnp.swapaxes(padk(v_all.astype(BF16)).reshape(B, nkt, KT, B_DH), 2, 3)
    ones_rows = jnp.zeros((B, nkt, V_ROWS - B_DH, KT), BF16).at[:, :, 0, :].set(1.0)
    vt4 = jnp.concatenate([vt4, ones_rows], axis=2)
    ki4 = padk(ki_all.astype(BF16)).reshape(B, nkt, KT, IDX_DIM)
    kc = jnp.pad(k_chunk.astype(jnp.int32), (0, nkt * KT - Tk), constant_values=NO_CHUNK)
    kc = jnp.broadcast_to(kc.reshape(nkt, KT, 1), (nkt, KT, QB))
    qc = jnp.pad(q_chunk.astype(jnp.int32), (0, nq * QB - Tq), constant_values=NO_QUERY).reshape(nq, 1, QB)
    per = QB // LANES
    nkeys = jnp.pad(nkeys.astype(jnp.int32), (0, nq * per - nkeys.shape[0])).reshape(nq, per).max(axis=1)
    nkb = (nkeys + (KT - 1)) // KT
    tri = jnp.asarray(np.tril(np.ones((KT, KT), np.float32), -1), BF16)
    grid_spec = pltpu.PrefetchScalarGridSpec(
        num_scalar_prefetch=1,
        grid=(B, nq),
        in_specs=[pl.BlockSpec((None, B_QW, QB), lambda b, i, n: (b, 0, i)),
                  pl.BlockSpec((None, IDX_HEADS * IDX_DIM, QB), lambda b, i, n: (b, 0, i)),
                  pl.BlockSpec((None, IDX_HEADS, QB), lambda b, i, n: (b, 0, i)),
                  pl.BlockSpec((None, 1, QB), lambda b, i, n: (i, 0, 0)),
                  pl.BlockSpec((nkt, KT, QB), lambda b, i, n: (0, 0, 0)),
                  pl.BlockSpec((None, nkt, KT, B_DH), lambda b, i, n: (b, 0, 0, 0)),
                  pl.BlockSpec((None, nkt, V_ROWS, KT), lambda b, i, n: (b, 0, 0, 0)),
                  pl.BlockSpec((None, nkt, KT, IDX_DIM), lambda b, i, n: (b, 0, 0, 0)),
                  pl.BlockSpec((KT, KT), lambda b, i, n: (0, 0))],
        out_specs=pl.BlockSpec((None, QB, B_QW), lambda b, i, n: (b, i, 0)),
        scratch_shapes=[pltpu.VMEM((nkt, KT, QB), jnp.int32),
                        pltpu.VMEM((nkt, KT, QB), BF16),
                        pltpu.VMEM((B_HEADS, V_ROWS, QB), F32)],
    )
    return pl.pallas_call(
        functools.partial(_dsa_kernel, topk=topk),
        grid_spec=grid_spec,
        out_shape=jax.ShapeDtypeStruct((B, Tq, B_QW), BF16),
        compiler_params=_cparams(("parallel", "arbitrary")),
        name="dsa",
    )(nkb, qt, qit, wit, qc, kc, k4, vt4, ki4, tri)


def _cast_kernel(x_ref, o_ref):
    C = x_ref.shape[1]
    o_ref[:, 0:C] = x_ref[...].astype(o_ref.dtype)
    if o_ref.shape[1] > C:
        o_ref[:, C:] = jnp.zeros((o_ref.shape[0], o_ref.shape[1] - C), o_ref.dtype)


def cast_bf16(w, cols=None):
    L, R, C = w.shape
    cols = C if cols is None else cols
    assert R % CAST_ROWS == 0
    spec = lambda c: pl.BlockSpec((None, CAST_ROWS, c), lambda l, i: (l, i, 0))
    return pl.pallas_call(
        _cast_kernel,
        grid=(L, R // CAST_ROWS),
        in_specs=[spec(C)],
        out_specs=spec(cols),
        out_shape=jax.ShapeDtypeStruct((L, R, cols), BF16),
        compiler_params=_cparams(("parallel", "parallel")),
        name="cast",
    )(w)


def _pool_mix(u, halo, w_ref, sc, pos):
    run = jnp.concatenate([halo, u], axis=0)
    outs = []
    for g, w in enumerate(POOL_WINDOWS):
        lo = g * POOL_GROUP
        if g:
            run = run[:, POOL_GROUP:]
        run = run + pltpu.roll(run, w // 2, 0)
        cnt = jnp.clip(pos + 1, 1, w).astype(F32)
        mean = run[HALO:, 0:POOL_GROUP] / cnt
        d = (mean - u[:, lo:lo + POOL_GROUP]).astype(BF16)
        outs.append(jnp.dot(d, w_ref[g], preferred_element_type=F32))
    return jnp.concatenate(outs, axis=1) * sc


def _pool_kernel(h_ref, u_ref, halo_ref, w_ref, sc_ref, y_ref, *, tm, pos0, zero_first_halo):
    i = pl.program_id(1)
    halo = halo_ref[...]
    if zero_first_halo:
        halo = jnp.where(i > 0, halo, 0.0)
    pos = pos0 + i * tm + lax.broadcasted_iota(jnp.int32, (tm, 1), 0)
    y_ref[...] = h_ref[...] + _pool_mix(u_ref[...], halo, w_ref, sc_ref[...], pos)


def pool(h, u, halo_src, w_grp, scale, tm, pos0, zero_first_halo):
    G, R, D = h.shape
    assert R % tm == 0 and tm % HALO == 0
    row = pl.BlockSpec((None, tm, D), lambda b, i: (b, i, 0))
    step = tm // HALO
    halo = pl.BlockSpec((None, HALO, D), lambda b, i: (b, jnp.maximum(i * step - 1, 0), 0))
    return pl.pallas_call(
        functools.partial(_pool_kernel, tm=tm, pos0=pos0, zero_first_halo=zero_first_halo),
        grid=(G, R // tm),
        in_specs=[row, row, halo,
                  pl.BlockSpec((len(POOL_WINDOWS), POOL_GROUP, POOL_GROUP), lambda b, i: (0, 0, 0)),
                  pl.BlockSpec((1, D), lambda b, i: (0, 0))],
        out_specs=row,
        out_shape=jax.ShapeDtypeStruct((G, R, D), F32),
        compiler_params=_cparams(("parallel", "parallel")),
        name="pool",
    )(h, u, halo_src, w_grp, scale.reshape(1, D))


def _hgrn_params(lb, o_gain):
    lb_pos = lb > 0
    lb_safe = jnp.where(lb_pos, lb, 0.5)
    rows = [jnp.log(lb_safe), jnp.log1p(-lb_safe), 1.0 - lb, lb_pos.astype(F32), o_gain.astype(F32)]
    rows += [jnp.zeros_like(lb)] * 3
    return jnp.stack(rows).astype(F32)


def kernel(x_prompt, x_sample, cache_k, cache_v, cache_ki, state_hgrn, state_pool, meta_tokens, norm_ffn1, ffn1_wg, ffn1_wu, ffn1_wd, norm_mix, norm_ffn2, ffn2_wg, ffn2_wu, ffn2_wd, ab_w_in, ab_w_out, hgrn_lb_logits, hgrn_out_norm, attn_q_norm, attn_k_norm, pool_w, pool_scale):
    Bp, Sp, D = x_prompt.shape
    Bs, Ts, _ = x_sample.shape
    P = cache_k.shape[2]
    Tp = FRONT_PAD + N_META + Sp
    assert Tp % LANES == 0 and Ts == CHUNK and P % LANES == 0
    tm_p = Tp // 8
    Rs = Bs * Ts

    hp = jnp.concatenate([jnp.zeros((Bp, FRONT_PAD, D), F32),
                          jnp.broadcast_to(meta_tokens.astype(F32)[None], (Bp, N_META, D)), x_prompt], axis=1)
    hs = x_sample.reshape(1, Rs, D)

    row_p = jnp.arange(Tp, dtype=jnp.int32)
    pos_p = row_p - FRONT_PAD
    chunk_p = jnp.where(pos_p < 0, -2, jnp.where(pos_p < N_META, -1, (pos_p - N_META) // CHUNK))
    kchunk_p = jnp.where(pos_p < 0, jnp.int32(NO_CHUNK), chunk_p)
    nkeys_p = (jnp.arange(Tp // LANES, dtype=jnp.int32) + 1) * LANES
    tabs_p = rope_tables(pos_p)

    pos_s = P + jnp.arange(Ts, dtype=jnp.int32)
    qchunk_s = jnp.concatenate([pos_s // CHUNK, jnp.full((QB - Ts,), NO_QUERY, jnp.int32)])
    kchunk_s = jnp.arange(P + Ts, dtype=jnp.int32) // CHUNK
    nkeys_s = jnp.full((1,), P + Ts, jnp.int32)
    tabs_s = [jnp.tile(t, (Bs, 1)) for t in rope_tables(pos_s)]

    lb_soft = jax.nn.softmax(hgrn_lb_logits.astype(F32), axis=0)
    lb_all = jnp.cumsum(lb_soft, axis=0) - lb_soft[0]

    cast = lambda w: w.astype(BF16)
    ffn1_w = [cast_bf16(w) for w in (ffn1_wg, ffn1_wu, ffn1_wd)]
    ffn2_w = [cast_bf16(w) for w in (ffn2_wg, ffn2_wu, ffn2_wd)]
    w_in_all = cast_bf16(ab_w_in, IN_COLS_PAD)
    kp_l, vp_l, kip_l, sp_l, pp_l = [], [], [], [], []
    ks_l, vs_l, kis_l, ss_l, ps_l = [], [], [], [], []
    for l in range(DEPTH):
        j = l // 2
        even = l % 2 == 0
        udt = BF16 if even else F32
        if even:
            hp, up = ffn(hp, norm_ffn1[l], *ffn1_w, l, g2=norm_mix[l], u_dtype=udt, tm=tm_p)
        else:
            hp = ffn(hp, norm_ffn1[l], *ffn1_w, l, tm=tm_p)
        hs, us = ffn(hs, norm_ffn1[l], *ffn1_w, l, g2=norm_mix[l], u_dtype=udt, tm=Rs)
        pool_p = None
        if even:
            w_in = w_in_all[j]
            w_out = cast(ab_w_out[j])
            par = _hgrn_params(lb_all[j], hgrn_out_norm[j])
            za, q, kv, qi, slab, kvb = inproj(up, w_in, attn_q_norm[j], attn_k_norm[j], tabs_p, tm_p)
            s0 = jnp.zeros((Bp, A_HEADS, A_DK, A_DK), F32)
            oa, sft = hgrn(za, par, s0, LANES)
            k_new, v_new, ki_new = kv[..., :B_DH], kv[..., B_DH:], slab[..., :IDX_DIM]
            wi = slab[..., IDX_DIM:IDX_DIM + IDX_HEADS]
            split_b = lambda a: (a[..., :B_DH], a[..., B_DH:2 * B_DH], a[..., 2 * B_DH:2 * B_DH + IDX_DIM])
            ob = dsa(q, qi, wi, *split_b(kvb), chunk_p, kchunk_p, nkeys_p)
            mix_p = (oa, ob, w_out)
            kp_l.append(k_new[:, FRONT_PAD:]); vp_l.append(v_new[:, FRONT_PAD:]); kip_l.append(ki_new[:, FRONT_PAD:])
            sp_l.append(jnp.swapaxes(sft, 2, 3))
            za, q, kv, qi, slab, kvb = inproj(us, w_in, attn_q_norm[j], attn_k_norm[j], tabs_s, Rs)
            s0 = jnp.swapaxes(state_hgrn[j].astype(F32), 2, 3)
            oa, sft = hgrn(za.reshape(Bs, Ts, A_COLS), par, s0, Ts)
            kv, slab = kv.reshape(Bs, Ts, -1), slab.reshape(Bs, Ts, -1)
            k_new, v_new, ki_new = kv[..., :B_DH], kv[..., B_DH:], slab[..., :IDX_DIM]
            wi = slab[..., IDX_DIM:IDX_DIM + IDX_HEADS]
            padq = lambda a: jnp.pad(a.reshape(Bs, Ts, -1), ((0, 0), (0, QB - Ts), (0, 0)))
            cat = lambda past, new: jnp.concatenate([past.astype(BF16), new], axis=1)
            kb, vb, kib = split_b(kvb.reshape(Bs, Ts, -1))
            ob = dsa(padq(q), padq(qi), padq(wi), cat(cache_k[j], kb), cat(cache_v[j], vb),
                     cat(cache_ki[j], kib), qchunk_s, kchunk_s, nkeys_s)
            mix_s = (oa.reshape(1, Rs, A_VW), ob[:, :Ts].reshape(1, Rs, B_QW), w_out)
            ks_l.append(k_new); vs_l.append(v_new); kis_l.append(ki_new)
            ss_l.append(jnp.swapaxes(sft, 2, 3).astype(state_hgrn.dtype))
        else:
            pw = cast(pool_w[j])
            pool_p = (norm_mix[l], pw, pool_scale[j], -FRONT_PAD)
            us3 = us.reshape(Bs, Ts, D)
            hist = state_pool[j].astype(F32)
            halo = jnp.pad(hist, ((0, 0), (HALO - POOL_HIST, 0), (0, 0)))
            hs = pool(hs.reshape(Bs, Ts, D), us3, halo, pw, pool_scale[j], Ts, P, False).reshape(1, Rs, D)
            ps_l.append(jnp.concatenate([hist, us3], axis=1)[:, -POOL_HIST:])
            mix_p = mix_s = None
        hp = ffn(hp, norm_ffn2[l], *ffn2_w, l, tm=tm_p, mix=mix_p, pool=pool_p)
        if pool_p is not None:
            hp, hist_p = hp
            pp_l.append(hist_p[:, HALO - POOL_HIST:])
        hs = ffn(hs, norm_ffn2[l], *ffn2_w, l, tm=Rs, mix=mix_s)

    y_prompt = hp[:, FRONT_PAD + N_META:]
    return (y_prompt, hs.reshape(Bs, Ts, D),
            jnp.stack(kp_l), jnp.stack(vp_l), jnp.stack(kip_l), jnp.stack(sp_l), jnp.stack(pp_l),
            jnp.stack(ks_l), jnp.stack(vs_l), jnp.stack(kis_l), jnp.stack(ss_l), jnp.stack(ps_l))
```
